```python
import math
import jax
import jax.numpy as jnp
from jax import lax
import numpy as np

D_MODEL = 1024
BATCH = 2
SEQ = 8192
DEPTH = 4
DEC_BATCH = 128
DEC_SEQ = 4
PAST_LEN = 2048
PAGE_SIZE = 128

N_EVEN = (DEPTH + 1) // 2
N_ODD = DEPTH // 2
GLA_HEADS = 4
GLA_DK = 64
GLA_DV = 128
GLA_GATE_RANK = 16
GLA_TAU = 16.0
GLA_CHUNK = 64
RW_HEADS = 8
RW_DH = 64
RW_W_LORA = 32
RW_A_LORA = 32
RW_G_LORA = 96
RW_LN_EPS = 64e-5
ATT_HEADS = 16
ATT_KV_HEADS = 4
ATT_DH = 64
IDX_HEADS = 8
IDX_DH = 64
TOPK_MAX = 256
Q_BLOCK = 128
ROPE_THETA = 10000.0
D_FF = 2816
N_EXPERTS = 8
TOP_K_EXPERTS = 2
D_FF_EXPERT = 1408
EPS = 1e-6
GLA_QK = GLA_HEADS * GLA_DK
GLA_V = GLA_HEADS * GLA_DV
RW_W = RW_HEADS * RW_DH
P_A = 2 * GLA_QK + 2 * GLA_V + GLA_GATE_RANK
P_B = 3 * RW_W + RW_W_LORA + RW_A_LORA + RW_G_LORA
P_EVEN = P_A + P_B
MIX_EVEN = GLA_V + RW_W
ATT_Q = ATT_HEADS * ATT_DH
ATT_KV = ATT_KV_HEADS * ATT_DH
P_ODD = ATT_Q + 2 * ATT_KV + IDX_HEADS * IDX_DH + IDX_DH + IDX_HEADS

kernel_name = 'hybrid_gla_rwkv7_dsa_adaln_step'


def rms_norm(x, eps=EPS):
    xf = x.astype(jnp.float32)
    return (xf * lax.rsqrt(jnp.mean(xf * xf, axis=-1, keepdims=True) + eps)).astype(x.dtype)


def layer_norm(x, g, b, eps):
    xf = x.astype(jnp.float32)
    mu = jnp.mean(xf, axis=-1, keepdims=True)
    var = jnp.mean(jnp.square(xf - mu), axis=-1, keepdims=True)
    return ((xf - mu) * lax.rsqrt(var + eps)).astype(x.dtype) * g + b


def modulate(x, shift, scale):
    return rms_norm(x) * (1 + scale[:, None, :]) + shift[:, None, :]


def rotary(x, pos):
    half = x.shape[-1] // 2
    inv_freq = jnp.power(ROPE_THETA, -jnp.arange(half, dtype=jnp.float32) / half)
    ang = pos.astype(jnp.float32)[:, None] * inv_freq[None, :]
    cos = jnp.cos(ang)[None, :, None, :]
    sin = jnp.sin(ang)[None, :, None, :]
    xf = x.astype(jnp.float32)
    x1, x2 = xf[..., :half], xf[..., half:]
    return jnp.concatenate([x1 * cos - x2 * sin, x2 * cos + x1 * sin], axis=-1).astype(x.dtype)


def gla_chunked(q, k, v, log_a, s0):
    f32 = jnp.float32
    bt, L, H, _ = q.shape
    dv = v.shape[-1]
    C = min(GLA_CHUNK, L)
    n = -(-L // C)
    pad = n * C - L

    def prep(t):
        t = jnp.pad(t.astype(f32), ((0, 0), (0, pad), (0, 0), (0, 0)))
        return t.reshape(bt, n, C, H, t.shape[-1]).transpose(1, 0, 3, 2, 4)

    qc, kc, vc, gc = prep(q), prep(k), prep(v), prep(log_a)
    bcum = jnp.cumsum(gc, axis=3)
    blast = bcum[:, :, :, -1:, :]
    q_e = qc * jnp.exp(bcum)
    k_e = kc * jnp.exp(-bcum)
    k_end = kc * jnp.exp(blast - bcum)
    causal = jnp.tril(jnp.ones((C, C), dtype=bool))
    att = jnp.where(causal, jnp.einsum('nbhtd,nbhsd->nbhts', q_e, k_e), 0.0)
    o_intra = jnp.einsum('nbhts,nbhsv->nbhtv', att, vc)
    decay = jnp.exp(blast[:, :, :, 0, :])
    kv = jnp.einsum('nbhsd,nbhsv->nbhdv', k_end, vc)

    def step(s, inp):
        dec, kv_i = inp
        return dec[..., None] * s + kv_i, s

    s_fin, s_start = lax.scan(step, s0.astype(f32), (decay, kv))
    o = o_intra + jnp.einsum('nbhtd,nbhdv->nbhtv', q_e, s_start)
    o = o.transpose(1, 0, 3, 2, 4).reshape(bt, n * C, H, dv)[:, :L]
    return o.astype(v.dtype), s_fin.astype(s0.dtype)


def rwkv7_scan(r, w, k, v, a, b, s0):
    xs = tuple(t.astype(jnp.float32).transpose(1, 0, 2, 3) for t in (r, w, k, v, a, b))

    def step(s, inp):
        r_t, w_t, k_t, v_t, a_t, b_t = inp
        sa = jnp.einsum('bhvk,bhk->bhv', s, a_t)
        s = s * w_t[:, :, None, :] + sa[..., None] * b_t[:, :, None, :] + v_t[..., None] * k_t[:, :, None, :]
        return s, jnp.einsum('bhvk,bhk->bhv', s, r_t)

    s_fin, y = lax.scan(step, s0.astype(jnp.float32), xs)
    return y.transpose(1, 0, 2, 3).astype(r.dtype), s_fin.astype(s0.dtype)


def even_mixer(h, shift_prev, s_gla, s_rw, j, P):
    f32 = jnp.float32
    bt, L, _ = h.shape
    proj = h @ P['w_in_even'][j]
    pa, pb_raw = proj[..., :P_A], proj[..., P_A:]
    q_a, k_a, v_a, r_a, g_low = jnp.split(pa, [GLA_QK, 2 * GLA_QK, 2 * GLA_QK + GLA_V, 2 * GLA_QK + 2 * GLA_V], axis=-1)
    ga = lambda t, d: t.reshape(bt, L, GLA_HEADS, d)
    log_alpha = jax.nn.log_sigmoid((g_low @ P['gla_w_gate'][j] + P['gla_b_gate'][j]).astype(f32)) / GLA_TAU
    o_a, s_gla_new = gla_chunked(ga(q_a, GLA_DK) * GLA_DK ** -0.5, ga(k_a, GLA_DK), ga(v_a, GLA_DV),
                                 ga(log_alpha, GLA_DK), s_gla)
    o_a = (rms_norm(o_a) * P['gla_norm_g'][j]).reshape(bt, L, GLA_V) * jax.nn.silu(r_a)
    prev = jnp.concatenate([shift_prev[:, None, :].astype(pb_raw.dtype), pb_raw[:, :-1]], axis=1)
    pb = pb_raw + (prev - pb_raw) * P['rw_mu'][j]
    o1 = 3 * RW_W + RW_W_LORA
    r_b, k_b, v_b, w_low, a_low, gt_low = jnp.split(pb, [RW_W, 2 * RW_W, 3 * RW_W, o1, o1 + RW_A_LORA], axis=-1)
    w_log = -jax.nn.softplus(-(P['rw_w0'][j] + jnp.tanh(w_low) @ P['rw_w_decay'][j])) - 0.5
    decay = jnp.exp(-jnp.exp(w_log.astype(f32)))
    iclr = jax.nn.sigmoid(P['rw_a0'][j] + a_low @ P['rw_w_iclr'][j])
    gate = jax.nn.sigmoid(gt_low) @ P['rw_w_gate'][j]
    hb = lambda t: t.reshape(bt, L, RW_HEADS, RW_DH)
    kk = hb(k_b * P['rw_k_k'][j]).astype(f32)
    kk = kk / jnp.maximum(jnp.sqrt(jnp.sum(kk * kk, axis=-1, keepdims=True)), 1e-12)
    k_b = k_b * (1 + (iclr - 1) * P['rw_k_a'][j])
    r_h, k_h, v_h, a_h = hb(r_b), hb(k_b), hb(v_b), hb(iclr)
    y_b, s_rw_new = rwkv7_scan(r_h, hb(decay), k_h, v_h, -kk, kk * a_h, s_rw)
    y_b = layer_norm(y_b, P['rw_ln_g'][j].reshape(RW_HEADS, RW_DH), P['rw_ln_b'][j].reshape(RW_HEADS, RW_DH), RW_LN_EPS)
    y_b = y_b + jnp.sum(r_h * k_h * P['rw_r_k'][j], axis=-1, keepdims=True) * v_h
    o_b = y_b.reshape(bt, L, RW_W) * gate
    out = jnp.concatenate([o_a, o_b], axis=-1) @ P['w_out_even'][j]
    return out, pb_raw[:, -1], s_gla_new, s_rw_new


def odd_qkv(h, pos, j, P):
    bt, L, _ = h.shape
    proj = h @ P['w_in_odd'][j]
    o2 = ATT_Q + ATT_KV
    o3 = o2 + ATT_KV
    o4 = o3 + IDX_HEADS * IDX_DH
    q, k, v, qi, ki, wi = jnp.split(proj, [ATT_Q, o2, o3, o4, o4 + IDX_DH], axis=-1)
    q = rotary(rms_norm(q.reshape(bt, L, ATT_HEADS, ATT_DH)) * P['q_norm_g'][j], pos)
    k = rotary(rms_norm(k.reshape(bt, L, ATT_KV_HEADS, ATT_DH)) * P['k_norm_g'][j], pos)
    v = v.reshape(bt, L, ATT_KV_HEADS, ATT_DH)
    qi = rotary(qi.reshape(bt, L, IDX_HEADS, IDX_DH), pos)
    ki = rotary(layer_norm(ki, P['kidx_ln_g'][j], P['kidx_ln_b'][j], EPS)[:, :, None, :], pos)[:, :, 0, :]
    wi = wi * (IDX_HEADS * IDX_DH) ** -0.5
    return q, k, v, qi, ki, wi


def index_scores(qi, wi, ki, q_pos, k_pos):
    dots = jax.nn.relu(jnp.einsum('bqhd,bsd->bqhs', qi.astype(jnp.float32), ki.astype(jnp.float32)))
    sc = jnp.einsum('bqh,bqhs->bqs', wi.astype(jnp.float32), dots)
    return jnp.where(k_pos[None, None, :] <= q_pos[None, :, None], sc, -jnp.inf)


def sparse_attend(q, kg, vg, valid):
    bt, nq = q.shape[:2]
    G = ATT_HEADS // ATT_KV_HEADS
    qg = q.reshape(bt, nq, ATT_KV_HEADS, G, ATT_DH).astype(jnp.float32)
    s = jnp.einsum('bqngd,bqknd->bqngk', qg, kg.astype(jnp.float32)) * ATT_DH ** -0.5
    s = jnp.where(valid[:, :, None, None, :], s, -jnp.inf)
    p = jax.nn.softmax(s, axis=-1)
    o = jnp.einsum('bqngk,bqknd->bqngd', p, vg.astype(jnp.float32))
    return o.reshape(bt, nq, ATT_Q).astype(q.dtype)


def prompt_sparse_attention(q, k, v, qi, ki, wi):
    bt, L = q.shape[:2]
    topk = min(TOPK_MAX, L // 4)
    qb = min(Q_BLOCK, L)
    nb = L // qb
    k_pos = jnp.arange(L)
    bidx = jnp.arange(bt)[:, None, None]

    def block(i):
        s0 = i * qb
        sl = lambda t: lax.dynamic_slice_in_dim(t, s0, qb, axis=1)
        q_pos = s0 + jnp.arange(qb)
        _, sel = lax.top_k(index_scores(sl(qi), sl(wi), ki, q_pos, k_pos), topk)
        valid = sel <= q_pos[None, :, None]
        return sparse_attend(sl(q), k[bidx, sel], v[bidx, sel], valid)

    o = lax.map(block, jnp.arange(nb))
    return o.transpose(1, 0, 2, 3).reshape(bt, L, ATT_Q)


def sample_sparse_attention(q, k_new, v_new, qi, ki_new, wi, cache_k, cache_v, cache_kidx, page_table, j):
    bt, nq = q.shape[:2]
    past = page_table.shape[1] * PAGE_SIZE
    L = past + nq
    topk = min(TOPK_MAX, L // 4)
    ki_past = cache_kidx[j, page_table].reshape(bt, past, IDX_DH)
    ki_all = jnp.concatenate([ki_past.astype(ki_new.dtype), ki_new], axis=1)
    q_pos = past + jnp.arange(nq)
    _, sel = lax.top_k(index_scores(qi, wi, ki_all, q_pos, jnp.arange(L)), topk)
    valid = sel <= q_pos[None, :, None]
    in_past = (sel < past)[..., None, None]
    bidx = jnp.arange(bt)[:, None, None]
    ps = jnp.minimum(sel, past - 1)
    phys = page_table[bidx, ps // PAGE_SIZE]
    off = ps % PAGE_SIZE
    ns = jnp.clip(sel - past, 0, nq - 1)
    kg = jnp.where(in_past, cache_k[j, phys, off], k_new[bidx, ns])
    vg = jnp.where(in_past, cache_v[j, phys, off], v_new[bidx, ns])
    return sparse_attend(q, kg, vg, valid)


def swiglu(h, w_gate, w_up, w_down):
    return (jax.nn.silu(h @ w_gate) * (h @ w_up)) @ w_down


def moe_swiglu(h, w_router, w_gate, w_up, w_down):
    bt, L, D = h.shape
    xt = h.reshape(bt * L, D)
    logits = (xt @ w_router).astype(jnp.float32)
    top_v, top_i = lax.top_k(logits, TOP_K_EXPERTS)
    probs = jax.nn.softmax(top_v, axis=-1)
    combine = jnp.sum(jax.nn.one_hot(top_i, N_EXPERTS, dtype=jnp.float32) * probs[..., None], axis=1)
    y = jnp.zeros((bt * L, D), jnp.float32)
    for e in range(N_EXPERTS):
        he = jax.nn.silu(xt @ w_gate[e]) * (xt @ w_up[e])
        y = y + combine[:, e:e + 1] * (he @ w_down[e]).astype(jnp.float32)
    return y.astype(h.dtype).reshape(bt, L, D)


def trunk(x, c, pos, shift0, gla0, rw0, attend, P):
    mod = jnp.einsum('bd,lde->lbe', jax.nn.silu(c), P['w_ada']) + P['b_ada'][:, None, :]
    glas, rws, shifts, ks, vs, kis = [], [], [], [], [], []
    for layer in range(DEPTH):
        sh1, sc1, g1, sh2, sc2, g2 = jnp.split(mod[layer], 6, axis=-1)
        j = layer // 2
        h = modulate(x, sh1, sc1)
        if layer % 2 == 0:
            mix, shf, sg, sr = even_mixer(h, shift0[j], gla0[j], rw0[j], j, P)
            glas.append(sg)
            rws.append(sr)
            shifts.append(shf)
        else:
            q, k, v, qi, ki, wi = odd_qkv(h, pos, j, P)
            mix = attend(j, q, k, v, qi, ki, wi) @ P['w_out_odd'][j]
            ks.append(k)
            vs.append(v)
            kis.append(ki)
        x = x + g1[:, None, :] * mix
        h = modulate(x, sh2, sc2)
        if layer % 2 == 0:
            ff = swiglu(h, P['ffn_w_gate'][j], P['ffn_w_up'][j], P['ffn_w_down'][j])
        else:
            ff = moe_swiglu(h, P['moe_router'][j], P['moe_w_gate'][j], P['moe_w_up'][j], P['moe_w_down'][j])
        x = x + g2[:, None, :] * ff
    return x, jnp.stack(glas), jnp.stack(rws), jnp.stack(shifts), jnp.stack(ks), jnp.stack(vs), jnp.stack(kis)


def setup_inputs(seed: int = 0) -> dict:
    keys = jax.random.split(jax.random.key(seed), 64)
    counter = [0]

    def next_key():
        counter[0] += 1
        return keys[counter[0] - 1]

    def nrm(shape, scale=1.0):
        return scale * jax.random.normal(next_key(), shape, jnp.float32)

    def unif(shape, lo, hi):
        return jax.random.uniform(next_key(), shape, jnp.float32, lo, hi)

    D = D_MODEL
    n_pages = PAST_LEN // PAGE_SIZE
    n_pool = (DEC_BATCH * n_pages * 5) // 4
    page_table = jax.random.permutation(next_key(), n_pool)[: DEC_BATCH * n_pages]
    page_table = page_table.reshape(DEC_BATCH, n_pages).astype(jnp.int32)
    return {
        'x_prompt': nrm((BATCH, SEQ, D)),
        'x_sample': nrm((DEC_BATCH, DEC_SEQ, D)),
        'state_gla': nrm((N_EVEN, DEC_BATCH, GLA_HEADS, GLA_DK, GLA_DV)),
        'state_rwkv': nrm((N_EVEN, DEC_BATCH, RW_HEADS, RW_DH, RW_DH), 0.5),
        'state_shift': nrm((N_EVEN, DEC_BATCH, P_B)),
        'cache_k': nrm((N_ODD, n_pool, PAGE_SIZE, ATT_KV_HEADS, ATT_DH)),
        'cache_v': nrm((N_ODD, n_pool, PAGE_SIZE, ATT_KV_HEADS, ATT_DH)),
        'cache_kidx': nrm((N_ODD, n_pool, PAGE_SIZE, IDX_DH)),
        'page_table': page_table,
        'c_prompt': nrm((BATCH, D)),
        'c_sample': nrm((DEC_BATCH, D)),
        'w_ada': nrm((DEPTH, D, 6 * D), 0.3 * D ** -0.5),
        'b_ada': nrm((DEPTH, 6 * D), 0.02),
        'w_in_even': nrm((N_EVEN, D, P_EVEN), D ** -0.5),
        'gla_w_gate': nrm((N_EVEN, GLA_GATE_RANK, GLA_QK), GLA_GATE_RANK ** -0.5),
        'gla_b_gate': nrm((N_EVEN, GLA_QK), 0.1),
        'gla_norm_g': 1.0 + nrm((N_EVEN, GLA_DV), 0.02),
        'rw_mu': unif((N_EVEN, P_B), 0.0, 1.0),
        'rw_w0': unif((N_EVEN, RW_W), -6.0, -0.5),
        'rw_w_decay': nrm((N_EVEN, RW_W_LORA, RW_W), 0.1),
        'rw_a0': nrm((N_EVEN, RW_W), 0.1),
        'rw_w_iclr': nrm((N_EVEN, RW_A_LORA, RW_W), 0.5 * RW_A_LORA ** -0.5),
        'rw_w_gate': nrm((N_EVEN, RW_G_LORA, RW_W), RW_G_LORA ** -0.5),
        'rw_k_k': 0.85 + nrm((N_EVEN, RW_W), 0.02),
        'rw_k_a': 1.0 + nrm((N_EVEN, RW_W), 0.02),
        'rw_r_k': nrm((N_EVEN, RW_HEADS, RW_DH), 0.1),
        'rw_ln_g': 1.0 + nrm((N_EVEN, RW_W), 0.02),
        'rw_ln_b': nrm((N_EVEN, RW_W), 0.02),
        'w_out_even': nrm((N_EVEN, MIX_EVEN, D), MIX_EVEN ** -0.5),
        'w_in_odd': nrm((N_ODD, D, P_ODD), D ** -0.5),
        'q_norm_g': 1.0 + nrm((N_ODD, ATT_DH), 0.02),
        'k_norm_g': 1.0 + nrm((N_ODD, ATT_DH), 0.02),
        'kidx_ln_g': 1.0 + nrm((N_ODD, IDX_DH), 0.02),
        'kidx_ln_b': nrm((N_ODD, IDX_DH), 0.02),
        'w_out_odd': nrm((N_ODD, ATT_Q, D), ATT_Q ** -0.5),
        'ffn_w_gate': nrm((N_EVEN, D, D_FF), D ** -0.5),
        'ffn_w_up': nrm((N_EVEN, D, D_FF), D ** -0.5),
        'ffn_w_down': nrm((N_EVEN, D_FF, D), D_FF ** -0.5),
        'moe_router': nrm((N_ODD, D, N_EXPERTS), D ** -0.5),
        'moe_w_gate': nrm((N_ODD, N_EXPERTS, D, D_FF_EXPERT), D ** -0.5),
        'moe_w_up': nrm((N_ODD, N_EXPERTS, D, D_FF_EXPERT), D ** -0.5),
        'moe_w_down': nrm((N_ODD, N_EXPERTS, D_FF_EXPERT, D), D_FF_EXPERT ** -0.5),
    }


def reference(x_prompt, x_sample, state_gla, state_rwkv, state_shift, cache_k, cache_v, cache_kidx, page_table,
              c_prompt, c_sample, w_ada, b_ada, w_in_even, gla_w_gate, gla_b_gate, gla_norm_g,
              rw_mu, rw_w0, rw_w_decay, rw_a0, rw_w_iclr, rw_w_gate, rw_k_k, rw_k_a, rw_r_k, rw_ln_g, rw_ln_b,
              w_out_even, w_in_odd, q_norm_g, k_norm_g, kidx_ln_g, kidx_ln_b, w_out_odd,
              ffn_w_gate, ffn_w_up, ffn_w_down, moe_router, moe_w_gate, moe_w_up, moe_w_down):
    P = dict(w_ada=w_ada, b_ada=b_ada, w_in_even=w_in_even, gla_w_gate=gla_w_gate, gla_b_gate=gla_b_gate,
             gla_norm_g=gla_norm_g, rw_mu=rw_mu, rw_w0=rw_w0, rw_w_decay=rw_w_decay, rw_a0=rw_a0,
             rw_w_iclr=rw_w_iclr, rw_w_gate=rw_w_gate, rw_k_k=rw_k_k, rw_k_a=rw_k_a, rw_r_k=rw_r_k,
             rw_ln_g=rw_ln_g, rw_ln_b=rw_ln_b, w_out_even=w_out_even, w_in_odd=w_in_odd,
             q_norm_g=q_norm_g, k_norm_g=k_norm_g, kidx_ln_g=kidx_ln_g, kidx_ln_b=kidx_ln_b,
             w_out_odd=w_out_odd, ffn_w_gate=ffn_w_gate, ffn_w_up=ffn_w_up, ffn_w_down=ffn_w_down,
             moe_router=moe_router, moe_w_gate=moe_w_gate, moe_w_up=moe_w_up, moe_w_down=moe_w_down)
    b_p, seq = x_prompt.shape[0], x_prompt.shape[1]
    zero_gla = jnp.zeros((N_EVEN, b_p) + state_gla.shape[2:], state_gla.dtype)
    zero_rw = jnp.zeros((N_EVEN, b_p) + state_rwkv.shape[2:], state_rwkv.dtype)
    zero_shift = jnp.zeros((N_EVEN, b_p) + state_shift.shape[2:], state_shift.dtype)
    pos_p = jnp.arange(seq, dtype=jnp.int32)
    past = page_table.shape[1] * PAGE_SIZE
    pos_s = past + jnp.arange(x_sample.shape[1], dtype=jnp.int32)

    def prompt_attend(j, q, k, v, qi, ki, wi):
        return prompt_sparse_attention(q, k, v, qi, ki, wi)

    def sample_attend(j, q, k, v, qi, ki, wi):
        return sample_sparse_attention(q, k, v, qi, ki, wi, cache_k, cache_v, cache_kidx, page_table, j)

    y_prompt, p_gla, p_rw, p_shift, p_k, p_v, p_kidx = trunk(
        x_prompt, c_prompt, pos_p, zero_shift, zero_gla, zero_rw, prompt_attend, P)
    y_sample, s_gla, s_rw, s_shift, s_k, s_v, s_kidx = trunk(
        x_sample, c_sample, pos_s, state_shift, state_gla, state_rwkv, sample_attend, P)
    return (y_prompt, y_sample, p_gla, p_rw, p_shift, p_k, p_v, p_kidx, s_gla, s_rw, s_shift, s_k, s_v, s_kidx)
```

```python
import functools
import math

import jax
import jax.numpy as jnp
from jax import lax
from jax.experimental import pallas as pl
from jax.experimental.pallas import tpu as pltpu

F32 = jnp.float32
BF16 = jnp.bfloat16

PAGE_SIZE = 128
GLA_HEADS, GLA_DK, GLA_DV, GLA_GATE_RANK, GLA_TAU, GLA_CHUNK = 4, 64, 128, 16, 16.0, 64
RW_HEADS, RW_DH, RW_W_LORA, RW_A_LORA, RW_G_LORA, RW_LN_EPS = 8, 64, 32, 32, 96, 64e-5
ATT_HEADS, ATT_KV_HEADS, ATT_DH = 16, 4, 64
IDX_HEADS, IDX_DH = 8, 64
TOPK_MAX, Q_BLOCK, ROPE_THETA = 256, 128, 10000.0
N_EXPERTS, TOP_K_EXPERTS = 8, 2
EPS = 1e-6
GLA_QK = GLA_HEADS * GLA_DK
GLA_V = GLA_HEADS * GLA_DV
RW_W = RW_HEADS * RW_DH
P_A = 2 * GLA_QK + 2 * GLA_V + GLA_GATE_RANK
ATT_Q = ATT_HEADS * ATT_DH
ATT_KV = ATT_KV_HEADS * ATT_DH

VMEM_LIMIT_BYTES = 56 * 1024 * 1024
ROW_TILE = 512


def _mm_kernel(x_ref, w_ref, o_ref):
    o_ref[...] = jnp.dot(x_ref[...].astype(BF16), w_ref[...], preferred_element_type=F32)


def mm(x, w):
    m, k = x.shape
    n = w.shape[1]
    tm = min(ROW_TILE, m)
    assert m % tm == 0
    return pl.pallas_call(
        _mm_kernel,
        grid=(m // tm,),
        in_specs=[pl.BlockSpec((tm, k), lambda i: (i, 0)), pl.BlockSpec((k, n), lambda i: (0, 0))],
        out_specs=pl.BlockSpec((tm, n), lambda i: (i, 0)),
        out_shape=jax.ShapeDtypeStruct((m, n), F32),
        compiler_params=pltpu.CompilerParams(dimension_semantics=("arbitrary",),
                                             vmem_limit_bytes=VMEM_LIMIT_BYTES),
    )(x, w.astype(BF16))


def mm3(h, w):
    b, l, d = h.shape
    return mm(h.reshape(b * l, d), w).reshape(b, l, w.shape[1])


def rms_norm(x, eps=EPS):
    return x * lax.rsqrt(jnp.mean(x * x, axis=-1, keepdims=True) + eps)


def layer_norm(x, g, b, eps):
    mu = jnp.mean(x, axis=-1, keepdims=True)
    var = jnp.mean(jnp.square(x - mu), axis=-1, keepdims=True)
    return (x - mu) * lax.rsqrt(var + eps) * g + b


def modulate(x, shift, scale):
    return rms_norm(x) * (1 + scale[:, None, :]) + shift[:, None, :]


def rotary(x, pos):
    half = x.shape[-1] // 2
    inv_freq = jnp.power(ROPE_THETA, -jnp.arange(half, dtype=F32) / half)
    ang = pos.astype(F32)[:, None] * inv_freq[None, :]
    cos = jnp.cos(ang)[None, :, None, :]
    sin = jnp.sin(ang)[None, :, None, :]
    x1, x2 = x[..., :half], x[..., half:]
    return jnp.concatenate([x1 * cos - x2 * sin, x2 * cos + x1 * sin], axis=-1)


def gla_chunked(q, k, v, log_a, s0):
    bt, L, H, _ = q.shape
    dv = v.shape[-1]
    C = min(GLA_CHUNK, L)
    n = -(-L // C)
    pad = n * C - L

    def prep(t):
        t = jnp.pad(t, ((0, 0), (0, pad), (0, 0), (0, 0)))
        return t.reshape(bt, n, C, H, t.shape[-1]).transpose(1, 0, 3, 2, 4)

    qc, kc, vc, gc = prep(q), prep(k), prep(v), prep(log_a)
    bcum = jnp.cumsum(gc, axis=3)
    blast = bcum[:, :, :, -1:, :]
    q_e = qc * jnp.exp(bcum)
    k_e = kc * jnp.exp(-bcum)
    k_end = kc * jnp.exp(blast - bcum)
    causal = jnp.tril(jnp.ones((C, C), dtype=bool))
    att = jnp.where(causal, jnp.einsum('nbhtd,nbhsd->nbhts', q_e, k_e), 0.0)
    o_intra = jnp.einsum('nbhts,nbhsv->nbhtv', att, vc)
    decay = jnp.exp(blast[:, :, :, 0, :])
    kv = jnp.einsum('nbhsd,nbhsv->nbhdv', k_end, vc)

    def step(s, inp):
        dec, kv_i = inp
        return dec[..., None] * s + kv_i, s

    s_fin, s_start = lax.scan(step, s0, (decay, kv))
    o = o_intra + jnp.einsum('nbhtd,nbhdv->nbhtv', q_e, s_start)
    o = o.transpose(1, 0, 3, 2, 4).reshape(bt, n * C, H, dv)[:, :L]
    return o, s_fin


def rwkv7_scan(r, w, k, v, a, b, s0):
    xs = tuple(t.transpose(1, 0, 2, 3) for t in (r, w, k, v, a, b))

    def step(s, inp):
        r_t, w_t, k_t, v_t, a_t, b_t = inp
        sa = jnp.einsum('bhvk,bhk->bhv', s, a_t)
        s = s * w_t[:, :, None, :] + sa[..., None] * b_t[:, :, None, :] + v_t[..., None] * k_t[:, :, None, :]
        return s, jnp.einsum('bhvk,bhk->bhv', s, r_t)

    s_fin, y = lax.scan(step, s0, xs)
    return y.transpose(1, 0, 2, 3), s_fin


def even_mixer(h, shift_prev, s_gla, s_rw, j, P):
    bt, L, _ = h.shape
    proj = mm3(h, P['w_in_even'][j])
    pa, pb_raw = proj[..., :P_A], proj[..., P_A:]
    q_a, k_a, v_a, r_a, g_low = jnp.split(pa, [GLA_QK, 2 * GLA_QK, 2 * GLA_QK + GLA_V, 2 * GLA_QK + 2 * GLA_V], axis=-1)
    ga = lambda t, d: t.reshape(bt, L, GLA_HEADS, d)
    log_alpha = jax.nn.log_sigmoid(g_low @ P['gla_w_gate'][j] + P['gla_b_gate'][j]) / GLA_TAU
    o_a, s_gla_new = gla_chunked(ga(q_a, GLA_DK) * GLA_DK ** -0.5, ga(k_a, GLA_DK), ga(v_a, GLA_DV),
                                 ga(log_alpha, GLA_DK), s_gla)
    o_a = (rms_norm(o_a) * P['gla_norm_g'][j]).reshape(bt, L, GLA_V) * jax.nn.silu(r_a)
    prev = jnp.concatenate([shift_prev[:, None, :], pb_raw[:, :-1]], axis=1)
    pb = pb_raw + (prev - pb_raw) * P['rw_mu'][j]
    o1 = 3 * RW_W + RW_W_LORA
    r_b, k_b, v_b, w_low, a_low, gt_low = jnp.split(pb, [RW_W, 2 * RW_W, 3 * RW_W, o1, o1 + RW_A_LORA], axis=-1)
    w_log = -jax.nn.softplus(-(P['rw_w0'][j] + jnp.tanh(w_low) @ P['rw_w_decay'][j])) - 0.5
    decay = jnp.exp(-jnp.exp(w_log))
    iclr = jax.nn.sigmoid(P['rw_a0'][j] + a_low @ P['rw_w_iclr'][j])
    gate = jax.nn.sigmoid(gt_low) @ P['rw_w_gate'][j]
    hb = lambda t: t.reshape(bt, L, RW_HEADS, RW_DH)
    kk = hb(k_b * P['rw_k_k'][j])
    kk = kk / jnp.maximum(jnp.sqrt(jnp.sum(kk * kk, axis=-1, keepdims=True)), 1e-12)
    k_b = k_b * (1 + (iclr - 1) * P['rw_k_a'][j])
    r_h, k_h, v_h, a_h = hb(r_b), hb(k_b), hb(v_b), hb(iclr)
    y_b, s_rw_new = rwkv7_scan(r_h, hb(decay), k_h, v_h, -kk, kk * a_h, s_rw)
    y_b = layer_norm(y_b, P['rw_ln_g'][j].reshape(RW_HEADS, RW_DH), P['rw_ln_b'][j].reshape(RW_HEADS, RW_DH), RW_LN_EPS)
    y_b = y_b + jnp.sum(r_h * k_h * P['rw_r_k'][j], axis=-1, keepdims=True) * v_h
    o_b = y_b.reshape(bt, L, RW_W) * gate
    out = mm3(jnp.concatenate([o_a, o_b], axis=-1), P['w_out_even'][j])
    return out, pb_raw[:, -1], s_gla_new, s_rw_new


def odd_qkv(h, pos, j, P):
    bt, L, _ = h.shape
    proj = mm3(h, P['w_in_odd'][j])
    o2 = ATT_Q + ATT_KV
    o3 = o2 + ATT_KV
    o4 = o3 + IDX_HEADS * IDX_DH
    q, k, v, qi, ki, wi = jnp.split(proj, [ATT_Q, o2, o3, o4, o4 + IDX_DH], axis=-1)
    q = rotary(rms_norm(q.reshape(bt, L, ATT_HEADS, ATT_DH)) * P['q_norm_g'][j], pos)
    k = rotary(rms_norm(k.reshape(bt, L, ATT_KV_HEADS, ATT_DH)) * P['k_norm_g'][j], pos)
    v = v.reshape(bt, L, ATT_KV_HEADS, ATT_DH)
    qi = rotary(qi.reshape(bt, L, IDX_HEADS, IDX_DH), pos)
    ki = rotary(layer_norm(ki, P['kidx_ln_g'][j], P['kidx_ln_b'][j], EPS)[:, :, None, :], pos)[:, :, 0, :]
    wi = wi * (IDX_HEADS * IDX_DH) ** -0.5
    return q, k, v, qi, ki, wi


def index_scores(qi, wi, ki, q_pos, k_pos):
    dots = jax.nn.relu(jnp.einsum('bqhd,bsd->bqhs', qi, ki))
    sc = jnp.einsum('bqh,bqhs->bqs', wi, dots)
    return jnp.where(k_pos[None, None, :] <= q_pos[None, :, None], sc, -jnp.inf)


def sparse_attend(q, kg, vg, valid):
    bt, nq = q.shape[:2]
    G = ATT_HEADS // ATT_KV_HEADS
    qg = q.reshape(bt, nq, ATT_KV_HEADS, G, ATT_DH)
    s = jnp.einsum('bqngd,bqknd->bqngk', qg, kg) * ATT_DH ** -0.5
    s = jnp.where(valid[:, :, None, None, :], s, -jnp.inf)
    p = jax.nn.softmax(s, axis=-1)
    o = jnp.einsum('bqngk,bqknd->bqngd', p, vg)
    return o.reshape(bt, nq, ATT_Q)


def prompt_sparse_attention(q, k, v, qi, ki, wi):
    bt, L = q.shape[:2]
    topk = min(TOPK_MAX, L // 4)
    qb = min(Q_BLOCK, L)
    nb = L // qb
    k_pos = jnp.arange(L)
    bidx = jnp.arange(bt)[:, None, None]

    def block(i):
        s0 = i * qb
        sl = lambda t: lax.dynamic_slice_in_dim(t, s0, qb, axis=1)
        q_pos = s0 + jnp.arange(qb)
        _, sel = lax.top_k(index_scores(sl(qi), sl(wi), ki, q_pos, k_pos), topk)
        valid = sel <= q_pos[None, :, None]
        return sparse_attend(sl(q), k[bidx, sel], v[bidx, sel], valid)

    o = lax.map(block, jnp.arange(nb))
    return o.transpose(1, 0, 2, 3).reshape(bt, L, ATT_Q)


def sample_sparse_attention(q, k_new, v_new, qi, ki_new, wi, cache_k, cache_v, cache_kidx, page_table, j):
    bt, nq = q.shape[:2]
    past = page_table.shape[1] * PAGE_SIZE
    L = past + nq
    topk = min(TOPK_MAX, L // 4)
    ki_past = cache_kidx[j, page_table].reshape(bt, past, IDX_DH)
    ki_all = jnp.concatenate([ki_past, ki_new], axis=1)
    q_pos = past + jnp.arange(nq)
    _, sel = lax.top_k(index_scores(qi, wi, ki_all, q_pos, jnp.arange(L)), topk)
    valid = sel <= q_pos[None, :, None]
    in_past = (sel < past)[..., None, None]
    bidx = jnp.arange(bt)[:, None, None]
    ps = jnp.minimum(sel, past - 1)
    phys = page_table[bidx, ps // PAGE_SIZE]
    off = ps % PAGE_SIZE
    ns = jnp.clip(sel - past, 0, nq - 1)
    kg = jnp.where(in_past, cache_k[j, phys, off], k_new[bidx, ns])
    vg = jnp.where(in_past, cache_v[j, phys, off], v_new[bidx, ns])
    return sparse_attend(q, kg, vg, valid)


def swiglu(h, w_gate, w_up, w_down):
    return mm3(jax.nn.silu(mm3(h, w_gate)) * mm3(h, w_up), w_down)


def moe_swiglu(h, w_router, w_gate, w_up, w_down):
    bt, L, D = h.shape
    xt = h.reshape(bt * L, D)
    logits = xt @ w_router
    top_v, top_i = lax.top_k(logits, TOP_K_EXPERTS)
    probs = jax.nn.softmax(top_v, axis=-1)
    combine = jnp.sum(jax.nn.one_hot(top_i, N_EXPERTS, dtype=F32) * probs[..., None], axis=1)
    y = jnp.zeros((bt * L, D), F32)
    for e in range(N_EXPERTS):
        he = jax.nn.silu(mm(xt, w_gate[e])) * mm(xt, w_up[e])
        y = y + combine[:, e:e + 1] * mm(he, w_down[e])
    return y.reshape(bt, L, D)


def trunk(x, c, pos, shift0, gla0, rw0, attend, P):
    depth = P['w_ada'].shape[0]
    mod = jnp.einsum('bd,lde->lbe', jax.nn.silu(c), P['w_ada']) + P['b_ada'][:, None, :]
    glas, rws, shifts, ks, vs, kis = [], [], [], [], [], []
    for layer in range(depth):
        sh1, sc1, g1, sh2, sc2, g2 = jnp.split(mod[layer], 6, axis=-1)
        j = layer // 2
        h = modulate(x, sh1, sc1)
        if layer % 2 == 0:
            mix, shf, sg, sr = even_mixer(h, shift0[j], gla0[j], rw0[j], j, P)
            glas.append(sg)
            rws.append(sr)
            shifts.append(shf)
        else:
            q, k, v, qi, ki, wi = odd_qkv(h, pos, j, P)
            mix = mm3(attend(j, q, k, v, qi, ki, wi), P['w_out_odd'][j])
            ks.append(k)
            vs.append(v)
            kis.append(ki)
        x = x + g1[:, None, :] * mix
        h = modulate(x, sh2, sc2)
        if layer % 2 == 0:
            ff = swiglu(h, P['ffn_w_gate'][j], P['ffn_w_up'][j], P['ffn_w_down'][j])
        else:
            ff = moe_swiglu(h, P['moe_router'][j], P['moe_w_gate'][j], P['moe_w_up'][j], P['moe_w_down'][j])
        x = x + g2[:, None, :] * ff
    return x, jnp.stack(glas), jnp.stack(rws), jnp.stack(shifts), jnp.stack(ks), jnp.stack(vs), jnp.stack(kis)


def kernel(x_prompt, x_sample, state_gla, state_rwkv, state_shift, cache_k, cache_v, cache_kidx, page_table, c_prompt, c_sample, w_ada, b_ada, w_in_even, gla_w_gate, gla_b_gate, gla_norm_g, rw_mu, rw_w0, rw_w_decay, rw_a0, rw_w_iclr, rw_w_gate, rw_k_k, rw_k_a, rw_r_k, rw_ln_g, rw_ln_b, w_out_even, w_in_odd, q_norm_g, k_norm_g, kidx_ln_g, kidx_ln_b, w_out_odd, ffn_w_gate, ffn_w_up, ffn_w_down, moe_router, moe_w_gate, moe_w_up, moe_w_down):
    P = dict(w_ada=w_ada, b_ada=b_ada, w_in_even=w_in_even, gla_w_gate=gla_w_gate, gla_b_gate=gla_b_gate,
             gla_norm_g=gla_norm_g, rw_mu=rw_mu, rw_w0=rw_w0, rw_w_decay=rw_w_decay, rw_a0=rw_a0,
             rw_w_iclr=rw_w_iclr, rw_w_gate=rw_w_gate, rw_k_k=rw_k_k, rw_k_a=rw_k_a, rw_r_k=rw_r_k,
             rw_ln_g=rw_ln_g, rw_ln_b=rw_ln_b, w_out_even=w_out_even, w_in_odd=w_in_odd,
             q_norm_g=q_norm_g, k_norm_g=k_norm_g, kidx_ln_g=kidx_ln_g, kidx_ln_b=kidx_ln_b,
             w_out_odd=w_out_odd, ffn_w_gate=ffn_w_gate, ffn_w_up=ffn_w_up, ffn_w_down=ffn_w_down,
             moe_router=moe_router, moe_w_gate=moe_w_gate, moe_w_up=moe_w_up, moe_w_down=moe_w_down)
    n_even = state_gla.shape[0]
    b_p, seq = x_prompt.shape[0], x_prompt.shape[1]
    zero_gla = jnp.zeros((n_even, b_p) + state_gla.shape[2:], state_gla.dtype)
    zero_rw = jnp.zeros((n_even, b_p) + state_rwkv.shape[2:], state_rwkv.dtype)
    zero_shift = jnp.zeros((n_even, b_p) + state_shift.shape[2:], state_shift.dtype)
    pos_p = jnp.arange(seq, dtype=jnp.int32)
    past = page_table.shape[1] * PAGE_SIZE
    pos_s = past + jnp.arange(x_sample.shape[1], dtype=jnp.int32)

    def prompt_attend(j, q, k, v, qi, ki, wi):
        return prompt_sparse_attention(q, k, v, qi, ki, wi)

    def sample_attend(j, q, k, v, qi, ki, wi):
        return sample_sparse_attention(q, k, v, qi, ki, wi, cache_k, cache_v, cache_kidx, page_table, j)

    y_prompt, p_gla, p_rw, p_shift, p_k, p_v, p_kidx = trunk(
        x_prompt, c_prompt, pos_p, zero_shift, zero_gla, zero_rw, prompt_attend, P)
    y_sample, s_gla, s_rw, s_shift, s_k, s_v, s_kidx = trunk(
        x_sample, c_sample, pos_s, state_shift, state_gla, state_rwkv, sample_attend, P)
    return (y_prompt, y_sample, p_gla, p_rw, p_shift, p_k, p_v, p_kidx, s_gla, s_rw, s_shift, s_k, s_v, s_kidx)
```

```python
import functools
import math

import jax
import jax.numpy as jnp
import numpy as np
from jax import lax
from jax.experimental import pallas as pl
from jax.experimental.pallas import tpu as pltpu

F32 = jnp.float32
BF16 = jnp.bfloat16

PAGE_SIZE = 128
GLA_HEADS, GLA_DK, GLA_DV, GLA_GATE_RANK, GLA_TAU, GLA_CHUNK = 4, 64, 128, 16, 16.0, 64
RW_HEADS, RW_DH, RW_W_LORA, RW_A_LORA, RW_G_LORA, RW_LN_EPS = 8, 64, 32, 32, 96, 64e-5
ATT_HEADS, ATT_KV_HEADS, ATT_DH = 16, 4, 64
IDX_HEADS, IDX_DH = 8, 64
TOPK_MAX, Q_BLOCK, ROPE_THETA = 256, 128, 10000.0
N_EXPERTS, TOP_K_EXPERTS = 8, 2
EPS = 1e-6
GLA_QK = GLA_HEADS * GLA_DK
GLA_V = GLA_HEADS * GLA_DV
RW_W = RW_HEADS * RW_DH
P_A = 2 * GLA_QK + 2 * GLA_V + GLA_GATE_RANK
ATT_Q = ATT_HEADS * ATT_DH
ATT_KV = ATT_KV_HEADS * ATT_DH

VMEM_LIMIT_BYTES = 56 * 1024 * 1024
ROW_TILE = 512


def _mm_kernel(x_ref, w_ref, o_ref):
    o_ref[...] = jnp.dot(x_ref[...].astype(BF16), w_ref[...], preferred_element_type=F32)


def mm(x, w):
    m, k = x.shape
    n = w.shape[1]
    tm = min(ROW_TILE, m)
    assert m % tm == 0
    return pl.pallas_call(
        _mm_kernel,
        grid=(m // tm,),
        in_specs=[pl.BlockSpec((tm, k), lambda i: (i, 0)), pl.BlockSpec((k, n), lambda i: (0, 0))],
        out_specs=pl.BlockSpec((tm, n), lambda i: (i, 0)),
        out_shape=jax.ShapeDtypeStruct((m, n), F32),
        compiler_params=pltpu.CompilerParams(dimension_semantics=("arbitrary",),
                                             vmem_limit_bytes=VMEM_LIMIT_BYTES),
    )(x, w.astype(BF16))


def mm3(h, w):
    b, l, d = h.shape
    return mm(h.reshape(b * l, d), w).reshape(b, l, w.shape[1])


RW_CHUNK = 64


def _bdot(a, b):
    return jnp.dot(a.astype(BF16), b.astype(BF16), preferred_element_type=F32)


def _bdot_nt(a, b):
    return lax.dot_general(a.astype(BF16), b.astype(BF16), (((1,), (1,)), ((), ())), preferred_element_type=F32)


def _bdot_tn(a, b):
    return lax.dot_general(a.astype(BF16), b.astype(BF16), (((0,), (0,)), ((), ())), preferred_element_type=F32)


def _split3(x):
    hi = x.astype(BF16)
    r1 = x - hi.astype(F32)
    mid = r1.astype(BF16)
    lo = (r1 - mid.astype(F32)).astype(BF16)
    return hi, mid, lo


def _rwkv_head_chunk(r, lw, k, v, a, b, h0, tri_bf, strict, incl, eye):
    hi, mid, lo = _split3(lw)
    cum = (jnp.dot(tri_bf, hi, preferred_element_type=F32) + jnp.dot(tri_bf, mid, preferred_element_type=F32)
           + jnp.dot(tri_bf, lo, preferred_element_type=F32))
    cum_last = cum[RW_CHUNK - 1:RW_CHUNK, :]
    e_neg = jnp.exp(-cum)
    e_end = jnp.exp(cum_last - cum)
    a_t = a * jnp.exp(cum - lw)
    r_t = r * jnp.exp(cum)
    k_t = k * e_neg
    b_t = b * e_neg
    k_e = k * e_end
    b_e = b * e_end
    a_ab = jnp.where(strict, _bdot_nt(a_t, b_t), 0.0)
    a_ak = jnp.where(strict, _bdot_nt(a_t, k_t), 0.0)
    a_rb = jnp.where(incl, _bdot_nt(r_t, b_t), 0.0)
    a_rk = jnp.where(incl, _bdot_nt(r_t, k_t), 0.0)
    x = eye + a_ab
    y = a_ab
    n_sq = int(math.log2(RW_CHUNK)) - 1
    for _ in range(n_sq):
        y = _bdot(y, y)
        x = x + _bdot(x, y)
    a_p = _bdot(x, a_t)
    v_p = _bdot(x, _bdot(a_ak, v))
    r_p = r_t + _bdot(a_rb, a_p)
    y_p = _bdot(a_rk, v) + _bdot(a_rb, v_p)
    y_out = _bdot(r_p, h0) + y_p
    m_lr = _bdot_tn(b_e, a_p)
    g = _bdot_tn(k_e, v) + _bdot_tn(b_e, v_p)
    w_col = jnp.sum(jnp.where(eye, jnp.exp(cum_last), 0.0), axis=1, keepdims=True)
    h_new = w_col * h0 + _bdot(m_lr, h0) + g
    return y_out, h_new


def _rwkv_kernel(r_ref, lw_ref, k_ref, v_ref, a_ref, b_ref, h0_ref, y_ref, hout_ref, h_scr):
    c = pl.program_id(0)

    @pl.when(c == 0)
    def _():
        h_scr[...] = h0_ref[...]

    rows = lax.broadcasted_iota(jnp.int32, (RW_CHUNK, RW_CHUNK), 0)
    cols = lax.broadcasted_iota(jnp.int32, (RW_CHUNK, RW_CHUNK), 1)
    strict = rows > cols
    incl = rows >= cols
    eye = rows == cols
    tri_bf = jnp.where(incl, 1.0, 0.0).astype(BF16)
    n_bh = r_ref.shape[0]
    for i in range(n_bh):
        y_out, h_new = _rwkv_head_chunk(r_ref[i], lw_ref[i], k_ref[i], v_ref[i], a_ref[i], b_ref[i], h_scr[i],
                                        tri_bf, strict, incl, eye)
        y_ref[i] = y_out
        h_scr[i] = h_new

    @pl.when(c == pl.num_programs(0) - 1)
    def _():
        hout_ref[...] = h_scr[...]


def rwkv7_chunked(r, lw, k, v, a, b, s0):
    bt, L, H, N = r.shape
    assert L % RW_CHUNK == 0
    hm = lambda t: t.transpose(0, 2, 1, 3).reshape(bt * H, L, N)
    h0 = s0.transpose(0, 1, 3, 2).reshape(bt * H, N, N)
    seq_spec = pl.BlockSpec((bt * H, RW_CHUNK, N), lambda c: (0, c, 0))
    st_spec = pl.BlockSpec((bt * H, N, N), lambda c: (0, 0, 0))
    y, h_fin = pl.pallas_call(
        _rwkv_kernel,
        grid=(L // RW_CHUNK,),
        in_specs=[seq_spec] * 6 + [st_spec],
        out_specs=[seq_spec, st_spec],
        out_shape=[jax.ShapeDtypeStruct((bt * H, L, N), F32), jax.ShapeDtypeStruct((bt * H, N, N), F32)],
        scratch_shapes=[pltpu.VMEM((bt * H, N, N), F32)],
        compiler_params=pltpu.CompilerParams(dimension_semantics=("arbitrary",),
                                             vmem_limit_bytes=VMEM_LIMIT_BYTES),
        name="rwkv7_chunked",
    )(hm(r), hm(lw), hm(k), hm(v), hm(a), hm(b), h0)
    y = y.reshape(bt, H, L, N).transpose(0, 2, 1, 3)
    return y, h_fin.reshape(bt, H, N, N).transpose(0, 1, 3, 2)


DSA_QB = 128
DSA_KC = 512
LANES = 128
MASK_NEG = -1e30
INT_MIN = -2 ** 31
ONE = np.float32(1.0)
ZERO = np.float32(0.0)
G_PER_KV = ATT_HEADS // ATT_KV_HEADS


def _sortable_key(x):
    bits = lax.bitcast_convert_type(x + 0.0, jnp.int32)
    return bits ^ ((bits >> 31) & 0x7FFFFFFF)


def _dsa_kernel(topk, q_ref, qi_ref, wi_ref, k_ref, v_ref, kit_ref, upper_ref, o_ref,
                qbd_scr, qi_scr, wb_scr, key_scr, m_scr, l_scr, acc_scr):
    i = pl.program_id(1)
    n_chunks = (i * DSA_QB) // DSA_KC + 1
    nt = DSA_KC // LANES
    q_pos = i * DSA_QB + lax.broadcasted_iota(jnp.int32, (DSA_QB, LANES), 0)
    lane = lax.broadcasted_iota(jnp.int32, (DSA_QB, LANES), 1)

    qbd_scr[...] = jnp.zeros(qbd_scr.shape, BF16)
    for n in range(ATT_KV_HEADS):
        for g in range(G_PER_KV):
            hd = n * G_PER_KV + g
            piece = q_ref[0, :, hd * ATT_DH:(hd + 1) * ATT_DH] * (ATT_DH ** -0.5)
            qbd_scr[hd * DSA_QB:(hd + 1) * DSA_QB, n * ATT_DH:(n + 1) * ATT_DH] = piece.astype(BF16)
    for h in range(IDX_HEADS):
        qi_scr[h * DSA_QB:(h + 1) * DSA_QB, :] = qi_ref[0, :, h * IDX_DH:(h + 1) * IDX_DH].astype(BF16)
        wb_scr[h] = jnp.broadcast_to(wi_ref[0, :, h:h + 1], (DSA_QB, LANES))

    def score_chunk(c, carry):
        dots = jnp.dot(qi_scr[...], kit_ref[0, c], preferred_element_type=F32)
        for jt in range(nt):
            sc = jnp.zeros((DSA_QB, LANES), F32)
            for h in range(IDX_HEADS):
                d = dots[h * DSA_QB:(h + 1) * DSA_QB, jt * LANES:(jt + 1) * LANES]
                sc = sc + jnp.maximum(d, 0.0) * wb_scr[h]
            k_pos = c * DSA_KC + jt * LANES + lane
            sc = jnp.where(k_pos <= q_pos, sc, -jnp.inf)
            key_scr[c, :, jt * LANES:(jt + 1) * LANES] = _sortable_key(sc)
        return carry

    lax.fori_loop(0, n_chunks, score_chunk, 0)

    def count_ge(cand):
        cand_b = jnp.broadcast_to(cand, (DSA_QB, LANES))

        def body(c, acc):
            for jt in range(nt):
                acc = acc + jnp.where(key_scr[c, :, jt * LANES:(jt + 1) * LANES] >= cand_b, ONE, ZERO)
            return acc

        acc = lax.fori_loop(0, n_chunks, body, jnp.zeros((DSA_QB, LANES), F32))
        return jnp.sum(acc, axis=1, keepdims=True)

    kf = float(topk)
    cur = jnp.where(count_ge(jnp.zeros((DSA_QB, 1), jnp.int32)) >= kf, 0, INT_MIN).astype(jnp.int32)

    def bit_step(p, cur):
        cand = cur | (jnp.int32(1) << (30 - p))
        return jnp.where(count_ge(cand) >= kf, cand, cur)

    thr = lax.fori_loop(0, 31, bit_step, cur)
    thr_b = jnp.broadcast_to(thr, (DSA_QB, LANES))

    def count_gt_body(c, acc):
        for jt in range(nt):
            acc = acc + jnp.where(key_scr[c, :, jt * LANES:(jt + 1) * LANES] > thr_b, ONE, ZERO)
        return acc

    n_gt = jnp.sum(lax.fori_loop(0, n_chunks, count_gt_body, jnp.zeros((DSA_QB, LANES), F32)), axis=1, keepdims=True)
    need = kf - n_gt

    m_scr[...] = jnp.full(m_scr.shape, MASK_NEG, F32)
    l_scr[...] = jnp.zeros(l_scr.shape, F32)
    acc_scr[...] = jnp.zeros(acc_scr.shape, F32)

    def attend_chunk(c, tie_carry):
        k0 = pl.multiple_of(c * DSA_KC, DSA_KC)
        keys = key_scr[c]
        thr_c = jnp.broadcast_to(thr, (DSA_QB, DSA_KC))
        eq = keys == thr_c
        eq_f = jnp.where(eq, ONE, ZERO)
        rank = tie_carry + jnp.dot(eq_f.astype(BF16), upper_ref[...], preferred_element_type=F32)
        k_pos = k0 + lax.broadcasted_iota(jnp.int32, (DSA_QB, DSA_KC), 1)
        qp = i * DSA_QB + lax.broadcasted_iota(jnp.int32, (DSA_QB, DSA_KC), 0)
        sel = jnp.where(keys > thr_c, ONE, jnp.where(rank < need, eq_f, ZERO))
        bias = jnp.where((sel > 0.5) & (k_pos <= qp), ZERO, np.float32(MASK_NEG))
        k_c = k_ref[0, pl.ds(k0, DSA_KC), :]
        v_c = v_ref[0, pl.ds(k0, DSA_KC), :]
        rows = G_PER_KV * DSA_QB
        for n in range(ATT_KV_HEADS):
            s = lax.dot_general(qbd_scr[n * rows:(n + 1) * rows, :], k_c, (((1,), (1,)), ((), ())),
                                preferred_element_type=F32)
            s = (s.reshape(G_PER_KV, DSA_QB, DSA_KC) + bias[None]).reshape(rows, DSA_KC)
            m_old = m_scr[n]
            m_new = jnp.maximum(m_old, jnp.max(s, axis=1, keepdims=True))
            alpha = jnp.exp(m_old - m_new)
            p = jnp.exp(s - m_new)
            l_scr[n] = alpha * l_scr[n] + jnp.sum(p, axis=1, keepdims=True)
            half = (n // 2) * LANES
            pv = jnp.dot(p.astype(BF16), v_c[:, half:half + LANES], preferred_element_type=F32)
            acc_scr[n] = alpha * acc_scr[n] + pv
            m_scr[n] = m_new
        return tie_carry + jnp.sum(eq_f, axis=1, keepdims=True)

    lax.fori_loop(0, n_chunks, attend_chunk, jnp.zeros((DSA_QB, 1), F32))

    for n in range(ATT_KV_HEADS):
        o_n = acc_scr[n] / l_scr[n]
        off = (n % 2) * ATT_DH
        for g in range(G_PER_KV):
            hd = n * G_PER_KV + g
            o_ref[0, :, hd * ATT_DH:(hd + 1) * ATT_DH] = o_n[g * DSA_QB:(g + 1) * DSA_QB, off:off + ATT_DH]


def dsa_prompt_attention(q, k, v, qi, ki, wi):
    bt, L = q.shape[:2]
    assert L % DSA_KC == 0 and L // 4 >= 1
    topk = min(TOPK_MAX, L // 4)
    nb = L // DSA_QB
    nc = L // DSA_KC
    kit = ki.astype(BF16).reshape(bt, nc, DSA_KC, IDX_DH).transpose(0, 1, 3, 2)
    upper = jnp.triu(jnp.ones((DSA_KC, DSA_KC), BF16), 1)
    rows = G_PER_KV * DSA_QB
    return pl.pallas_call(
        functools.partial(_dsa_kernel, topk),
        grid=(bt, nb),
        in_specs=[
            pl.BlockSpec((1, DSA_QB, ATT_Q), lambda b, i: (b, i, 0)),
            pl.BlockSpec((1, DSA_QB, IDX_HEADS * IDX_DH), lambda b, i: (b, i, 0)),
            pl.BlockSpec((1, DSA_QB, IDX_HEADS), lambda b, i: (b, i, 0)),
            pl.BlockSpec((1, L, ATT_KV), lambda b, i: (b, 0, 0)),
            pl.BlockSpec((1, L, ATT_KV), lambda b, i: (b, 0, 0)),
            pl.BlockSpec((1, nc, IDX_DH, DSA_KC), lambda b, i: (b, 0, 0, 0)),
            pl.BlockSpec((DSA_KC, DSA_KC), lambda b, i: (0, 0)),
        ],
        out_specs=pl.BlockSpec((1, DSA_QB, ATT_Q), lambda b, i: (b, i, 0)),
        out_shape=jax.ShapeDtypeStruct((bt, L, ATT_Q), F32),
        scratch_shapes=[
            pltpu.VMEM((ATT_HEADS * DSA_QB, ATT_KV), BF16),
            pltpu.VMEM((IDX_HEADS * DSA_QB, IDX_DH), BF16),
            pltpu.VMEM((IDX_HEADS, DSA_QB, LANES), F32),
            pltpu.VMEM((nc, DSA_QB, DSA_KC), jnp.int32),
            pltpu.VMEM((ATT_KV_HEADS, rows, 1), F32),
            pltpu.VMEM((ATT_KV_HEADS, rows, 1), F32),
            pltpu.VMEM((ATT_KV_HEADS, rows, LANES), F32),
        ],
        compiler_params=pltpu.CompilerParams(dimension_semantics=("arbitrary", "arbitrary"),
                                             vmem_limit_bytes=VMEM_LIMIT_BYTES),
        name="dsa_prompt_attention",
    )(q.reshape(bt, L, ATT_Q), qi.reshape(bt, L, IDX_HEADS * IDX_DH), wi,
      k.reshape(bt, L, ATT_KV).astype(BF16), v.reshape(bt, L, ATT_KV).astype(BF16), kit, upper)


def rms_norm(x, eps=EPS):
    return x * lax.rsqrt(jnp.mean(x * x, axis=-1, keepdims=True) + eps)


def layer_norm(x, g, b, eps):
    mu = jnp.mean(x, axis=-1, keepdims=True)
    var = jnp.mean(jnp.square(x - mu), axis=-1, keepdims=True)
    return (x - mu) * lax.rsqrt(var + eps) * g + b


def modulate(x, shift, scale):
    return rms_norm(x) * (1 + scale[:, None, :]) + shift[:, None, :]


def rotary(x, pos):
    half = x.shape[-1] // 2
    inv_freq = jnp.power(ROPE_THETA, -jnp.arange(half, dtype=F32) / half)
    ang = pos.astype(F32)[:, None] * inv_freq[None, :]
    cos = jnp.cos(ang)[None, :, None, :]
    sin = jnp.sin(ang)[None, :, None, :]
    x1, x2 = x[..., :half], x[..., half:]
    return jnp.concatenate([x1 * cos - x2 * sin, x2 * cos + x1 * sin], axis=-1)


def gla_chunked(q, k, v, log_a, s0):
    bt, L, H, _ = q.shape
    dv = v.shape[-1]
    C = min(GLA_CHUNK, L)
    n = -(-L // C)
    pad = n * C - L

    def prep(t):
        t = jnp.pad(t, ((0, 0), (0, pad), (0, 0), (0, 0)))
        return t.reshape(bt, n, C, H, t.shape[-1]).transpose(1, 0, 3, 2, 4)

    qc, kc, vc, gc = prep(q), prep(k), prep(v), prep(log_a)
    bcum = jnp.cumsum(gc, axis=3)
    blast = bcum[:, :, :, -1:, :]
    q_e = qc * jnp.exp(bcum)
    k_e = kc * jnp.exp(-bcum)
    k_end = kc * jnp.exp(blast - bcum)
    causal = jnp.tril(jnp.ones((C, C), dtype=bool))
    att = jnp.where(causal, jnp.einsum('nbhtd,nbhsd->nbhts', q_e, k_e), 0.0)
    o_intra = jnp.einsum('nbhts,nbhsv->nbhtv', att, vc)
    decay = jnp.exp(blast[:, :, :, 0, :])
    kv = jnp.einsum('nbhsd,nbhsv->nbhdv', k_end, vc)

    def step(s, inp):
        dec, kv_i = inp
        return dec[..., None] * s + kv_i, s

    s_fin, s_start = lax.scan(step, s0, (decay, kv))
    o = o_intra + jnp.einsum('nbhtd,nbhdv->nbhtv', q_e, s_start)
    o = o.transpose(1, 0, 3, 2, 4).reshape(bt, n * C, H, dv)[:, :L]
    return o, s_fin


def rwkv7_scan(r, w, k, v, a, b, s0):
    xs = tuple(t.transpose(1, 0, 2, 3) for t in (r, w, k, v, a, b))

    def step(s, inp):
        r_t, w_t, k_t, v_t, a_t, b_t = inp
        sa = jnp.einsum('bhvk,bhk->bhv', s, a_t)
        s = s * w_t[:, :, None, :] + sa[..., None] * b_t[:, :, None, :] + v_t[..., None] * k_t[:, :, None, :]
        return s, jnp.einsum('bhvk,bhk->bhv', s, r_t)

    s_fin, y = lax.scan(step, s0, xs)
    return y.transpose(1, 0, 2, 3), s_fin


def even_mixer(h, shift_prev, s_gla, s_rw, j, P):
    bt, L, _ = h.shape
    proj = mm3(h, P['w_in_even'][j])
    pa, pb_raw = proj[..., :P_A], proj[..., P_A:]
    q_a, k_a, v_a, r_a, g_low = jnp.split(pa, [GLA_QK, 2 * GLA_QK, 2 * GLA_QK + GLA_V, 2 * GLA_QK + 2 * GLA_V], axis=-1)
    ga = lambda t, d: t.reshape(bt, L, GLA_HEADS, d)
    log_alpha = jax.nn.log_sigmoid(g_low @ P['gla_w_gate'][j] + P['gla_b_gate'][j]) / GLA_TAU
    o_a, s_gla_new = gla_chunked(ga(q_a, GLA_DK) * GLA_DK ** -0.5, ga(k_a, GLA_DK), ga(v_a, GLA_DV),
                                 ga(log_alpha, GLA_DK), s_gla)
    o_a = (rms_norm(o_a) * P['gla_norm_g'][j]).reshape(bt, L, GLA_V) * jax.nn.silu(r_a)
    prev = jnp.concatenate([shift_prev[:, None, :], pb_raw[:, :-1]], axis=1)
    pb = pb_raw + (prev - pb_raw) * P['rw_mu'][j]
    o1 = 3 * RW_W + RW_W_LORA
    r_b, k_b, v_b, w_low, a_low, gt_low = jnp.split(pb, [RW_W, 2 * RW_W, 3 * RW_W, o1, o1 + RW_A_LORA], axis=-1)
    w_log = -jax.nn.softplus(-(P['rw_w0'][j] + jnp.tanh(w_low) @ P['rw_w_decay'][j])) - 0.5
    log_decay = -jnp.exp(w_log)
    iclr = jax.nn.sigmoid(P['rw_a0'][j] + a_low @ P['rw_w_iclr'][j])
    gate = jax.nn.sigmoid(gt_low) @ P['rw_w_gate'][j]
    hb = lambda t: t.reshape(bt, L, RW_HEADS, RW_DH)
    kk = hb(k_b * P['rw_k_k'][j])
    kk = kk / jnp.maximum(jnp.sqrt(jnp.sum(kk * kk, axis=-1, keepdims=True)), 1e-12)
    k_b = k_b * (1 + (iclr - 1) * P['rw_k_a'][j])
    r_h, k_h, v_h, a_h = hb(r_b), hb(k_b), hb(v_b), hb(iclr)
    if L % RW_CHUNK == 0:
        y_b, s_rw_new = rwkv7_chunked(r_h, hb(log_decay), k_h, v_h, -kk, kk * a_h, s_rw)
    else:
        y_b, s_rw_new = rwkv7_scan(r_h, hb(jnp.exp(log_decay)), k_h, v_h, -kk, kk * a_h, s_rw)
    y_b = layer_norm(y_b, P['rw_ln_g'][j].reshape(RW_HEADS, RW_DH), P['rw_ln_b'][j].reshape(RW_HEADS, RW_DH), RW_LN_EPS)
    y_b = y_b + jnp.sum(r_h * k_h * P['rw_r_k'][j], axis=-1, keepdims=True) * v_h
    o_b = y_b.reshape(bt, L, RW_W) * gate
    out = mm3(jnp.concatenate([o_a, o_b], axis=-1), P['w_out_even'][j])
    return out, pb_raw[:, -1], s_gla_new, s_rw_new


def odd_qkv(h, pos, j, P):
    bt, L, _ = h.shape
    proj = mm3(h, P['w_in_odd'][j])
    o2 = ATT_Q + ATT_KV
    o3 = o2 + ATT_KV
    o4 = o3 + IDX_HEADS * IDX_DH
    q, k, v, qi, ki, wi = jnp.split(proj, [ATT_Q, o2, o3, o4, o4 + IDX_DH], axis=-1)
    q = rotary(rms_norm(q.reshape(bt, L, ATT_HEADS, ATT_DH)) * P['q_norm_g'][j], pos)
    k = rotary(rms_norm(k.reshape(bt, L, ATT_KV_HEADS, ATT_DH)) * P['k_norm_g'][j], pos)
    v = v.reshape(bt, L, ATT_KV_HEADS, ATT_DH)
    qi = rotary(qi.reshape(bt, L, IDX_HEADS, IDX_DH), pos)
    ki = rotary(layer_norm(ki, P['kidx_ln_g'][j], P['kidx_ln_b'][j], EPS)[:, :, None, :], pos)[:, :, 0, :]
    wi = wi * (IDX_HEADS * IDX_DH) ** -0.5
    return q, k, v, qi, ki, wi


def index_scores(qi, wi, ki, q_pos, k_pos):
    dots = jax.nn.relu(jnp.einsum('bqhd,bsd->bqhs', qi, ki))
    sc = jnp.einsum('bqh,bqhs->bqs', wi, dots)
    return jnp.where(k_pos[None, None, :] <= q_pos[None, :, None], sc, -jnp.inf)


def sparse_attend(q, kg, vg, valid):
    bt, nq = q.shape[:2]
    G = ATT_HEADS // ATT_KV_HEADS
    qg = q.reshape(bt, nq, ATT_KV_HEADS, G, ATT_DH)
    s = jnp.einsum('bqngd,bqknd->bqngk', qg, kg) * ATT_DH ** -0.5
    s = jnp.where(valid[:, :, None, None, :], s, -jnp.inf)
    p = jax.nn.softmax(s, axis=-1)
    o = jnp.einsum('bqngk,bqknd->bqngd', p, vg)
    return o.reshape(bt, nq, ATT_Q)


def prompt_sparse_attention(q, k, v, qi, ki, wi):
    bt, L = q.shape[:2]
    topk = min(TOPK_MAX, L // 4)
    qb = min(Q_BLOCK, L)
    nb = L // qb
    k_pos = jnp.arange(L)
    bidx = jnp.arange(bt)[:, None, None]

    def block(i):
        s0 = i * qb
        sl = lambda t: lax.dynamic_slice_in_dim(t, s0, qb, axis=1)
        q_pos = s0 + jnp.arange(qb)
        _, sel = lax.top_k(index_scores(sl(qi), sl(wi), ki, q_pos, k_pos), topk)
        valid = sel <= q_pos[None, :, None]
        return sparse_attend(sl(q), k[bidx, sel], v[bidx, sel], valid)

    o = lax.map(block, jnp.arange(nb))
    return o.transpose(1, 0, 2, 3).reshape(bt, L, ATT_Q)


def sample_sparse_attention(q, k_new, v_new, qi, ki_new, wi, cache_k, cache_v, cache_kidx, page_table, j):
    bt, nq = q.shape[:2]
    past = page_table.shape[1] * PAGE_SIZE
    L = past + nq
    topk = min(TOPK_MAX, L // 4)
    ki_past = cache_kidx[j, page_table].reshape(bt, past, IDX_DH)
    ki_all = jnp.concatenate([ki_past, ki_new], axis=1)
    q_pos = past + jnp.arange(nq)
    _, sel = lax.top_k(index_scores(qi, wi, ki_all, q_pos, jnp.arange(L)), topk)
    valid = sel <= q_pos[None, :, None]
    in_past = (sel < past)[..., None, None]
    bidx = jnp.arange(bt)[:, None, None]
    ps = jnp.minimum(sel, past - 1)
    phys = page_table[bidx, ps // PAGE_SIZE]
    off = ps % PAGE_SIZE
    ns = jnp.clip(sel - past, 0, nq - 1)
    kg = jnp.where(in_past, cache_k[j, phys, off], k_new[bidx, ns])
    vg = jnp.where(in_past, cache_v[j, phys, off], v_new[bidx, ns])
    return sparse_attend(q, kg, vg, valid)


def swiglu(h, w_gate, w_up, w_down):
    return mm3(jax.nn.silu(mm3(h, w_gate)) * mm3(h, w_up), w_down)


def moe_swiglu(h, w_router, w_gate, w_up, w_down):
    bt, L, D = h.shape
    xt = h.reshape(bt * L, D)
    logits = xt @ w_router
    top_v, top_i = lax.top_k(logits, TOP_K_EXPERTS)
    probs = jax.nn.softmax(top_v, axis=-1)
    combine = jnp.sum(jax.nn.one_hot(top_i, N_EXPERTS, dtype=F32) * probs[..., None], axis=1)
    y = jnp.zeros((bt * L, D), F32)
    for e in range(N_EXPERTS):
        he = jax.nn.silu(mm(xt, w_gate[e])) * mm(xt, w_up[e])
        y = y + combine[:, e:e + 1] * mm(he, w_down[e])
    return y.reshape(bt, L, D)


def trunk(x, c, pos, shift0, gla0, rw0, attend, P):
    depth = P['w_ada'].shape[0]
    mod = jnp.einsum('bd,lde->lbe', jax.nn.silu(c), P['w_ada']) + P['b_ada'][:, None, :]
    glas, rws, shifts, ks, vs, kis = [], [], [], [], [], []
    for layer in range(depth):
        sh1, sc1, g1, sh2, sc2, g2 = jnp.split(mod[layer], 6, axis=-1)
        j = layer // 2
        h = modulate(x, sh1, sc1)
        if layer % 2 == 0:
            mix, shf, sg, sr = even_mixer(h, shift0[j], gla0[j], rw0[j], j, P)
            glas.append(sg)
            rws.append(sr)
            shifts.append(shf)
        else:
            q, k, v, qi, ki, wi = odd_qkv(h, pos, j, P)
            mix = mm3(attend(j, q, k, v, qi, ki, wi), P['w_out_odd'][j])
            ks.append(k)
            vs.append(v)
            kis.append(ki)
        x = x + g1[:, None, :] * mix
        h = modulate(x, sh2, sc2)
        if layer % 2 == 0:
            ff = swiglu(h, P['ffn_w_gate'][j], P['ffn_w_up'][j], P['ffn_w_down'][j])
        else:
            ff = moe_swiglu(h, P['moe_router'][j], P['moe_w_gate'][j], P['moe_w_up'][j], P['moe_w_down'][j])
        x = x + g2[:, None, :] * ff
    return x, jnp.stack(glas), jnp.stack(rws), jnp.stack(shifts), jnp.stack(ks), jnp.stack(vs), jnp.stack(kis)


def kernel(x_prompt, x_sample, state_gla, state_rwkv, state_shift, cache_k, cache_v, cache_kidx, page_table, c_prompt, c_sample, w_ada, b_ada, w_in_even, gla_w_gate, gla_b_gate, gla_norm_g, rw_mu, rw_w0, rw_w_decay, rw_a0, rw_w_iclr, rw_w_gate, rw_k_k, rw_k_a, rw_r_k, rw_ln_g, rw_ln_b, w_out_even, w_in_odd, q_norm_g, k_norm_g, kidx_ln_g, kidx_ln_b, w_out_odd, ffn_w_gate, ffn_w_up, ffn_w_down, moe_router, moe_w_gate, moe_w_up, moe_w_down):
    P = dict(w_ada=w_ada, b_ada=b_ada, w_in_even=w_in_even, gla_w_gate=gla_w_gate, gla_b_gate=gla_b_gate,
             gla_norm_g=gla_norm_g, rw_mu=rw_mu, rw_w0=rw_w0, rw_w_decay=rw_w_decay, rw_a0=rw_a0,
             rw_w_iclr=rw_w_iclr, rw_w_gate=rw_w_gate, rw_k_k=rw_k_k, rw_k_a=rw_k_a, rw_r_k=rw_r_k,
             rw_ln_g=rw_ln_g, rw_ln_b=rw_ln_b, w_out_even=w_out_even, w_in_odd=w_in_odd,
             q_norm_g=q_norm_g, k_norm_g=k_norm_g, kidx_ln_g=kidx_ln_g, kidx_ln_b=kidx_ln_b,
             w_out_odd=w_out_odd, ffn_w_gate=ffn_w_gate, ffn_w_up=ffn_w_up, ffn_w_down=ffn_w_down,
             moe_router=moe_router, moe_w_gate=moe_w_gate, moe_w_up=moe_w_up, moe_w_down=moe_w_down)
    n_even = state_gla.shape[0]
    b_p, seq = x_prompt.shape[0], x_prompt.shape[1]
    zero_gla = jnp.zeros((n_even, b_p) + state_gla.shape[2:], state_gla.dtype)
    zero_rw = jnp.zeros((n_even, b_p) + state_rwkv.shape[2:], state_rwkv.dtype)
    zero_shift = jnp.zeros((n_even, b_p) + state_shift.shape[2:], state_shift.dtype)
    pos_p = jnp.arange(seq, dtype=jnp.int32)
    past = page_table.shape[1] * PAGE_SIZE
    pos_s = past + jnp.arange(x_sample.shape[1], dtype=jnp.int32)

    def prompt_attend(j, q, k, v, qi, ki, wi):
        return dsa_prompt_attention(q, k, v, qi, ki, wi)

    def sample_attend(j, q, k, v, qi, ki, wi):
        return sample_sparse_attention(q, k, v, qi, ki, wi, cache_k, cache_v, cache_kidx, page_table, j)

    y_prompt, p_gla, p_rw, p_shift, p_k, p_v, p_kidx = trunk(
        x_prompt, c_prompt, pos_p, zero_shift, zero_gla, zero_rw, prompt_attend, P)
    y_sample, s_gla, s_rw, s_shift, s_k, s_v, s_kidx = trunk(
        x_sample, c_sample, pos_s, state_shift, state_gla, state_rwkv, sample_attend, P)
    return (y_prompt, y_sample, p_gla, p_rw, p_shift, p_k, p_v, p_kidx, s_gla, s_rw, s_shift, s_k, s_v, s_kidx)
```

```python
import functools
import math

import jax
import jax.numpy as jnp
import numpy as np
from jax import lax
from jax.experimental import pallas as pl
from jax.experimental.pallas import tpu as pltpu

F32 = jnp.float32
BF16 = jnp.bfloat16

PAGE_SIZE = 128
GLA_HEADS, GLA_DK, GLA_DV, GLA_GATE_RANK, GLA_TAU, GLA_CHUNK = 4, 64, 128, 16, 16.0, 64
RW_HEADS, RW_DH, RW_W_LORA, RW_A_LORA, RW_G_LORA, RW_LN_EPS = 8, 64, 32, 32, 96, 64e-5
ATT_HEADS, ATT_KV_HEADS, ATT_DH = 16, 4, 64
IDX_HEADS, IDX_DH = 8, 64
TOPK_MAX, Q_BLOCK, ROPE_THETA = 256, 128, 10000.0
N_EXPERTS, TOP_K_EXPERTS = 8, 2
EPS = 1e-6
GLA_QK = GLA_HEADS * GLA_DK
GLA_V = GLA_HEADS * GLA_DV
RW_W = RW_HEADS * RW_DH
P_A = 2 * GLA_QK + 2 * GLA_V + GLA_GATE_RANK
ATT_Q = ATT_HEADS * ATT_DH
ATT_KV = ATT_KV_HEADS * ATT_DH

VMEM_LIMIT_BYTES = 56 * 1024 * 1024
ROW_TILE = 512


def _mm_kernel(x_ref, w_ref, o_ref):
    o_ref[...] = jnp.dot(x_ref[...].astype(BF16), w_ref[...], preferred_element_type=F32)


def mm(x, w):
    m, k = x.shape
    n = w.shape[1]
    tm = min(ROW_TILE, m)
    assert m % tm == 0
    return pl.pallas_call(
        _mm_kernel,
        grid=(m // tm,),
        in_specs=[pl.BlockSpec((tm, k), lambda i: (i, 0)), pl.BlockSpec((k, n), lambda i: (0, 0))],
        out_specs=pl.BlockSpec((tm, n), lambda i: (i, 0)),
        out_shape=jax.ShapeDtypeStruct((m, n), F32),
        compiler_params=pltpu.CompilerParams(dimension_semantics=("arbitrary",),
                                             vmem_limit_bytes=VMEM_LIMIT_BYTES),
    )(x, w.astype(BF16))


def mm3(h, w):
    b, l, d = h.shape
    return mm(h.reshape(b * l, d), w).reshape(b, l, w.shape[1])


RW_CHUNK = 64


def _bdot(a, b):
    return jnp.dot(a.astype(BF16), b.astype(BF16), preferred_element_type=F32)


def _bdot_nt(a, b):
    return lax.dot_general(a.astype(BF16), b.astype(BF16), (((1,), (1,)), ((), ())), preferred_element_type=F32)


def _bdot_tn(a, b):
    return lax.dot_general(a.astype(BF16), b.astype(BF16), (((0,), (0,)), ((), ())), preferred_element_type=F32)


def _split3(x):
    hi = x.astype(BF16)
    r1 = x - hi.astype(F32)
    mid = r1.astype(BF16)
    lo = (r1 - mid.astype(F32)).astype(BF16)
    return hi, mid, lo


def _rwkv_chunk_heads(r, lw, k, v, a, b, h0, tri_bf, strict, incl, eye):
    nh = len(r)
    hs = range(nh)
    cum = []
    for i in hs:
        hi, mid, lo = _split3(lw[i])
        cum.append(jnp.dot(tri_bf, hi, preferred_element_type=F32) + jnp.dot(tri_bf, mid, preferred_element_type=F32)
                   + jnp.dot(tri_bf, lo, preferred_element_type=F32))
    cum_last = [cum[i][RW_CHUNK - 1:RW_CHUNK, :] for i in hs]
    e_neg = [jnp.exp(-cum[i]) for i in hs]
    e_end = [jnp.exp(cum_last[i] - cum[i]) for i in hs]
    a_t = [a[i] * jnp.exp(cum[i] - lw[i]) for i in hs]
    r_t = [r[i] * jnp.exp(cum[i]) for i in hs]
    k_t = [k[i] * e_neg[i] for i in hs]
    b_t = [b[i] * e_neg[i] for i in hs]
    k_e = [k[i] * e_end[i] for i in hs]
    b_e = [b[i] * e_end[i] for i in hs]
    a_ab = [jnp.where(strict, _bdot_nt(a_t[i], b_t[i]), 0.0) for i in hs]
    a_ak = [jnp.where(strict, _bdot_nt(a_t[i], k_t[i]), 0.0) for i in hs]
    a_rb = [jnp.where(incl, _bdot_nt(r_t[i], b_t[i]), 0.0) for i in hs]
    a_rk = [jnp.where(incl, _bdot_nt(r_t[i], k_t[i]), 0.0) for i in hs]
    x = [eye + a_ab[i] for i in hs]
    y = list(a_ab)
    for _ in range(int(math.log2(RW_CHUNK)) - 1):
        y = [_bdot(y[i], y[i]) for i in hs]
        x = [x[i] + _bdot(x[i], y[i]) for i in hs]
    a_p = [_bdot(x[i], a_t[i]) for i in hs]
    akv = [_bdot(a_ak[i], v[i]) for i in hs]
    v_p = [_bdot(x[i], akv[i]) for i in hs]
    r_p = [r_t[i] + _bdot(a_rb[i], a_p[i]) for i in hs]
    y_p = [_bdot(a_rk[i], v[i]) + _bdot(a_rb[i], v_p[i]) for i in hs]
    m_lr = [_bdot_tn(b_e[i], a_p[i]) for i in hs]
    g = [_bdot_tn(k_e[i], v[i]) + _bdot_tn(b_e[i], v_p[i]) for i in hs]
    w_col = [jnp.sum(jnp.where(eye, jnp.exp(cum_last[i]), 0.0), axis=1, keepdims=True) for i in hs]
    y_out = [_bdot(r_p[i], h0[i]) + y_p[i] for i in hs]
    h_new = [w_col[i] * h0[i] + _bdot(m_lr[i], h0[i]) + g[i] for i in hs]
    return y_out, h_new


RW_HEAD_GROUP = 16


def _rwkv_kernel(r_ref, lw_ref, k_ref, v_ref, a_ref, b_ref, h0_ref, y_ref, hout_ref, h_scr):
    c = pl.program_id(0)

    @pl.when(c == 0)
    def _():
        h_scr[...] = h0_ref[...]

    rows = lax.broadcasted_iota(jnp.int32, (RW_CHUNK, RW_CHUNK), 0)
    cols = lax.broadcasted_iota(jnp.int32, (RW_CHUNK, RW_CHUNK), 1)
    strict = rows > cols
    incl = rows >= cols
    eye = rows == cols
    tri_bf = jnp.where(incl, 1.0, 0.0).astype(BF16)
    n_bh = r_ref.shape[0]
    for i0 in range(0, n_bh, RW_HEAD_GROUP):
        ids = range(i0, min(i0 + RW_HEAD_GROUP, n_bh))
        y_out, h_new = _rwkv_chunk_heads(*[[ref[i] for i in ids] for ref in
                                           (r_ref, lw_ref, k_ref, v_ref, a_ref, b_ref, h_scr)],
                                         tri_bf, strict, incl, eye)
        for j, i in enumerate(ids):
            y_ref[i] = y_out[j]
            h_scr[i] = h_new[j]

    @pl.when(c == pl.num_programs(0) - 1)
    def _():
        hout_ref[...] = h_scr[...]


def rwkv7_chunked(r, lw, k, v, a, b, s0):
    bt, L, H, N = r.shape
    assert L % RW_CHUNK == 0
    hm = lambda t: t.transpose(0, 2, 1, 3).reshape(bt * H, L, N)
    h0 = s0.transpose(0, 1, 3, 2).reshape(bt * H, N, N)
    seq_spec = pl.BlockSpec((bt * H, RW_CHUNK, N), lambda c: (0, c, 0))
    st_spec = pl.BlockSpec((bt * H, N, N), lambda c: (0, 0, 0))
    y, h_fin = pl.pallas_call(
        _rwkv_kernel,
        grid=(L // RW_CHUNK,),
        in_specs=[seq_spec] * 6 + [st_spec],
        out_specs=[seq_spec, st_spec],
        out_shape=[jax.ShapeDtypeStruct((bt * H, L, N), F32), jax.ShapeDtypeStruct((bt * H, N, N), F32)],
        scratch_shapes=[pltpu.VMEM((bt * H, N, N), F32)],
        compiler_params=pltpu.CompilerParams(dimension_semantics=("arbitrary",),
                                             vmem_limit_bytes=VMEM_LIMIT_BYTES),
        name="rwkv7_chunked",
    )(hm(r), hm(lw), hm(k), hm(v), hm(a), hm(b), h0)
    y = y.reshape(bt, H, L, N).transpose(0, 2, 1, 3)
    return y, h_fin.reshape(bt, H, N, N).transpose(0, 1, 3, 2)


def _gla_kernel(q_ref, k_ref, v_ref, g_ref, s0_ref, o_ref, sout_ref, s_scr):
    c = pl.program_id(0)

    @pl.when(c == 0)
    def _():
        s_scr[...] = s0_ref[...]

    rows = lax.broadcasted_iota(jnp.int32, (GLA_CHUNK, GLA_CHUNK), 0)
    cols = lax.broadcasted_iota(jnp.int32, (GLA_CHUNK, GLA_CHUNK), 1)
    incl = rows >= cols
    eye = lax.broadcasted_iota(jnp.int32, (GLA_DK, GLA_DK), 0) == lax.broadcasted_iota(jnp.int32, (GLA_DK, GLA_DK), 1)
    tri_bf = jnp.where(incl, 1.0, 0.0).astype(BF16)
    hs = range(q_ref.shape[0])
    bcum = []
    for i in hs:
        hi, mid, lo = _split3(g_ref[i])
        bcum.append(jnp.dot(tri_bf, hi, preferred_element_type=F32) + jnp.dot(tri_bf, mid, preferred_element_type=F32)
                    + jnp.dot(tri_bf, lo, preferred_element_type=F32))
    blast = [bcum[i][GLA_CHUNK - 1:GLA_CHUNK, :] for i in hs]
    q_e = [q_ref[i] * jnp.exp(bcum[i]) for i in hs]
    k_e = [k_ref[i] * jnp.exp(-bcum[i]) for i in hs]
    k_end = [k_ref[i] * jnp.exp(blast[i] - bcum[i]) for i in hs]
    att = [jnp.where(incl, _bdot_nt(q_e[i], k_e[i]), 0.0) for i in hs]
    s_old = [s_scr[i] for i in hs]
    o = [_bdot(att[i], v_ref[i]) + _bdot(q_e[i], s_old[i]) for i in hs]
    kv = [_bdot_tn(k_end[i], v_ref[i]) for i in hs]
    dec = [jnp.sum(jnp.where(eye, jnp.exp(blast[i]), 0.0), axis=1, keepdims=True) for i in hs]
    for i in hs:
        o_ref[i] = o[i]
        s_scr[i] = dec[i] * s_old[i] + kv[i]

    @pl.when(c == pl.num_programs(0) - 1)
    def _():
        sout_ref[...] = s_scr[...]


def gla_chunked_pallas(q, k, v, log_a, s0):
    bt, L, H, dk = q.shape
    dv = v.shape[-1]
    assert L % GLA_CHUNK == 0
    hm = lambda t: t.transpose(0, 2, 1, 3).reshape(bt * H, L, t.shape[-1])
    qk_spec = pl.BlockSpec((bt * H, GLA_CHUNK, dk), lambda c: (0, c, 0))
    v_spec = pl.BlockSpec((bt * H, GLA_CHUNK, dv), lambda c: (0, c, 0))
    st_spec = pl.BlockSpec((bt * H, dk, dv), lambda c: (0, 0, 0))
    o, s_fin = pl.pallas_call(
        _gla_kernel,
        grid=(L // GLA_CHUNK,),
        in_specs=[qk_spec, qk_spec, v_spec, qk_spec, st_spec],
        out_specs=[v_spec, st_spec],
        out_shape=[jax.ShapeDtypeStruct((bt * H, L, dv), F32), jax.ShapeDtypeStruct((bt * H, dk, dv), F32)],
        scratch_shapes=[pltpu.VMEM((bt * H, dk, dv), F32)],
        compiler_params=pltpu.CompilerParams(dimension_semantics=("arbitrary",),
                                             vmem_limit_bytes=VMEM_LIMIT_BYTES),
        name="gla_chunked",
    )(hm(q), hm(k), hm(v), hm(log_a), s0.reshape(bt * H, dk, dv))
    return o.reshape(bt, H, L, dv).transpose(0, 2, 1, 3), s_fin.reshape(bt, H, dk, dv)


DSA_QB = 128
DSA_KC = 512
LANES = 128
MASK_NEG = -1e30
INT_MIN = -2 ** 31
ONE = np.float32(1.0)
ZERO = np.float32(0.0)
G_PER_KV = ATT_HEADS // ATT_KV_HEADS


def _sortable_key(x):
    bits = lax.bitcast_convert_type(x + 0.0, jnp.int32)
    return bits ^ ((bits >> 31) & 0x7FFFFFFF)


def _dsa_kernel(topk, q_ref, qi_ref, wi_ref, k_ref, v_ref, kit_ref, upper_ref, o_ref,
                qbd_scr, qi_scr, wb_scr, key_scr, m_scr, acc_scr):
    i = pl.program_id(1)
    n_chunks = (i * DSA_QB) // DSA_KC + 1
    nt = DSA_KC // LANES
    q_pos = i * DSA_QB + lax.broadcasted_iota(jnp.int32, (DSA_QB, LANES), 0)
    lane = lax.broadcasted_iota(jnp.int32, (DSA_QB, LANES), 1)

    qbd_scr[...] = jnp.zeros(qbd_scr.shape, BF16)
    for n in range(ATT_KV_HEADS):
        for g in range(G_PER_KV):
            hd = n * G_PER_KV + g
            piece = q_ref[0, :, hd * ATT_DH:(hd + 1) * ATT_DH] * (ATT_DH ** -0.5)
            qbd_scr[hd * DSA_QB:(hd + 1) * DSA_QB, n * ATT_DH:(n + 1) * ATT_DH] = piece.astype(BF16)
    for h in range(IDX_HEADS):
        qi_scr[h * DSA_QB:(h + 1) * DSA_QB, :] = qi_ref[0, :, h * IDX_DH:(h + 1) * IDX_DH].astype(BF16)
        wb_scr[h] = jnp.broadcast_to(wi_ref[0, :, h:h + 1], (DSA_QB, LANES))

    def score_chunk(c, carry):
        dots = jnp.dot(qi_scr[...], kit_ref[0, c], preferred_element_type=F32)
        for jt in range(nt):
            sc = jnp.zeros((DSA_QB, LANES), F32)
            for h in range(IDX_HEADS):
                d = dots[h * DSA_QB:(h + 1) * DSA_QB, jt * LANES:(jt + 1) * LANES]
                sc = sc + jnp.maximum(d, 0.0) * wb_scr[h]
            k_pos = c * DSA_KC + jt * LANES + lane
            sc = jnp.where(k_pos <= q_pos, sc, -jnp.inf)
            key_scr[c, :, jt * LANES:(jt + 1) * LANES] = _sortable_key(sc)
        return carry

    lax.fori_loop(0, n_chunks, score_chunk, 0)

    def count_ge(cand):
        cand_b = jnp.broadcast_to(cand, (DSA_QB, LANES))

        def body(c, acc):
            for jt in range(nt):
                acc = acc + jnp.where(key_scr[c, :, jt * LANES:(jt + 1) * LANES] >= cand_b, ONE, ZERO)
            return acc

        acc = lax.fori_loop(0, n_chunks, body, jnp.zeros((DSA_QB, LANES), F32))
        return jnp.sum(acc, axis=1, keepdims=True)

    kf = float(topk)
    cur = jnp.where(count_ge(jnp.zeros((DSA_QB, 1), jnp.int32)) >= kf, 0, INT_MIN).astype(jnp.int32)

    def bit_step(p, cur):
        cand = cur | (jnp.int32(1) << (30 - p))
        return jnp.where(count_ge(cand) >= kf, cand, cur)

    thr = lax.fori_loop(0, 31, bit_step, cur)
    thr_b = jnp.broadcast_to(thr, (DSA_QB, LANES))

    def count_gt_body(c, acc):
        for jt in range(nt):
            acc = acc + jnp.where(key_scr[c, :, jt * LANES:(jt + 1) * LANES] > thr_b, ONE, ZERO)
        return acc

    n_gt = jnp.sum(lax.fori_loop(0, n_chunks, count_gt_body, jnp.zeros((DSA_QB, LANES), F32)), axis=1, keepdims=True)
    need = kf - n_gt

    m_scr[...] = jnp.full(m_scr.shape, MASK_NEG, F32)
    acc_scr[...] = jnp.zeros(acc_scr.shape, F32)

    def attend_chunk(c, tie_carry):
        k0 = pl.multiple_of(c * DSA_KC, DSA_KC)
        keys = key_scr[c]
        thr_c = jnp.broadcast_to(thr, (DSA_QB, DSA_KC))
        eq = keys == thr_c
        eq_f = jnp.where(eq, ONE, ZERO)
        rank = tie_carry + jnp.dot(eq_f.astype(BF16), upper_ref[...], preferred_element_type=F32)
        k_pos = k0 + lax.broadcasted_iota(jnp.int32, (DSA_QB, DSA_KC), 1)
        qp = i * DSA_QB + lax.broadcasted_iota(jnp.int32, (DSA_QB, DSA_KC), 0)
        sel = jnp.where(keys > thr_c, ONE, jnp.where(rank < need, eq_f, ZERO))
        bias = jnp.where((sel > 0.5) & (k_pos <= qp), ZERO, np.float32(MASK_NEG))
        k_c = k_ref[0, pl.ds(k0, DSA_KC), :]
        v_c = v_ref[0, pl.ds(k0, DSA_KC), :]
        rows = G_PER_KV * DSA_QB
        for n in range(ATT_KV_HEADS):
            s = lax.dot_general(qbd_scr[n * rows:(n + 1) * rows, :], k_c, (((1,), (1,)), ((), ())),
                                preferred_element_type=F32)
            s = (s.reshape(G_PER_KV, DSA_QB, DSA_KC) + bias[None]).reshape(rows, DSA_KC)
            tiles = [s[:, jt * LANES:(jt + 1) * LANES] for jt in range(nt)]
            tile_max = functools.reduce(jnp.maximum, tiles)
            m_old = m_scr[n]
            m_new = jnp.maximum(m_old, jnp.max(tile_max, axis=1, keepdims=True))
            alpha = jnp.exp(m_old - m_new)
            p = jnp.concatenate([jnp.exp(t - m_new) for t in tiles], axis=1)
            pv = jnp.dot(p.astype(BF16), v_c[:, n * LANES:(n + 1) * LANES], preferred_element_type=F32)
            acc_scr[n] = alpha * acc_scr[n] + pv
            m_scr[n] = m_new
        return tie_carry + jnp.sum(eq_f, axis=1, keepdims=True)

    lax.fori_loop(0, n_chunks, attend_chunk, jnp.zeros((DSA_QB, 1), F32))

    for n in range(ATT_KV_HEADS):
        acc = acc_scr[n]
        o_n = acc[:, :ATT_DH] / acc[:, ATT_DH:]
        for g in range(G_PER_KV):
            hd = n * G_PER_KV + g
            o_ref[0, :, hd * ATT_DH:(hd + 1) * ATT_DH] = o_n[g * DSA_QB:(g + 1) * DSA_QB, :]


def dsa_prompt_attention(q, k, v, qi, ki, wi):
    bt, L = q.shape[:2]
    assert L % DSA_KC == 0 and L // 4 >= 1
    topk = min(TOPK_MAX, L // 4)
    nb = L // DSA_QB
    nc = L // DSA_KC
    kit = ki.astype(BF16).reshape(bt, nc, DSA_KC, IDX_DH).transpose(0, 1, 3, 2)
    upper = jnp.triu(jnp.ones((DSA_KC, DSA_KC), BF16), 1)
    rows = G_PER_KV * DSA_QB
    v_aug = jnp.concatenate([v.astype(BF16), jnp.ones(v.shape, BF16)], axis=-1).reshape(bt, L, ATT_KV_HEADS * LANES)
    return pl.pallas_call(
        functools.partial(_dsa_kernel, topk),
        grid=(bt, nb),
        in_specs=[
            pl.BlockSpec((1, DSA_QB, ATT_Q), lambda b, i: (b, i, 0)),
            pl.BlockSpec((1, DSA_QB, IDX_HEADS * IDX_DH), lambda b, i: (b, i, 0)),
            pl.BlockSpec((1, DSA_QB, IDX_HEADS), lambda b, i: (b, i, 0)),
            pl.BlockSpec((1, L, ATT_KV), lambda b, i: (b, 0, 0)),
            pl.BlockSpec((1, L, ATT_KV_HEADS * LANES), lambda b, i: (b, 0, 0)),
            pl.BlockSpec((1, nc, IDX_DH, DSA_KC), lambda b, i: (b, 0, 0, 0)),
            pl.BlockSpec((DSA_KC, DSA_KC), lambda b, i: (0, 0)),
        ],
        out_specs=pl.BlockSpec((1, DSA_QB, ATT_Q), lambda b, i: (b, i, 0)),
        out_shape=jax.ShapeDtypeStruct((bt, L, ATT_Q), F32),
        scratch_shapes=[
            pltpu.VMEM((ATT_HEADS * DSA_QB, ATT_KV), BF16),
            pltpu.VMEM((IDX_HEADS * DSA_QB, IDX_DH), BF16),
            pltpu.VMEM((IDX_HEADS, DSA_QB, LANES), F32),
            pltpu.VMEM((nc, DSA_QB, DSA_KC), jnp.int32),
            pltpu.VMEM((ATT_KV_HEADS, rows, LANES), F32),
            pltpu.VMEM((ATT_KV_HEADS, rows, LANES), F32),
        ],
        compiler_params=pltpu.CompilerParams(dimension_semantics=("arbitrary", "arbitrary"),
                                             vmem_limit_bytes=VMEM_LIMIT_BYTES),
        name="dsa_prompt_attention",
    )(q.reshape(bt, L, ATT_Q), qi.reshape(bt, L, IDX_HEADS * IDX_DH), wi,
      k.reshape(bt, L, ATT_KV).astype(BF16), v_aug, kit, upper)


NEW_PAD = 16


def _dsa_sample_kernel(topk, n_new, pt_ref, qbd_ref, qi_ref, wrep_ref, knew_ref, vnew_ref, kinew_ref,
                       ck_ref, cv_ref, cki_ref, upper_ref, o_ref, kbuf, vbuf, kibuf):
    p = pl.program_id(1)
    n_pages = pl.num_programs(1)
    past = n_pages * PAGE_SIZE
    n_keys = kbuf.shape[0]
    nq = qbd_ref.shape[1] // ATT_HEADS
    row0 = pl.multiple_of(p * PAGE_SIZE, PAGE_SIZE)
    kbuf[pl.ds(row0, PAGE_SIZE), :] = ck_ref[0, 0].astype(BF16)
    vbuf[pl.ds(row0, PAGE_SIZE), :] = cv_ref[0, 0].astype(BF16)
    kibuf[pl.ds(row0, PAGE_SIZE), :] = cki_ref[0, 0].astype(BF16)

    @pl.when(p == n_pages - 1)
    def _():
        tail = n_keys - past
        kbuf[past:, :] = jnp.zeros((tail, ATT_KV), BF16)
        vbuf[past:, :] = jnp.zeros((tail, ATT_KV), BF16)
        kibuf[past:, :] = jnp.zeros((tail, IDX_DH), BF16)
        kbuf[past:past + NEW_PAD, :] = knew_ref[0].astype(BF16)
        vbuf[past:past + NEW_PAD, :] = vnew_ref[0].astype(BF16)
        kibuf[past:past + NEW_PAD, :] = kinew_ref[0].astype(BF16)

        dots = lax.dot_general(qi_ref[0], kibuf[...], (((1,), (1,)), ((), ())), preferred_element_type=F32)
        k_idx = lax.broadcasted_iota(jnp.int32, (8, n_keys), 1)
        q_row = lax.broadcasted_iota(jnp.int32, (8, n_keys), 0)
        valid = (k_idx - past <= q_row) & (q_row < nq)
        rows = []
        for qn in range(nq):
            d = jnp.maximum(dots[qn * IDX_HEADS:(qn + 1) * IDX_HEADS], 0.0)
            w = wrep_ref[0, qn * IDX_HEADS:(qn + 1) * IDX_HEADS, :]
            parts = []
            for jt in range(n_keys // LANES):
                parts.append(jnp.sum(d[:, jt * LANES:(jt + 1) * LANES] * w, axis=0, keepdims=True))
            rows.append(jnp.concatenate(parts, axis=1))
        rows.append(jnp.zeros((8 - nq, n_keys), F32))
        sc = jnp.where(valid, jnp.concatenate(rows, axis=0), -jnp.inf)
        keys = _sortable_key(sc)

        kf = float(topk)

        def count_ge(cand):
            return jnp.sum(jnp.where(keys >= cand, ONE, ZERO), axis=1, keepdims=True)

        cur = jnp.where(count_ge(jnp.zeros((8, 1), jnp.int32)) >= kf, 0, INT_MIN).astype(jnp.int32)

        def bit_step(i, cur):
            cand = cur | (jnp.int32(1) << (30 - i))
            return jnp.where(count_ge(cand) >= kf, cand, cur)

        thr = lax.fori_loop(0, 31, bit_step, cur)
        n_gt = jnp.sum(jnp.where(keys > thr, ONE, ZERO), axis=1, keepdims=True)
        need = kf - n_gt
        eq_f = jnp.where(keys == thr, ONE, ZERO)
        carry = jnp.zeros((8, 1), F32)
        ranks = []
        for jt in range(n_keys // LANES):
            e = eq_f[:, jt * LANES:(jt + 1) * LANES]
            ranks.append(carry + jnp.dot(e.astype(BF16), upper_ref[...], preferred_element_type=F32))
            carry = carry + jnp.sum(e, axis=1, keepdims=True)
        rank = jnp.concatenate(ranks, axis=1)
        sel = jnp.where(keys > thr, ONE, jnp.where(rank < need, eq_f, ZERO))
        bias = jnp.where((sel > 0.5) & valid, ZERO, np.float32(MASK_NEG))

        s = lax.dot_general(qbd_ref[0], kbuf[...], (((1,), (1,)), ((), ())), preferred_element_type=F32)
        s = jnp.concatenate([s[qn * ATT_HEADS:(qn + 1) * ATT_HEADS] + bias[qn:qn + 1] for qn in range(nq)], axis=0)
        m = jnp.max(s, axis=1, keepdims=True)
        pr = jnp.exp(s - m)
        l = jnp.sum(pr, axis=1, keepdims=True)
        o_ref[0] = jnp.dot(pr.astype(BF16), vbuf[...], preferred_element_type=F32) / l


def dsa_sample_attention(q, k_new, v_new, qi, ki_new, wi, cache_k, cache_v, cache_kidx, page_table, j):
    bt, nq = q.shape[:2]
    n_pages = page_table.shape[1]
    past = n_pages * PAGE_SIZE
    assert nq <= 8 and nq <= NEW_PAD
    topk = min(TOPK_MAX, (past + nq) // 4)
    n_keys = past + LANES
    n_pool = cache_k.shape[1]
    qs = (q * ATT_DH ** -0.5).astype(BF16).reshape(bt, nq, ATT_KV_HEADS, G_PER_KV, 1, ATT_DH)
    eye = jnp.eye(ATT_KV_HEADS, dtype=BF16)[None, None, :, None, :, None]
    qbd = (qs * eye).reshape(bt, nq * ATT_HEADS, ATT_KV)
    qi2 = qi.astype(BF16).reshape(bt, nq * IDX_HEADS, IDX_DH)
    wrep = jnp.broadcast_to(wi.reshape(bt, nq * IDX_HEADS, 1), (bt, nq * IDX_HEADS, LANES))
    pad = lambda t: jnp.pad(t.reshape(bt, nq, -1), ((0, 0), (0, NEW_PAD - nq), (0, 0)))
    upper = jnp.triu(jnp.ones((LANES, LANES), BF16), 1)
    per_b = lambda shape: pl.BlockSpec((1,) + shape, lambda b, p, pt: (b, 0, 0))
    page = lambda width: pl.BlockSpec((1, 1, PAGE_SIZE, width), lambda b, p, pt: (j, pt[b, p], 0, 0))
    o_all = pl.pallas_call(
        functools.partial(_dsa_sample_kernel, topk, nq),
        grid_spec=pltpu.PrefetchScalarGridSpec(
            num_scalar_prefetch=1,
            grid=(bt, n_pages),
            in_specs=[per_b((nq * ATT_HEADS, ATT_KV)), per_b((nq * IDX_HEADS, IDX_DH)), per_b((nq * IDX_HEADS, LANES)),
                      per_b((NEW_PAD, ATT_KV)), per_b((NEW_PAD, ATT_KV)), per_b((NEW_PAD, IDX_DH)),
                      page(ATT_KV), page(ATT_KV), page(IDX_DH),
                      pl.BlockSpec((LANES, LANES), lambda b, p, pt: (0, 0))],
            out_specs=per_b((nq * ATT_HEADS, ATT_KV)),
            scratch_shapes=[pltpu.VMEM((n_keys, ATT_KV), BF16), pltpu.VMEM((n_keys, ATT_KV), BF16),
                            pltpu.VMEM((n_keys, IDX_DH), BF16)]),
        out_shape=jax.ShapeDtypeStruct((bt, nq * ATT_HEADS, ATT_KV), F32),
        compiler_params=pltpu.CompilerParams(dimension_semantics=("arbitrary", "arbitrary"),
                                             vmem_limit_bytes=VMEM_LIMIT_BYTES),
        name="dsa_sample_attention",
    )(page_table, qbd, qi2, wrep, pad(k_new), pad(v_new), pad(ki_new),
      cache_k.reshape(cache_k.shape[0], n_pool, PAGE_SIZE, ATT_KV),
      cache_v.reshape(cache_v.shape[0], n_pool, PAGE_SIZE, ATT_KV), cache_kidx, upper)
    o6 = o_all.reshape(bt, nq, ATT_KV_HEADS, G_PER_KV, ATT_KV_HEADS, ATT_DH)
    o = jnp.stack([o6[:, :, n, :, n, :] for n in range(ATT_KV_HEADS)], axis=2)
    return o.reshape(bt, nq, ATT_Q)


def rms_norm(x, eps=EPS):
    return x * lax.rsqrt(jnp.mean(x * x, axis=-1, keepdims=True) + eps)


def layer_norm(x, g, b, eps):
    mu = jnp.mean(x, axis=-1, keepdims=True)
    var = jnp.mean(jnp.square(x - mu), axis=-1, keepdims=True)
    return (x - mu) * lax.rsqrt(var + eps) * g + b


def modulate(x, shift, scale):
    return rms_norm(x) * (1 + scale[:, None, :]) + shift[:, None, :]


def rotary(x, pos):
    half = x.shape[-1] // 2
    inv_freq = jnp.power(ROPE_THETA, -jnp.arange(half, dtype=F32) / half)
    ang = pos.astype(F32)[:, None] * inv_freq[None, :]
    cos = jnp.cos(ang)[None, :, None, :]
    sin = jnp.sin(ang)[None, :, None, :]
    x1, x2 = x[..., :half], x[..., half:]
    return jnp.concatenate([x1 * cos - x2 * sin, x2 * cos + x1 * sin], axis=-1)


def gla_chunked(q, k, v, log_a, s0):
    bt, L, H, _ = q.shape
    dv = v.shape[-1]
    C = min(GLA_CHUNK, L)
    n = -(-L // C)
    pad = n * C - L

    def prep(t):
        t = jnp.pad(t, ((0, 0), (0, pad), (0, 0), (0, 0)))
        return t.reshape(bt, n, C, H, t.shape[-1]).transpose(1, 0, 3, 2, 4)

    qc, kc, vc, gc = prep(q), prep(k), prep(v), prep(log_a)
    bcum = jnp.cumsum(gc, axis=3)
    blast = bcum[:, :, :, -1:, :]
    q_e = qc * jnp.exp(bcum)
    k_e = kc * jnp.exp(-bcum)
    k_end = kc * jnp.exp(blast - bcum)
    causal = jnp.tril(jnp.ones((C, C), dtype=bool))
    att = jnp.where(causal, jnp.einsum('nbhtd,nbhsd->nbhts', q_e, k_e), 0.0)
    o_intra = jnp.einsum('nbhts,nbhsv->nbhtv', att, vc)
    decay = jnp.exp(blast[:, :, :, 0, :])
    kv = jnp.einsum('nbhsd,nbhsv->nbhdv', k_end, vc)

    def step(s, inp):
        dec, kv_i = inp
        return dec[..., None] * s + kv_i, s

    s_fin, s_start = lax.scan(step, s0, (decay, kv))
    o = o_intra + jnp.einsum('nbhtd,nbhdv->nbhtv', q_e, s_start)
    o = o.transpose(1, 0, 3, 2, 4).reshape(bt, n * C, H, dv)[:, :L]
    return o, s_fin


def rwkv7_scan(r, w, k, v, a, b, s0):
    xs = tuple(t.transpose(1, 0, 2, 3) for t in (r, w, k, v, a, b))

    def step(s, inp):
        r_t, w_t, k_t, v_t, a_t, b_t = inp
        sa = jnp.einsum('bhvk,bhk->bhv', s, a_t)
        s = s * w_t[:, :, None, :] + sa[..., None] * b_t[:, :, None, :] + v_t[..., None] * k_t[:, :, None, :]
        return s, jnp.einsum('bhvk,bhk->bhv', s, r_t)

    s_fin, y = lax.scan(step, s0, xs)
    return y.transpose(1, 0, 2, 3), s_fin


def even_mixer(h, shift_prev, s_gla, s_rw, j, P):
    bt, L, _ = h.shape
    proj = mm3(h, P['w_in_even'][j])
    pa, pb_raw = proj[..., :P_A], proj[..., P_A:]
    q_a, k_a, v_a, r_a, g_low = jnp.split(pa, [GLA_QK, 2 * GLA_QK, 2 * GLA_QK + GLA_V, 2 * GLA_QK + 2 * GLA_V], axis=-1)
    ga = lambda t, d: t.reshape(bt, L, GLA_HEADS, d)
    log_alpha = jax.nn.log_sigmoid(g_low @ P['gla_w_gate'][j] + P['gla_b_gate'][j]) / GLA_TAU
    gla = gla_chunked_pallas if L % GLA_CHUNK == 0 else gla_chunked
    o_a, s_gla_new = gla(ga(q_a, GLA_DK) * GLA_DK ** -0.5, ga(k_a, GLA_DK), ga(v_a, GLA_DV),
                         ga(log_alpha, GLA_DK), s_gla)
    o_a = (rms_norm(o_a) * P['gla_norm_g'][j]).reshape(bt, L, GLA_V) * jax.nn.silu(r_a)
    prev = jnp.concatenate([shift_prev[:, None, :], pb_raw[:, :-1]], axis=1)
    pb = pb_raw + (prev - pb_raw) * P['rw_mu'][j]
    o1 = 3 * RW_W + RW_W_LORA
    r_b, k_b, v_b, w_low, a_low, gt_low = jnp.split(pb, [RW_W, 2 * RW_W, 3 * RW_W, o1, o1 + RW_A_LORA], axis=-1)
    w_log = -jax.nn.softplus(-(P['rw_w0'][j] + jnp.tanh(w_low) @ P['rw_w_decay'][j])) - 0.5
    log_decay = -jnp.exp(w_log)
    iclr = jax.nn.sigmoid(P['rw_a0'][j] + a_low @ P['rw_w_iclr'][j])
    gate = jax.nn.sigmoid(gt_low) @ P['rw_w_gate'][j]
    hb = lambda t: t.reshape(bt, L, RW_HEADS, RW_DH)
    kk = hb(k_b * P['rw_k_k'][j])
    kk = kk / jnp.maximum(jnp.sqrt(jnp.sum(kk * kk, axis=-1, keepdims=True)), 1e-12)
    k_b = k_b * (1 + (iclr - 1) * P['rw_k_a'][j])
    r_h, k_h, v_h, a_h = hb(r_b), hb(k_b), hb(v_b), hb(iclr)
    if L % RW_CHUNK == 0:
        y_b, s_rw_new = rwkv7_chunked(r_h, hb(log_decay), k_h, v_h, -kk, kk * a_h, s_rw)
    else:
        y_b, s_rw_new = rwkv7_scan(r_h, hb(jnp.exp(log_decay)), k_h, v_h, -kk, kk * a_h, s_rw)
    y_b = layer_norm(y_b, P['rw_ln_g'][j].reshape(RW_HEADS, RW_DH), P['rw_ln_b'][j].reshape(RW_HEADS, RW_DH), RW_LN_EPS)
    y_b = y_b + jnp.sum(r_h * k_h * P['rw_r_k'][j], axis=-1, keepdims=True) * v_h
    o_b = y_b.reshape(bt, L, RW_W) * gate
    out = mm3(jnp.concatenate([o_a, o_b], axis=-1), P['w_out_even'][j])
    return out, pb_raw[:, -1], s_gla_new, s_rw_new


def odd_qkv(h, pos, j, P):
    bt, L, _ = h.shape
    proj = mm3(h, P['w_in_odd'][j])
    o2 = ATT_Q + ATT_KV
    o3 = o2 + ATT_KV
    o4 = o3 + IDX_HEADS * IDX_DH
    q, k, v, qi, ki, wi = jnp.split(proj, [ATT_Q, o2, o3, o4, o4 + IDX_DH], axis=-1)
    q = rotary(rms_norm(q.reshape(bt, L, ATT_HEADS, ATT_DH)) * P['q_norm_g'][j], pos)
    k = rotary(rms_norm(k.reshape(bt, L, ATT_KV_HEADS, ATT_DH)) * P['k_norm_g'][j], pos)
    v = v.reshape(bt, L, ATT_KV_HEADS, ATT_DH)
    qi = rotary(qi.reshape(bt, L, IDX_HEADS, IDX_DH), pos)
    ki = rotary(layer_norm(ki, P['kidx_ln_g'][j], P['kidx_ln_b'][j], EPS)[:, :, None, :], pos)[:, :, 0, :]
    wi = wi * (IDX_HEADS * IDX_DH) ** -0.5
    return q, k, v, qi, ki, wi


def swiglu(h, w_gate, w_up, w_down):
    return mm3(jax.nn.silu(mm3(h, w_gate)) * mm3(h, w_up), w_down)


def moe_swiglu(h, w_router, w_gate, w_up, w_down):
    bt, L, D = h.shape
    xt = h.reshape(bt * L, D)
    logits = xt @ w_router
    top_v, top_i = lax.top_k(logits, TOP_K_EXPERTS)
    probs = jax.nn.softmax(top_v, axis=-1)
    combine = jnp.sum(jax.nn.one_hot(top_i, N_EXPERTS, dtype=F32) * probs[..., None], axis=1)
    y = jnp.zeros((bt * L, D), F32)
    for e in range(N_EXPERTS):
        he = jax.nn.silu(mm(xt, w_gate[e])) * mm(xt, w_up[e])
        y = y + combine[:, e:e + 1] * mm(he, w_down[e])
    return y.reshape(bt, L, D)


def trunk(x, c, pos, shift0, gla0, rw0, attend, P):
    depth = P['w_ada'].shape[0]
    mod = jnp.einsum('bd,lde->lbe', jax.nn.silu(c), P['w_ada']) + P['b_ada'][:, None, :]
    glas, rws, shifts, ks, vs, kis = [], [], [], [], [], []
    for layer in range(depth):
        sh1, sc1, g1, sh2, sc2, g2 = jnp.split(mod[layer], 6, axis=-1)
        j = layer // 2
        h = modulate(x, sh1, sc1)
        if layer % 2 == 0:
            mix, shf, sg, sr = even_mixer(h, shift0[j], gla0[j], rw0[j], j, P)
            glas.append(sg)
            rws.append(sr)
            shifts.append(shf)
        else:
            q, k, v, qi, ki, wi = odd_qkv(h, pos, j, P)
            mix = mm3(attend(j, q, k, v, qi, ki, wi), P['w_out_odd'][j])
            ks.append(k)
            vs.append(v)
            kis.append(ki)
        x = x + g1[:, None, :] * mix
        h = modulate(x, sh2, sc2)
        if layer % 2 == 0:
            ff = swiglu(h, P['ffn_w_gate'][j], P['ffn_w_up'][j], P['ffn_w_down'][j])
        else:
            ff = moe_swiglu(h, P['moe_router'][j], P['moe_w_gate'][j], P['moe_w_up'][j], P['moe_w_down'][j])
        x = x + g2[:, None, :] * ff
    return x, jnp.stack(glas), jnp.stack(rws), jnp.stack(shifts), jnp.stack(ks), jnp.stack(vs), jnp.stack(kis)


def kernel(x_prompt, x_sample, state_gla, state_rwkv, state_shift, cache_k, cache_v, cache_kidx, page_table, c_prompt, c_sample, w_ada, b_ada, w_in_even, gla_w_gate, gla_b_gate, gla_norm_g, rw_mu, rw_w0, rw_w_decay, rw_a0, rw_w_iclr, rw_w_gate, rw_k_k, rw_k_a, rw_r_k, rw_ln_g, rw_ln_b, w_out_even, w_in_odd, q_norm_g, k_norm_g, kidx_ln_g, kidx_ln_b, w_out_odd, ffn_w_gate, ffn_w_up, ffn_w_down, moe_router, moe_w_gate, moe_w_up, moe_w_down):
    P = dict(w_ada=w_ada, b_ada=b_ada, w_in_even=w_in_even, gla_w_gate=gla_w_gate, gla_b_gate=gla_b_gate,
             gla_norm_g=gla_norm_g, rw_mu=rw_mu, rw_w0=rw_w0, rw_w_decay=rw_w_decay, rw_a0=rw_a0,
             rw_w_iclr=rw_w_iclr, rw_w_gate=rw_w_gate, rw_k_k=rw_k_k, rw_k_a=rw_k_a, rw_r_k=rw_r_k,
             rw_ln_g=rw_ln_g, rw_ln_b=rw_ln_b, w_out_even=w_out_even, w_in_odd=w_in_odd,
             q_norm_g=q_norm_g, k_norm_g=k_norm_g, kidx_ln_g=kidx_ln_g, kidx_ln_b=kidx_ln_b,
             w_out_odd=w_out_odd, ffn_w_gate=ffn_w_gate, ffn_w_up=ffn_w_up, ffn_w_down=ffn_w_down,
             moe_router=moe_router, moe_w_gate=moe_w_gate, moe_w_up=moe_w_up, moe_w_down=moe_w_down)
    n_even = state_gla.shape[0]
    b_p, seq = x_prompt.shape[0], x_prompt.shape[1]
    zero_gla = jnp.zeros((n_even, b_p) + state_gla.shape[2:], state_gla.dtype)
    zero_rw = jnp.zeros((n_even, b_p) + state_rwkv.shape[2:], state_rwkv.dtype)
    zero_shift = jnp.zeros((n_even, b_p) + state_shift.shape[2:], state_shift.dtype)
    pos_p = jnp.arange(seq, dtype=jnp.int32)
    past = page_table.shape[1] * PAGE_SIZE
    pos_s = past + jnp.arange(x_sample.shape[1], dtype=jnp.int32)

    def prompt_attend(j, q, k, v, qi, ki, wi):
        return dsa_prompt_attention(q, k, v, qi, ki, wi)

    def sample_attend(j, q, k, v, qi, ki, wi):
        return dsa_sample_attention(q, k, v, qi, ki, wi, cache_k, cache_v, cache_kidx, page_table, j)

    y_prompt, p_gla, p_rw, p_shift, p_k, p_v, p_kidx = trunk(
        x_prompt, c_prompt, pos_p, zero_shift, zero_gla, zero_rw, prompt_attend, P)
    y_sample, s_gla, s_rw, s_shift, s_k, s_v, s_kidx = trunk(
        x_sample, c_sample, pos_s, state_shift, state_gla, state_rwkv, sample_attend, P)
    return (y_prompt, y_sample, p_gla, p_rw, p_shift, p_k, p_v, p_kidx, s_gla, s_rw, s_shift, s_k, s_v, s_kidx)
```

```python
import functools
import math

import jax
import jax.numpy as jnp
import numpy as np
from jax import lax
from jax.experimental import pallas as pl
from jax.experimental.pallas import tpu as pltpu

F32 = jnp.float32
BF16 = jnp.bfloat16

PAGE_SIZE = 128
GLA_HEADS, GLA_DK, GLA_DV, GLA_GATE_RANK, GLA_TAU, GLA_CHUNK = 4, 64, 128, 16, 16.0, 64
RW_HEADS, RW_DH, RW_W_LORA, RW_A_LORA, RW_G_LORA, RW_LN_EPS = 8, 64, 32, 32, 96, 64e-5
ATT_HEADS, ATT_KV_HEADS, ATT_DH = 16, 4, 64
IDX_HEADS, IDX_DH = 8, 64
TOPK_MAX, Q_BLOCK, ROPE_THETA = 256, 128, 10000.0
N_EXPERTS, TOP_K_EXPERTS = 8, 2
EPS = 1e-6
GLA_QK = GLA_HEADS * GLA_DK
GLA_V = GLA_HEADS * GLA_DV
RW_W = RW_HEADS * RW_DH
P_A = 2 * GLA_QK + 2 * GLA_V + GLA_GATE_RANK
ATT_Q = ATT_HEADS * ATT_DH
ATT_KV = ATT_KV_HEADS * ATT_DH

VMEM_LIMIT_BYTES = 56 * 1024 * 1024
LANES = 128
ROW_TILE = 512


def _mm_kernel(x_ref, w_ref, o_ref):
    o_ref[...] = jnp.dot(x_ref[...].astype(BF16), w_ref[...], preferred_element_type=F32)


def mm(x, w):
    m, k = x.shape
    n = w.shape[1]
    tm = min(ROW_TILE, m)
    assert m % tm == 0
    return pl.pallas_call(
        _mm_kernel,
        grid=(m // tm,),
        in_specs=[pl.BlockSpec((tm, k), lambda i: (i, 0)), pl.BlockSpec((k, n), lambda i: (0, 0))],
        out_specs=pl.BlockSpec((tm, n), lambda i: (i, 0)),
        out_shape=jax.ShapeDtypeStruct((m, n), F32),
        compiler_params=pltpu.CompilerParams(dimension_semantics=("arbitrary",),
                                             vmem_limit_bytes=VMEM_LIMIT_BYTES),
    )(x, w.astype(BF16))


def _mod_operand(mod, n_rows, tm):
    b, d = mod.shape
    per_b = n_rows // b
    if per_b % tm == 0:
        tiles_per_b = per_b // tm
        return mod[:, None, :], pl.BlockSpec((1, 1, d), lambda i, *_: (i // tiles_per_b, 0, 0))
    rows = jnp.repeat(mod, per_b, axis=0).reshape(n_rows // tm, tm, d)
    return rows, pl.BlockSpec((1, tm, d), lambda i, *_: (i, 0, 0))


def _modulated(x, shift, scale):
    xn = x * lax.rsqrt(jnp.mean(x * x, axis=-1, keepdims=True) + EPS)
    return xn * (1.0 + scale) + shift


def _mod_mm_kernel(x_ref, sh_ref, sc_ref, w_ref, o_ref):
    h = _modulated(x_ref[...], sh_ref[0], sc_ref[0])
    o_ref[...] = jnp.dot(h.astype(BF16), w_ref[...], preferred_element_type=F32)


def mod_mm(x, shift, scale, w):
    t, d = x.shape
    n = w.shape[1]
    tm = min(ROW_TILE, t)
    sh, mod_spec = _mod_operand(shift, t, tm)
    sc, _ = _mod_operand(scale, t, tm)
    return pl.pallas_call(
        _mod_mm_kernel,
        grid=(t // tm,),
        in_specs=[pl.BlockSpec((tm, d), lambda i: (i, 0)), mod_spec, mod_spec, pl.BlockSpec((d, n), lambda i: (0, 0))],
        out_specs=pl.BlockSpec((tm, n), lambda i: (i, 0)),
        out_shape=jax.ShapeDtypeStruct((t, n), F32),
        compiler_params=pltpu.CompilerParams(dimension_semantics=("arbitrary",),
                                             vmem_limit_bytes=VMEM_LIMIT_BYTES),
        name="modulated_in_proj",
    )(x, sh, sc, w.astype(BF16))


def _mm_res_kernel(a_ref, w_ref, x_ref, g_ref, o_ref):
    y = jnp.dot(a_ref[...].astype(BF16), w_ref[...], preferred_element_type=F32)
    o_ref[...] = x_ref[...] + g_ref[0] * y


def mm_res(a, w, x, gate):
    t, k = a.shape
    d = w.shape[1]
    tm = min(ROW_TILE, t)
    g, mod_spec = _mod_operand(gate, t, tm)
    return pl.pallas_call(
        _mm_res_kernel,
        grid=(t // tm,),
        in_specs=[pl.BlockSpec((tm, k), lambda i: (i, 0)), pl.BlockSpec((k, d), lambda i: (0, 0)),
                  pl.BlockSpec((tm, d), lambda i: (i, 0)), mod_spec],
        out_specs=pl.BlockSpec((tm, d), lambda i: (i, 0)),
        out_shape=jax.ShapeDtypeStruct((t, d), F32),
        compiler_params=pltpu.CompilerParams(dimension_semantics=("arbitrary",),
                                             vmem_limit_bytes=VMEM_LIMIT_BYTES),
        name="out_proj_residual",
    )(a, w.astype(BF16), x, g)


def _ffn_kernel(routed, x_ref, sh_ref, sc_ref, gt_ref, wr_ref, wg_ref, wu_ref, wd_ref, o_ref, h_scr, cmb_scr, acc_scr):
    e = pl.program_id(1)
    lane = lax.broadcasted_iota(jnp.int32, cmb_scr.shape, 1)

    @pl.when(e == 0)
    def _():
        hb = _modulated(x_ref[...], sh_ref[0], sc_ref[0]).astype(BF16)
        h_scr[...] = hb
        acc_scr[...] = jnp.zeros(acc_scr.shape, F32)
        if routed:
            logits = jnp.dot(hb, wr_ref[...], preferred_element_type=F32)
            lg = jnp.where(lane < N_EXPERTS, logits, -jnp.inf)
            m1 = jnp.max(lg, axis=1, keepdims=True)
            i1 = jnp.min(jnp.where(lg == m1, lane, LANES), axis=1, keepdims=True)
            lg2 = jnp.where(lane == i1, -jnp.inf, lg)
            m2 = jnp.max(lg2, axis=1, keepdims=True)
            i2 = jnp.min(jnp.where(lg2 == m2, lane, LANES), axis=1, keepdims=True)
            t = jnp.exp(m2 - m1)
            p1 = 1.0 / (1.0 + t)
            cmb_scr[...] = jnp.where(lane == i1, p1, 0.0) + jnp.where(lane == i2, t * p1, 0.0)

    hb = h_scr[...]
    g = jnp.dot(hb, wg_ref[0], preferred_element_type=F32)
    u = jnp.dot(hb, wu_ref[0], preferred_element_type=F32)
    act = g * (1.0 / (1.0 + jnp.exp(-g))) * u
    y = jnp.dot(act.astype(BF16), wd_ref[0], preferred_element_type=F32)
    if routed:
        y = y * jnp.sum(jnp.where(lane == e, cmb_scr[...], 0.0), axis=1, keepdims=True)
    acc_scr[...] += y

    @pl.when(e == pl.num_programs(1) - 1)
    def _():
        o_ref[...] = x_ref[...] + gt_ref[0] * acc_scr[...]


FFN_BLOCK = 1408


def ffn_block(x, shift, scale, gate, w_router, w_gate, w_up, w_down):
    t, d = x.shape
    routed = w_router is not None
    tm = min(ROW_TILE, t)
    sh, mod_spec = _mod_operand(shift, t, tm)
    sc, _ = _mod_operand(scale, t, tm)
    gt, _ = _mod_operand(gate, t, tm)
    if routed:
        n_e, _, f = w_gate.shape
        assert f == FFN_BLOCK and n_e == N_EXPERTS
        wr = jnp.pad(w_router, ((0, 0), (0, LANES - n_e))).astype(BF16)
        up_spec = pl.BlockSpec((1, d, f), lambda i, e: (e, 0, 0))
        down_spec = pl.BlockSpec((1, f, d), lambda i, e: (e, 0, 0))
    else:
        f_all = w_gate.shape[1]
        assert f_all % FFN_BLOCK == 0
        n_e, f = f_all // FFN_BLOCK, FFN_BLOCK
        wr = jnp.zeros((d, LANES), BF16)
        w_gate, w_up, w_down = w_gate[None], w_up[None], w_down[None]
        up_spec = pl.BlockSpec((1, d, f), lambda i, e: (0, 0, e))
        down_spec = pl.BlockSpec((1, f, d), lambda i, e: (0, e, 0))
    return pl.pallas_call(
        functools.partial(_ffn_kernel, routed),
        grid=(t // tm, n_e),
        in_specs=[pl.BlockSpec((tm, d), lambda i, e: (i, 0)), mod_spec, mod_spec, mod_spec,
                  pl.BlockSpec((d, LANES), lambda i, e: (0, 0)), up_spec, up_spec, down_spec],
        out_specs=pl.BlockSpec((tm, d), lambda i, e: (i, 0)),
        out_shape=jax.ShapeDtypeStruct((t, d), F32),
        scratch_shapes=[pltpu.VMEM((tm, d), BF16), pltpu.VMEM((tm, LANES), F32), pltpu.VMEM((tm, d), F32)],
        compiler_params=pltpu.CompilerParams(dimension_semantics=("arbitrary", "arbitrary"),
                                             vmem_limit_bytes=VMEM_LIMIT_BYTES),
        name="routed_experts" if routed else "dense_swiglu",
    )(x, sh, sc, gt, wr, w_gate.astype(BF16), w_up.astype(BF16), w_down.astype(BF16))


RW_CHUNK = 64


def _bdot(a, b):
    return jnp.dot(a.astype(BF16), b.astype(BF16), preferred_element_type=F32)


def _bdot_nt(a, b):
    return lax.dot_general(a.astype(BF16), b.astype(BF16), (((1,), (1,)), ((), ())), preferred_element_type=F32)


def _bdot_tn(a, b):
    return lax.dot_general(a.astype(BF16), b.astype(BF16), (((0,), (0,)), ((), ())), preferred_element_type=F32)


def _split3(x):
    hi = x.astype(BF16)
    r1 = x - hi.astype(F32)
    mid = r1.astype(BF16)
    lo = (r1 - mid.astype(F32)).astype(BF16)
    return hi, mid, lo


def _rwkv_chunk_heads(r, lw, k, v, a, b, h0, tri_bf, strict, incl, eye):
    nh = len(r)
    hs = range(nh)
    cum = []
    for i in hs:
        hi, mid, lo = _split3(lw[i])
        cum.append(jnp.dot(tri_bf, hi, preferred_element_type=F32) + jnp.dot(tri_bf, mid, preferred_element_type=F32)
                   + jnp.dot(tri_bf, lo, preferred_element_type=F32))
    cum_last = [cum[i][RW_CHUNK - 1:RW_CHUNK, :] for i in hs]
    e_neg = [jnp.exp(-cum[i]) for i in hs]
    e_end = [jnp.exp(cum_last[i] - cum[i]) for i in hs]
    a_t = [a[i] * jnp.exp(cum[i] - lw[i]) for i in hs]
    r_t = [r[i] * jnp.exp(cum[i]) for i in hs]
    k_t = [k[i] * e_neg[i] for i in hs]
    b_t = [b[i] * e_neg[i] for i in hs]
    k_e = [k[i] * e_end[i] for i in hs]
    b_e = [b[i] * e_end[i] for i in hs]
    a_ab = [jnp.where(strict, _bdot_nt(a_t[i], b_t[i]), 0.0) for i in hs]
    a_ak = [jnp.where(strict, _bdot_nt(a_t[i], k_t[i]), 0.0) for i in hs]
    a_rb = [jnp.where(incl, _bdot_nt(r_t[i], b_t[i]), 0.0) for i in hs]
    a_rk = [jnp.where(incl, _bdot_nt(r_t[i], k_t[i]), 0.0) for i in hs]
    x = [eye + a_ab[i] for i in hs]
    y = list(a_ab)
    for _ in range(int(math.log2(RW_CHUNK)) - 1):
        y = [_bdot(y[i], y[i]) for i in hs]
        x = [x[i] + _bdot(x[i], y[i]) for i in hs]
    a_p = [_bdot(x[i], a_t[i]) for i in hs]
    akv = [_bdot(a_ak[i], v[i]) for i in hs]
    v_p = [_bdot(x[i], akv[i]) for i in hs]
    r_p = [r_t[i] + _bdot(a_rb[i], a_p[i]) for i in hs]
    y_p = [_bdot(a_rk[i], v[i]) + _bdot(a_rb[i], v_p[i]) for i in hs]
    m_lr = [_bdot_tn(b_e[i], a_p[i]) for i in hs]
    g = [_bdot_tn(k_e[i], v[i]) + _bdot_tn(b_e[i], v_p[i]) for i in hs]
    w_col = [jnp.sum(jnp.where(eye, jnp.exp(cum_last[i]), 0.0), axis=1, keepdims=True) for i in hs]
    y_out = [_bdot(r_p[i], h0[i]) + y_p[i] for i in hs]
    h_new = [w_col[i] * h0[i] + _bdot(m_lr[i], h0[i]) + g[i] for i in hs]
    return y_out, h_new


RW_HEAD_GROUP = 16


def _rwkv_kernel(r_ref, lw_ref, k_ref, v_ref, a_ref, b_ref, h0_ref, y_ref, hout_ref, h_scr):
    c = pl.program_id(0)

    @pl.when(c == 0)
    def _():
        h_scr[...] = h0_ref[...]

    rows = lax.broadcasted_iota(jnp.int32, (RW_CHUNK, RW_CHUNK), 0)
    cols = lax.broadcasted_iota(jnp.int32, (RW_CHUNK, RW_CHUNK), 1)
    strict = rows > cols
    incl = rows >= cols
    eye = rows == cols
    tri_bf = jnp.where(incl, 1.0, 0.0).astype(BF16)
    n_bh = r_ref.shape[0]
    for i0 in range(0, n_bh, RW_HEAD_GROUP):
        ids = range(i0, min(i0 + RW_HEAD_GROUP, n_bh))
        y_out, h_new = _rwkv_chunk_heads(*[[ref[i] for i in ids] for ref in
                                           (r_ref, lw_ref, k_ref, v_ref, a_ref, b_ref, h_scr)],
                                         tri_bf, strict, incl, eye)
        for j, i in enumerate(ids):
            y_ref[i] = y_out[j]
            h_scr[i] = h_new[j]

    @pl.when(c == pl.num_programs(0) - 1)
    def _():
        hout_ref[...] = h_scr[...]


def rwkv7_chunked(r, lw, k, v, a, b, s0):
    bt, L, H, N = r.shape
    assert L % RW_CHUNK == 0
    hm = lambda t: t.transpose(0, 2, 1, 3).reshape(bt * H, L, N)
    h0 = s0.transpose(0, 1, 3, 2).reshape(bt * H, N, N)
    seq_spec = pl.BlockSpec((bt * H, RW_CHUNK, N), lambda c: (0, c, 0))
    st_spec = pl.BlockSpec((bt * H, N, N), lambda c: (0, 0, 0))
    y, h_fin = pl.pallas_call(
        _rwkv_kernel,
        grid=(L // RW_CHUNK,),
        in_specs=[seq_spec] * 6 + [st_spec],
        out_specs=[seq_spec, st_spec],
        out_shape=[jax.ShapeDtypeStruct((bt * H, L, N), F32), jax.ShapeDtypeStruct((bt * H, N, N), F32)],
        scratch_shapes=[pltpu.VMEM((bt * H, N, N), F32)],
        compiler_params=pltpu.CompilerParams(dimension_semantics=("arbitrary",),
                                             vmem_limit_bytes=VMEM_LIMIT_BYTES),
        name="rwkv7_chunked",
    )(hm(r), hm(lw), hm(k), hm(v), hm(a), hm(b), h0)
    y = y.reshape(bt, H, L, N).transpose(0, 2, 1, 3)
    return y, h_fin.reshape(bt, H, N, N).transpose(0, 1, 3, 2)


def _gla_kernel(q_ref, k_ref, v_ref, g_ref, s0_ref, o_ref, sout_ref, s_scr):
    c = pl.program_id(0)

    @pl.when(c == 0)
    def _():
        s_scr[...] = s0_ref[...]

    rows = lax.broadcasted_iota(jnp.int32, (GLA_CHUNK, GLA_CHUNK), 0)
    cols = lax.broadcasted_iota(jnp.int32, (GLA_CHUNK, GLA_CHUNK), 1)
    incl = rows >= cols
    eye = lax.broadcasted_iota(jnp.int32, (GLA_DK, GLA_DK), 0) == lax.broadcasted_iota(jnp.int32, (GLA_DK, GLA_DK), 1)
    tri_bf = jnp.where(incl, 1.0, 0.0).astype(BF16)
    hs = range(q_ref.shape[0])
    bcum = []
    for i in hs:
        hi, mid, lo = _split3(g_ref[i])
        bcum.append(jnp.dot(tri_bf, hi, preferred_element_type=F32) + jnp.dot(tri_bf, mid, preferred_element_type=F32)
                    + jnp.dot(tri_bf, lo, preferred_element_type=F32))
    blast = [bcum[i][GLA_CHUNK - 1:GLA_CHUNK, :] for i in hs]
    q_e = [q_ref[i] * jnp.exp(bcum[i]) for i in hs]
    k_e = [k_ref[i] * jnp.exp(-bcum[i]) for i in hs]
    k_end = [k_ref[i] * jnp.exp(blast[i] - bcum[i]) for i in hs]
    att = [jnp.where(incl, _bdot_nt(q_e[i], k_e[i]), 0.0) for i in hs]
    s_old = [s_scr[i] for i in hs]
    o = [_bdot(att[i], v_ref[i]) + _bdot(q_e[i], s_old[i]) for i in hs]
    kv = [_bdot_tn(k_end[i], v_ref[i]) for i in hs]
    dec = [jnp.sum(jnp.where(eye, jnp.exp(blast[i]), 0.0), axis=1, keepdims=True) for i in hs]
    for i in hs:
        o_ref[i] = o[i]
        s_scr[i] = dec[i] * s_old[i] + kv[i]

    @pl.when(c == pl.num_programs(0) - 1)
    def _():
        sout_ref[...] = s_scr[...]


def gla_chunked_pallas(q, k, v, log_a, s0):
    bt, L, H, dk = q.shape
    dv = v.shape[-1]
    assert L % GLA_CHUNK == 0
    hm = lambda t: t.transpose(0, 2, 1, 3).reshape(bt * H, L, t.shape[-1])
    qk_spec = pl.BlockSpec((bt * H, GLA_CHUNK, dk), lambda c: (0, c, 0))
    v_spec = pl.BlockSpec((bt * H, GLA_CHUNK, dv), lambda c: (0, c, 0))
    st_spec = pl.BlockSpec((bt * H, dk, dv), lambda c: (0, 0, 0))
    o, s_fin = pl.pallas_call(
        _gla_kernel,
        grid=(L // GLA_CHUNK,),
        in_specs=[qk_spec, qk_spec, v_spec, qk_spec, st_spec],
        out_specs=[v_spec, st_spec],
        out_shape=[jax.ShapeDtypeStruct((bt * H, L, dv), F32), jax.ShapeDtypeStruct((bt * H, dk, dv), F32)],
        scratch_shapes=[pltpu.VMEM((bt * H, dk, dv), F32)],
        compiler_params=pltpu.CompilerParams(dimension_semantics=("arbitrary",),
                                             vmem_limit_bytes=VMEM_LIMIT_BYTES),
        name="gla_chunked",
    )(hm(q), hm(k), hm(v), hm(log_a), s0.reshape(bt * H, dk, dv))
    return o.reshape(bt, H, L, dv).transpose(0, 2, 1, 3), s_fin.reshape(bt, H, dk, dv)


DSA_QB = 128
DSA_KC = 512
MASK_NEG = -1e30
INT_MIN = -2 ** 31
ONE = np.float32(1.0)
ZERO = np.float32(0.0)
G_PER_KV = ATT_HEADS // ATT_KV_HEADS


def _sortable_key(x):
    bits = lax.bitcast_convert_type(x + 0.0, jnp.int32)
    return bits ^ ((bits >> 31) & 0x7FFFFFFF)


def _dsa_kernel(topk, q_ref, qi_ref, wi_ref, k_ref, v_ref, kit_ref, upper_ref, o_ref,
                qbd_scr, qi_scr, wb_scr, key_scr, m_scr, acc_scr):
    i = pl.program_id(1)
    n_chunks = (i * DSA_QB) // DSA_KC + 1
    nt = DSA_KC // LANES
    q_pos = i * DSA_QB + lax.broadcasted_iota(jnp.int32, (DSA_QB, LANES), 0)
    lane = lax.broadcasted_iota(jnp.int32, (DSA_QB, LANES), 1)

    qbd_scr[...] = jnp.zeros(qbd_scr.shape, BF16)
    for n in range(ATT_KV_HEADS):
        for g in range(G_PER_KV):
            hd = n * G_PER_KV + g
            piece = q_ref[0, :, hd * ATT_DH:(hd + 1) * ATT_DH] * (ATT_DH ** -0.5)
            qbd_scr[hd * DSA_QB:(hd + 1) * DSA_QB, n * ATT_DH:(n + 1) * ATT_DH] = piece.astype(BF16)
    for h in range(IDX_HEADS):
        qi_scr[h * DSA_QB:(h + 1) * DSA_QB, :] = qi_ref[0, :, h * IDX_DH:(h + 1) * IDX_DH].astype(BF16)
        wb_scr[h] = jnp.broadcast_to(wi_ref[0, :, h:h + 1], (DSA_QB, LANES))

    def score_chunk(c, carry):
        dots = jnp.dot(qi_scr[...], kit_ref[0, c], preferred_element_type=F32)
        for jt in range(nt):
            sc = jnp.zeros((DSA_QB, LANES), F32)
            for h in range(IDX_HEADS):
                d = dots[h * DSA_QB:(h + 1) * DSA_QB, jt * LANES:(jt + 1) * LANES]
                sc = sc + jnp.maximum(d, 0.0) * wb_scr[h]
            k_pos = c * DSA_KC + jt * LANES + lane
            sc = jnp.where(k_pos <= q_pos, sc, -jnp.inf)
            key_scr[c, :, jt * LANES:(jt + 1) * LANES] = _sortable_key(sc)
        return carry

    lax.fori_loop(0, n_chunks, score_chunk, 0)

    def count_ge(cand):
        cand_b = jnp.broadcast_to(cand, (DSA_QB, LANES))

        def body(c, acc):
            for jt in range(nt):
                acc = acc + jnp.where(key_scr[c, :, jt * LANES:(jt + 1) * LANES] >= cand_b, ONE, ZERO)
            return acc

        acc = lax.fori_loop(0, n_chunks, body, jnp.zeros((DSA_QB, LANES), F32))
        return jnp.sum(acc, axis=1, keepdims=True)

    kf = float(topk)
    cur = jnp.where(count_ge(jnp.zeros((DSA_QB, 1), jnp.int32)) >= kf, 0, INT_MIN).astype(jnp.int32)

    def bit_step(p, cur):
        cand = cur | (jnp.int32(1) << (30 - p))
        return jnp.where(count_ge(cand) >= kf, cand, cur)

    thr = lax.fori_loop(0, 31, bit_step, cur)
    thr_b = jnp.broadcast_to(thr, (DSA_QB, LANES))

    def count_gt_body(c, acc):
        for jt in range(nt):
            acc = acc + jnp.where(key_scr[c, :, jt * LANES:(jt + 1) * LANES] > thr_b, ONE, ZERO)
        return acc

    n_gt = jnp.sum(lax.fori_loop(0, n_chunks, count_gt_body, jnp.zeros((DSA_QB, LANES), F32)), axis=1, keepdims=True)
    need = kf - n_gt

    m_scr[...] = jnp.full(m_scr.shape, MASK_NEG, F32)
    acc_scr[...] = jnp.zeros(acc_scr.shape, F32)

    def attend_chunk(c, tie_carry):
        k0 = pl.multiple_of(c * DSA_KC, DSA_KC)
        keys = key_scr[c]
        thr_c = jnp.broadcast_to(thr, (DSA_QB, DSA_KC))
        eq = keys == thr_c
        eq_f = jnp.where(eq, ONE, ZERO)
        rank = tie_carry + jnp.dot(eq_f.astype(BF16), upper_ref[...], preferred_element_type=F32)
        k_pos = k0 + lax.broadcasted_iota(jnp.int32, (DSA_QB, DSA_KC), 1)
        qp = i * DSA_QB + lax.broadcasted_iota(jnp.int32, (DSA_QB, DSA_KC), 0)
        sel = jnp.where(keys > thr_c, ONE, jnp.where(rank < need, eq_f, ZERO))
        bias = jnp.where((sel > 0.5) & (k_pos <= qp), ZERO, np.float32(MASK_NEG))
        k_c = k_ref[0, pl.ds(k0, DSA_KC), :]
        v_c = v_ref[0, pl.ds(k0, DSA_KC), :]
        rows = G_PER_KV * DSA_QB
        for n in range(ATT_KV_HEADS):
            s = lax.dot_general(qbd_scr[n * rows:(n + 1) * rows, :], k_c, (((1,), (1,)), ((), ())),
                                preferred_element_type=F32)
            s = (s.reshape(G_PER_KV, DSA_QB, DSA_KC) + bias[None]).reshape(rows, DSA_KC)
            tiles = [s[:, jt * LANES:(jt + 1) * LANES] for jt in range(nt)]
            tile_max = functools.reduce(jnp.maximum, tiles)
            m_old = m_scr[n]
            m_new = jnp.maximum(m_old, jnp.max(tile_max, axis=1, keepdims=True))
            alpha = jnp.exp(m_old - m_new)
            p = jnp.concatenate([jnp.exp(t - m_new) for t in tiles], axis=1)
            pv = jnp.dot(p.astype(BF16), v_c[:, n * LANES:(n + 1) * LANES], preferred_element_type=F32)
            acc_scr[n] = alpha * acc_scr[n] + pv
            m_scr[n] = m_new
        return tie_carry + jnp.sum(eq_f, axis=1, keepdims=True)

    lax.fori_loop(0, n_chunks, attend_chunk, jnp.zeros((DSA_QB, 1), F32))

    for n in range(ATT_KV_HEADS):
        acc = acc_scr[n]
        o_n = acc[:, :ATT_DH] / acc[:, ATT_DH:]
        for g in range(G_PER_KV):
            hd = n * G_PER_KV + g
            o_ref[0, :, hd * ATT_DH:(hd + 1) * ATT_DH] = o_n[g * DSA_QB:(g + 1) * DSA_QB, :]


def dsa_prompt_attention(q, k, v, qi, ki, wi):
    bt, L = q.shape[:2]
    assert L % DSA_KC == 0 and L // 4 >= 1
    topk = min(TOPK_MAX, L // 4)
    nb = L // DSA_QB
    nc = L // DSA_KC
    kit = ki.astype(BF16).reshape(bt, nc, DSA_KC, IDX_DH).transpose(0, 1, 3, 2)
    upper = jnp.triu(jnp.ones((DSA_KC, DSA_KC), BF16), 1)
    rows = G_PER_KV * DSA_QB
    v_aug = jnp.concatenate([v.astype(BF16), jnp.ones(v.shape, BF16)], axis=-1).reshape(bt, L, ATT_KV_HEADS * LANES)
    return pl.pallas_call(
        functools.partial(_dsa_kernel, topk),
        grid=(bt, nb),
        in_specs=[
            pl.BlockSpec((1, DSA_QB, ATT_Q), lambda b, i: (b, i, 0)),
            pl.BlockSpec((1, DSA_QB, IDX_HEADS * IDX_DH), lambda b, i: (b, i, 0)),
            pl.BlockSpec((1, DSA_QB, IDX_HEADS), lambda b, i: (b, i, 0)),
            pl.BlockSpec((1, L, ATT_KV), lambda b, i: (b, 0, 0)),
            pl.BlockSpec((1, L, ATT_KV_HEADS * LANES), lambda b, i: (b, 0, 0)),
            pl.BlockSpec((1, nc, IDX_DH, DSA_KC), lambda b, i: (b, 0, 0, 0)),
            pl.BlockSpec((DSA_KC, DSA_KC), lambda b, i: (0, 0)),
        ],
        out_specs=pl.BlockSpec((1, DSA_QB, ATT_Q), lambda b, i: (b, i, 0)),
        out_shape=jax.ShapeDtypeStruct((bt, L, ATT_Q), F32),
        scratch_shapes=[
            pltpu.VMEM((ATT_HEADS * DSA_QB, ATT_KV), BF16),
            pltpu.VMEM((IDX_HEADS * DSA_QB, IDX_DH), BF16),
            pltpu.VMEM((IDX_HEADS, DSA_QB, LANES), F32),
            pltpu.VMEM((nc, DSA_QB, DSA_KC), jnp.int32),
            pltpu.VMEM((ATT_KV_HEADS, rows, LANES), F32),
            pltpu.VMEM((ATT_KV_HEADS, rows, LANES), F32),
        ],
        compiler_params=pltpu.CompilerParams(dimension_semantics=("arbitrary", "arbitrary"),
                                             vmem_limit_bytes=VMEM_LIMIT_BYTES),
        name="dsa_prompt_attention",
    )(q.reshape(bt, L, ATT_Q), qi.reshape(bt, L, IDX_HEADS * IDX_DH), wi,
      k.reshape(bt, L, ATT_KV).astype(BF16), v_aug, kit, upper)


NEW_PAD = 16
PAGES_PER_STEP = 4


def _dsa_sample_kernel(topk, n_new, pt_ref, qbd_ref, qi_ref, wrep_ref, knew_ref, vnew_ref, kinew_ref, *rest):
    pages = rest[:3 * PAGES_PER_STEP]
    upper_ref, o_ref, kbuf, vbuf, kibuf = rest[3 * PAGES_PER_STEP:]
    p = pl.program_id(1)
    n_steps = pl.num_programs(1)
    past = n_steps * PAGES_PER_STEP * PAGE_SIZE
    n_keys = kbuf.shape[0]
    nq = qbd_ref.shape[1] // ATT_HEADS
    for buf, refs in zip((kbuf, vbuf, kibuf), (pages[:PAGES_PER_STEP], pages[PAGES_PER_STEP:2 * PAGES_PER_STEP],
                                               pages[2 * PAGES_PER_STEP:])):
        for s, ref in enumerate(refs):
            row0 = pl.multiple_of((p * PAGES_PER_STEP + s) * PAGE_SIZE, PAGE_SIZE)
            buf[pl.ds(row0, PAGE_SIZE), :] = ref[0, 0].astype(BF16)

    @pl.when(p == n_steps - 1)
    def _():
        tail = n_keys - past
        kbuf[past:, :] = jnp.zeros((tail, ATT_KV), BF16)
        vbuf[past:, :] = jnp.zeros((tail, ATT_KV), BF16)
        kibuf[past:, :] = jnp.zeros((tail, IDX_DH), BF16)
        kbuf[past:past + NEW_PAD, :] = knew_ref[0].astype(BF16)
        vbuf[past:past + NEW_PAD, :] = vnew_ref[0].astype(BF16)
        kibuf[past:past + NEW_PAD, :] = kinew_ref[0].astype(BF16)

        dots = lax.dot_general(qi_ref[0], kibuf[...], (((1,), (1,)), ((), ())), preferred_element_type=F32)
        k_idx = lax.broadcasted_iota(jnp.int32, (8, n_keys), 1)
        q_row = lax.broadcasted_iota(jnp.int32, (8, n_keys), 0)
        valid = (k_idx - past <= q_row) & (q_row < nq)
        rows = []
        for qn in range(nq):
            d = jnp.maximum(dots[qn * IDX_HEADS:(qn + 1) * IDX_HEADS], 0.0)
            w = wrep_ref[0, qn * IDX_HEADS:(qn + 1) * IDX_HEADS, :]
            parts = []
            for jt in range(n_keys // LANES):
                parts.append(jnp.sum(d[:, jt * LANES:(jt + 1) * LANES] * w, axis=0, keepdims=True))
            rows.append(jnp.concatenate(parts, axis=1))
        rows.append(jnp.zeros((8 - nq, n_keys), F32))
        sc = jnp.where(valid, jnp.concatenate(rows, axis=0), -jnp.inf)
        keys = _sortable_key(sc)

        kf = float(topk)

        def count_ge(cand):
            return jnp.sum(jnp.where(keys >= cand, ONE, ZERO), axis=1, keepdims=True)

        cur = jnp.where(count_ge(jnp.zeros((8, 1), jnp.int32)) >= kf, 0, INT_MIN).astype(jnp.int32)

        def bit_step(i, cur):
            cand = cur | (jnp.int32(1) << (30 - i))
            return jnp.where(count_ge(cand) >= kf, cand, cur)

        thr = lax.fori_loop(0, 31, bit_step, cur)
        n_gt = jnp.sum(jnp.where(keys > thr, ONE, ZERO), axis=1, keepdims=True)
        need = kf - n_gt
        eq_f = jnp.where(keys == thr, ONE, ZERO)
        carry = jnp.zeros((8, 1), F32)
        ranks = []
        for jt in range(n_keys // LANES):
            e = eq_f[:, jt * LANES:(jt + 1) * LANES]
            ranks.append(carry + jnp.dot(e.astype(BF16), upper_ref[...], preferred_element_type=F32))
            carry = carry + jnp.sum(e, axis=1, keepdims=True)
        rank = jnp.concatenate(ranks, axis=1)
        sel = jnp.where(keys > thr, ONE, jnp.where(rank < need, eq_f, ZERO))
        bias = jnp.where((sel > 0.5) & valid, ZERO, np.float32(MASK_NEG))

        s = lax.dot_general(qbd_ref[0], kbuf[...], (((1,), (1,)), ((), ())), preferred_element_type=F32)
        s = jnp.concatenate([s[qn * ATT_HEADS:(qn + 1) * ATT_HEADS] + bias[qn:qn + 1] for qn in range(nq)], axis=0)
        m = jnp.max(s, axis=1, keepdims=True)
        pr = jnp.exp(s - m)
        l = jnp.sum(pr, axis=1, keepdims=True)
        o_ref[0] = jnp.dot(pr.astype(BF16), vbuf[...], preferred_element_type=F32) / l


def dsa_sample_attention(q, k_new, v_new, qi, ki_new, wi, cache_k, cache_v, cache_kidx, page_table, j):
    bt, nq = q.shape[:2]
    n_pages = page_table.shape[1]
    past = n_pages * PAGE_SIZE
    assert nq <= 8 and nq <= NEW_PAD
    topk = min(TOPK_MAX, (past + nq) // 4)
    n_keys = past + LANES
    n_pool = cache_k.shape[1]
    qs = (q * ATT_DH ** -0.5).astype(BF16).reshape(bt, nq, ATT_KV_HEADS, G_PER_KV, 1, ATT_DH)
    eye = jnp.eye(ATT_KV_HEADS, dtype=BF16)[None, None, :, None, :, None]
    qbd = (qs * eye).reshape(bt, nq * ATT_HEADS, ATT_KV)
    qi2 = qi.astype(BF16).reshape(bt, nq * IDX_HEADS, IDX_DH)
    wrep = jnp.broadcast_to(wi.reshape(bt, nq * IDX_HEADS, 1), (bt, nq * IDX_HEADS, LANES))
    pad = lambda t: jnp.pad(t.reshape(bt, nq, -1), ((0, 0), (0, NEW_PAD - nq), (0, 0)))
    upper = jnp.triu(jnp.ones((LANES, LANES), BF16), 1)
    per_b = lambda shape: pl.BlockSpec((1,) + shape, lambda b, p, pt: (b, 0, 0))
    assert n_pages % PAGES_PER_STEP == 0

    def pages(width):
        return [pl.BlockSpec((1, 1, PAGE_SIZE, width), lambda b, p, pt, s=s: (j, pt[b, p * PAGES_PER_STEP + s], 0, 0))
                for s in range(PAGES_PER_STEP)]

    o_all = pl.pallas_call(
        functools.partial(_dsa_sample_kernel, topk, nq),
        grid_spec=pltpu.PrefetchScalarGridSpec(
            num_scalar_prefetch=1,
            grid=(bt, n_pages // PAGES_PER_STEP),
            in_specs=[per_b((nq * ATT_HEADS, ATT_KV)), per_b((nq * IDX_HEADS, IDX_DH)), per_b((nq * IDX_HEADS, LANES)),
                      per_b((NEW_PAD, ATT_KV)), per_b((NEW_PAD, ATT_KV)), per_b((NEW_PAD, IDX_DH))]
            + pages(ATT_KV) + pages(ATT_KV) + pages(IDX_DH)
            + [pl.BlockSpec((LANES, LANES), lambda b, p, pt: (0, 0))],
            out_specs=per_b((nq * ATT_HEADS, ATT_KV)),
            scratch_shapes=[pltpu.VMEM((n_keys, ATT_KV), BF16), pltpu.VMEM((n_keys, ATT_KV), BF16),
                            pltpu.VMEM((n_keys, IDX_DH), BF16)]),
        out_shape=jax.ShapeDtypeStruct((bt, nq * ATT_HEADS, ATT_KV), F32),
        compiler_params=pltpu.CompilerParams(dimension_semantics=("arbitrary", "arbitrary"),
                                             vmem_limit_bytes=VMEM_LIMIT_BYTES),
        name="dsa_sample_attention",
    )(page_table, qbd, qi2, wrep, pad(k_new), pad(v_new), pad(ki_new),
      *([cache_k.reshape(cache_k.shape[0], n_pool, PAGE_SIZE, ATT_KV)] * PAGES_PER_STEP),
      *([cache_v.reshape(cache_v.shape[0], n_pool, PAGE_SIZE, ATT_KV)] * PAGES_PER_STEP),
      *([cache_kidx] * PAGES_PER_STEP), upper)
    o6 = o_all.reshape(bt, nq, ATT_KV_HEADS, G_PER_KV, ATT_KV_HEADS, ATT_DH)
    o = jnp.stack([o6[:, :, n, :, n, :] for n in range(ATT_KV_HEADS)], axis=2)
    return o.reshape(bt, nq, ATT_Q)


def rms_norm(x, eps=EPS):
    return x * lax.rsqrt(jnp.mean(x * x, axis=-1, keepdims=True) + eps)


def layer_norm(x, g, b, eps):
    mu = jnp.mean(x, axis=-1, keepdims=True)
    var = jnp.mean(jnp.square(x - mu), axis=-1, keepdims=True)
    return (x - mu) * lax.rsqrt(var + eps) * g + b


def rotary(x, pos):
    half = x.shape[-1] // 2
    inv_freq = jnp.power(ROPE_THETA, -jnp.arange(half, dtype=F32) / half)
    ang = pos.astype(F32)[:, None] * inv_freq[None, :]
    cos = jnp.cos(ang)[None, :, None, :]
    sin = jnp.sin(ang)[None, :, None, :]
    x1, x2 = x[..., :half], x[..., half:]
    return jnp.concatenate([x1 * cos - x2 * sin, x2 * cos + x1 * sin], axis=-1)


def gla_chunked(q, k, v, log_a, s0):
    bt, L, H, _ = q.shape
    dv = v.shape[-1]
    C = min(GLA_CHUNK, L)
    n = -(-L // C)
    pad = n * C - L

    def prep(t):
        t = jnp.pad(t, ((0, 0), (0, pad), (0, 0), (0, 0)))
        return t.reshape(bt, n, C, H, t.shape[-1]).transpose(1, 0, 3, 2, 4)

    qc, kc, vc, gc = prep(q), prep(k), prep(v), prep(log_a)
    bcum = jnp.cumsum(gc, axis=3)
    blast = bcum[:, :, :, -1:, :]
    q_e = qc * jnp.exp(bcum)
    k_e = kc * jnp.exp(-bcum)
    k_end = kc * jnp.exp(blast - bcum)
    causal = jnp.tril(jnp.ones((C, C), dtype=bool))
    att = jnp.where(causal, jnp.einsum('nbhtd,nbhsd->nbhts', q_e, k_e), 0.0)
    o_intra = jnp.einsum('nbhts,nbhsv->nbhtv', att, vc)
    decay = jnp.exp(blast[:, :, :, 0, :])
    kv = jnp.einsum('nbhsd,nbhsv->nbhdv', k_end, vc)

    def step(s, inp):
        dec, kv_i = inp
        return dec[..., None] * s + kv_i, s

    s_fin, s_start = lax.scan(step, s0, (decay, kv))
    o = o_intra + jnp.einsum('nbhtd,nbhdv->nbhtv', q_e, s_start)
    o = o.transpose(1, 0, 3, 2, 4).reshape(bt, n * C, H, dv)[:, :L]
    return o, s_fin


def rwkv7_scan(r, w, k, v, a, b, s0):
    xs = tuple(t.transpose(1, 0, 2, 3) for t in (r, w, k, v, a, b))

    def step(s, inp):
        r_t, w_t, k_t, v_t, a_t, b_t = inp
        sa = jnp.einsum('bhvk,bhk->bhv', s, a_t)
        s = s * w_t[:, :, None, :] + sa[..., None] * b_t[:, :, None, :] + v_t[..., None] * k_t[:, :, None, :]
        return s, jnp.einsum('bhvk,bhk->bhv', s, r_t)

    s_fin, y = lax.scan(step, s0, xs)
    return y.transpose(1, 0, 2, 3), s_fin


def even_mixer(proj, shift_prev, s_gla, s_rw, j, P):
    bt, L, _ = proj.shape
    pa, pb_raw = proj[..., :P_A], proj[..., P_A:]
    q_a, k_a, v_a, r_a, g_low = jnp.split(pa, [GLA_QK, 2 * GLA_QK, 2 * GLA_QK + GLA_V, 2 * GLA_QK + 2 * GLA_V], axis=-1)
    ga = lambda t, d: t.reshape(bt, L, GLA_HEADS, d)
    log_alpha = jax.nn.log_sigmoid(g_low @ P['gla_w_gate'][j] + P['gla_b_gate'][j]) / GLA_TAU
    gla = gla_chunked_pallas if L % GLA_CHUNK == 0 else gla_chunked
    o_a, s_gla_new = gla(ga(q_a, GLA_DK) * GLA_DK ** -0.5, ga(k_a, GLA_DK), ga(v_a, GLA_DV),
                         ga(log_alpha, GLA_DK), s_gla)
    o_a = (rms_norm(o_a) * P['gla_norm_g'][j]).reshape(bt, L, GLA_V) * jax.nn.silu(r_a)
    prev = jnp.concatenate([shift_prev[:, None, :], pb_raw[:, :-1]], axis=1)
    pb = pb_raw + (prev - pb_raw) * P['rw_mu'][j]
    o1 = 3 * RW_W + RW_W_LORA
    r_b, k_b, v_b, w_low, a_low, gt_low = jnp.split(pb, [RW_W, 2 * RW_W, 3 * RW_W, o1, o1 + RW_A_LORA], axis=-1)
    w_log = -jax.nn.softplus(-(P['rw_w0'][j] + jnp.tanh(w_low) @ P['rw_w_decay'][j])) - 0.5
    log_decay = -jnp.exp(w_log)
    iclr = jax.nn.sigmoid(P['rw_a0'][j] + a_low @ P['rw_w_iclr'][j])
    gate = jax.nn.sigmoid(gt_low) @ P['rw_w_gate'][j]
    hb = lambda t: t.reshape(bt, L, RW_HEADS, RW_DH)
    kk = hb(k_b * P['rw_k_k'][j])
    kk = kk / jnp.maximum(jnp.sqrt(jnp.sum(kk * kk, axis=-1, keepdims=True)), 1e-12)
    k_b = k_b * (1 + (iclr - 1) * P['rw_k_a'][j])
    r_h, k_h, v_h, a_h = hb(r_b), hb(k_b), hb(v_b), hb(iclr)
    if L % RW_CHUNK == 0:
        y_b, s_rw_new = rwkv7_chunked(r_h, hb(log_decay), k_h, v_h, -kk, kk * a_h, s_rw)
    else:
        y_b, s_rw_new = rwkv7_scan(r_h, hb(jnp.exp(log_decay)), k_h, v_h, -kk, kk * a_h, s_rw)
    y_b = layer_norm(y_b, P['rw_ln_g'][j].reshape(RW_HEADS, RW_DH), P['rw_ln_b'][j].reshape(RW_HEADS, RW_DH), RW_LN_EPS)
    y_b = y_b + jnp.sum(r_h * k_h * P['rw_r_k'][j], axis=-1, keepdims=True) * v_h
    o_b = y_b.reshape(bt, L, RW_W) * gate
    return jnp.concatenate([o_a, o_b], axis=-1), pb_raw[:, -1], s_gla_new, s_rw_new


def odd_qkv(proj, pos, j, P):
    bt, L, _ = proj.shape
    o2 = ATT_Q + ATT_KV
    o3 = o2 + ATT_KV
    o4 = o3 + IDX_HEADS * IDX_DH
    q, k, v, qi, ki, wi = jnp.split(proj, [ATT_Q, o2, o3, o4, o4 + IDX_DH], axis=-1)
    q = rotary(rms_norm(q.reshape(bt, L, ATT_HEADS, ATT_DH)) * P['q_norm_g'][j], pos)
    k = rotary(rms_norm(k.reshape(bt, L, ATT_KV_HEADS, ATT_DH)) * P['k_norm_g'][j], pos)
    v = v.reshape(bt, L, ATT_KV_HEADS, ATT_DH)
    qi = rotary(qi.reshape(bt, L, IDX_HEADS, IDX_DH), pos)
    ki = rotary(layer_norm(ki, P['kidx_ln_g'][j], P['kidx_ln_b'][j], EPS)[:, :, None, :], pos)[:, :, 0, :]
    wi = wi * (IDX_HEADS * IDX_DH) ** -0.5
    return q, k, v, qi, ki, wi


def trunk(x, mods, pos, shift0, gla0, rw0, attend, P):
    bt, L, D = x.shape
    x2 = x.reshape(bt * L, D)
    glas, rws, shifts, ks, vs, kis = [], [], [], [], [], []
    for layer, mod in enumerate(mods):
        sh1, sc1, g1, sh2, sc2, g2 = jnp.split(mod, 6, axis=-1)
        j = layer // 2
        if layer % 2 == 0:
            proj = mod_mm(x2, sh1, sc1, P['w_in_even'][j]).reshape(bt, L, -1)
            mix, shf, sg, sr = even_mixer(proj, shift0[j], gla0[j], rw0[j], j, P)
            glas.append(sg)
            rws.append(sr)
            shifts.append(shf)
            x2 = mm_res(mix.reshape(bt * L, -1), P['w_out_even'][j], x2, g1)
            x2 = ffn_block(x2, sh2, sc2, g2, None, P['ffn_w_gate'][j], P['ffn_w_up'][j], P['ffn_w_down'][j])
        else:
            proj = mod_mm(x2, sh1, sc1, P['w_in_odd'][j]).reshape(bt, L, -1)
            q, k, v, qi, ki, wi = odd_qkv(proj, pos, j, P)
            ks.append(k)
            vs.append(v)
            kis.append(ki)
            x2 = mm_res(attend(j, q, k, v, qi, ki, wi).reshape(bt * L, -1), P['w_out_odd'][j], x2, g1)
            x2 = ffn_block(x2, sh2, sc2, g2, P['moe_router'][j], P['moe_w_gate'][j], P['moe_w_up'][j],
                           P['moe_w_down'][j])
    return (x2.reshape(bt, L, D), jnp.stack(glas), jnp.stack(rws), jnp.stack(shifts), jnp.stack(ks), jnp.stack(vs),
            jnp.stack(kis))


def kernel(x_prompt, x_sample, state_gla, state_rwkv, state_shift, cache_k, cache_v, cache_kidx, page_table, c_prompt, c_sample, w_ada, b_ada, w_in_even, gla_w_gate, gla_b_gate, gla_norm_g, rw_mu, rw_w0, rw_w_decay, rw_a0, rw_w_iclr, rw_w_gate, rw_k_k, rw_k_a, rw_r_k, rw_ln_g, rw_ln_b, w_out_even, w_in_odd, q_norm_g, k_norm_g, kidx_ln_g, kidx_ln_b, w_out_odd, ffn_w_gate, ffn_w_up, ffn_w_down, moe_router, moe_w_gate, moe_w_up, moe_w_down):
    P = dict(w_ada=w_ada, b_ada=b_ada, w_in_even=w_in_even, gla_w_gate=gla_w_gate, gla_b_gate=gla_b_gate,
             gla_norm_g=gla_norm_g, rw_mu=rw_mu, rw_w0=rw_w0, rw_w_decay=rw_w_decay, rw_a0=rw_a0,
             rw_w_iclr=rw_w_iclr, rw_w_gate=rw_w_gate, rw_k_k=rw_k_k, rw_k_a=rw_k_a, rw_r_k=rw_r_k,
             rw_ln_g=rw_ln_g, rw_ln_b=rw_ln_b, w_out_even=w_out_even, w_in_odd=w_in_odd,
             q_norm_g=q_norm_g, k_norm_g=k_norm_g, kidx_ln_g=kidx_ln_g, kidx_ln_b=kidx_ln_b,
             w_out_odd=w_out_odd, ffn_w_gate=ffn_w_gate, ffn_w_up=ffn_w_up, ffn_w_down=ffn_w_down,
             moe_router=moe_router, moe_w_gate=moe_w_gate, moe_w_up=moe_w_up, moe_w_down=moe_w_down)
    n_even = state_gla.shape[0]
    b_p, seq = x_prompt.shape[0], x_prompt.shape[1]
    zero_gla = jnp.zeros((n_even, b_p) + state_gla.shape[2:], state_gla.dtype)
    zero_rw = jnp.zeros((n_even, b_p) + state_rwkv.shape[2:], state_rwkv.dtype)
    zero_shift = jnp.zeros((n_even, b_p) + state_shift.shape[2:], state_shift.dtype)
    pos_p = jnp.arange(seq, dtype=jnp.int32)
    past = page_table.shape[1] * PAGE_SIZE
    pos_s = past + jnp.arange(x_sample.shape[1], dtype=jnp.int32)

    def prompt_attend(j, q, k, v, qi, ki, wi):
        return dsa_prompt_attention(q, k, v, qi, ki, wi)

    def sample_attend(j, q, k, v, qi, ki, wi):
        return dsa_sample_attention(q, k, v, qi, ki, wi, cache_k, cache_v, cache_kidx, page_table, j)

    silu_c = jax.nn.silu(jnp.concatenate([c_prompt, c_sample], axis=0))
    mods = [mm(silu_c, w_ada[layer]) + b_ada[layer] for layer in range(w_ada.shape[0])]
    mods_p = [m[:b_p] for m in mods]
    mods_s = [m[b_p:] for m in mods]

    y_prompt, p_gla, p_rw, p_shift, p_k, p_v, p_kidx = trunk(
        x_prompt, mods_p, pos_p, zero_shift, zero_gla, zero_rw, prompt_attend, P)
    y_sample, s_gla, s_rw, s_shift, s_k, s_v, s_kidx = trunk(
        x_sample, mods_s, pos_s, state_shift, state_gla, state_rwkv, sample_attend, P)
    return (y_prompt, y_sample, p_gla, p_rw, p_shift, p_k, p_v, p_kidx, s_gla, s_rw, s_shift, s_k, s_v, s_kidx)
```

```python
import functools
import math

import jax
import jax.numpy as jnp
import numpy as np
from jax import lax
from jax.experimental import pallas as pl
from jax.experimental.pallas import tpu as pltpu

F32 = jnp.float32
BF16 = jnp.bfloat16

PAGE_SIZE = 128
GLA_HEADS, GLA_DK, GLA_DV, GLA_GATE_RANK, GLA_TAU, GLA_CHUNK = 4, 64, 128, 16, 16.0, 64
RW_HEADS, RW_DH, RW_W_LORA, RW_A_LORA, RW_G_LORA, RW_LN_EPS = 8, 64, 32, 32, 96, 64e-5
ATT_HEADS, ATT_KV_HEADS, ATT_DH = 16, 4, 64
IDX_HEADS, IDX_DH = 8, 64
TOPK_MAX, Q_BLOCK, ROPE_THETA = 256, 128, 10000.0
N_EXPERTS, TOP_K_EXPERTS = 8, 2
EPS = 1e-6
GLA_QK = GLA_HEADS * GLA_DK
GLA_V = GLA_HEADS * GLA_DV
RW_W = RW_HEADS * RW_DH
P_A = 2 * GLA_QK + 2 * GLA_V + GLA_GATE_RANK
ATT_Q = ATT_HEADS * ATT_DH
ATT_KV = ATT_KV_HEADS * ATT_DH

VMEM_LIMIT_BYTES = 56 * 1024 * 1024
LANES = 128
ROW_TILE = 512


def _mm_kernel(x_ref, w_ref, o_ref):
    o_ref[...] = jnp.dot(x_ref[...].astype(BF16), w_ref[...], preferred_element_type=F32)


def mm(x, w):
    m, k = x.shape
    n = w.shape[1]
    tm = min(ROW_TILE, m)
    assert m % tm == 0
    return pl.pallas_call(
        _mm_kernel,
        grid=(m // tm,),
        in_specs=[pl.BlockSpec((tm, k), lambda i: (i, 0)), pl.BlockSpec((k, n), lambda i: (0, 0))],
        out_specs=pl.BlockSpec((tm, n), lambda i: (i, 0)),
        out_shape=jax.ShapeDtypeStruct((m, n), F32),
        compiler_params=pltpu.CompilerParams(dimension_semantics=("arbitrary",),
                                             vmem_limit_bytes=VMEM_LIMIT_BYTES),
    )(x, w.astype(BF16))


def _mod_operand(mod, n_rows, tm):
    b, d = mod.shape
    per_b = n_rows // b
    if per_b % tm == 0:
        tiles_per_b = per_b // tm
        return mod[:, None, :], pl.BlockSpec((1, 1, d), lambda i, *_: (i // tiles_per_b, 0, 0))
    rows = jnp.repeat(mod, per_b, axis=0).reshape(n_rows // tm, tm, d)
    return rows, pl.BlockSpec((1, tm, d), lambda i, *_: (i, 0, 0))


def _modulated(x, shift, scale):
    xn = x * lax.rsqrt(jnp.mean(x * x, axis=-1, keepdims=True) + EPS)
    return xn * (1.0 + scale) + shift


def _mod_mm_kernel(x_ref, sh_ref, sc_ref, w_ref, o_ref):
    h = _modulated(x_ref[...], sh_ref[0], sc_ref[0])
    o_ref[...] = jnp.dot(h.astype(BF16), w_ref[...], preferred_element_type=F32)


def mod_mm(x, shift, scale, w):
    t, d = x.shape
    n = w.shape[1]
    tm = min(ROW_TILE, t)
    sh, mod_spec = _mod_operand(shift, t, tm)
    sc, _ = _mod_operand(scale, t, tm)
    return pl.pallas_call(
        _mod_mm_kernel,
        grid=(t // tm,),
        in_specs=[pl.BlockSpec((tm, d), lambda i: (i, 0)), mod_spec, mod_spec, pl.BlockSpec((d, n), lambda i: (0, 0))],
        out_specs=pl.BlockSpec((tm, n), lambda i: (i, 0)),
        out_shape=jax.ShapeDtypeStruct((t, n), F32),
        compiler_params=pltpu.CompilerParams(dimension_semantics=("arbitrary",),
                                             vmem_limit_bytes=VMEM_LIMIT_BYTES),
        name="modulated_in_proj",
    )(x, sh, sc, w.astype(BF16))


def _mm_res_kernel(a_ref, w_ref, x_ref, g_ref, o_ref):
    y = jnp.dot(a_ref[...].astype(BF16), w_ref[...], preferred_element_type=F32)
    o_ref[...] = x_ref[...] + g_ref[0] * y


def mm_res(a, w, x, gate):
    t, k = a.shape
    d = w.shape[1]
    tm = min(ROW_TILE, t)
    g, mod_spec = _mod_operand(gate, t, tm)
    return pl.pallas_call(
        _mm_res_kernel,
        grid=(t // tm,),
        in_specs=[pl.BlockSpec((tm, k), lambda i: (i, 0)), pl.BlockSpec((k, d), lambda i: (0, 0)),
                  pl.BlockSpec((tm, d), lambda i: (i, 0)), mod_spec],
        out_specs=pl.BlockSpec((tm, d), lambda i: (i, 0)),
        out_shape=jax.ShapeDtypeStruct((t, d), F32),
        compiler_params=pltpu.CompilerParams(dimension_semantics=("arbitrary",),
                                             vmem_limit_bytes=VMEM_LIMIT_BYTES),
        name="out_proj_residual",
    )(a, w.astype(BF16), x, g)


def _ffn_kernel(routed, x_ref, sh_ref, sc_ref, gt_ref, wr_ref, wg_ref, wu_ref, wd_ref, o_ref, h_scr, cmb_scr, acc_scr):
    e = pl.program_id(1)
    lane = lax.broadcasted_iota(jnp.int32, cmb_scr.shape, 1)

    @pl.when(e == 0)
    def _():
        hb = _modulated(x_ref[...], sh_ref[0], sc_ref[0]).astype(BF16)
        h_scr[...] = hb
        acc_scr[...] = jnp.zeros(acc_scr.shape, F32)
        if routed:
            logits = jnp.dot(hb, wr_ref[...], preferred_element_type=F32)
            lg = jnp.where(lane < N_EXPERTS, logits, -jnp.inf)
            m1 = jnp.max(lg, axis=1, keepdims=True)
            i1 = jnp.min(jnp.where(lg == m1, lane, LANES), axis=1, keepdims=True)
            lg2 = jnp.where(lane == i1, -jnp.inf, lg)
            m2 = jnp.max(lg2, axis=1, keepdims=True)
            i2 = jnp.min(jnp.where(lg2 == m2, lane, LANES), axis=1, keepdims=True)
            t = jnp.exp(m2 - m1)
            p1 = 1.0 / (1.0 + t)
            cmb_scr[...] = jnp.where(lane == i1, p1, 0.0) + jnp.where(lane == i2, t * p1, 0.0)

    hb = h_scr[...]
    g = jnp.dot(hb, wg_ref[0], preferred_element_type=F32)
    u = jnp.dot(hb, wu_ref[0], preferred_element_type=F32)
    act = g * (1.0 / (1.0 + jnp.exp(-g))) * u
    y = jnp.dot(act.astype(BF16), wd_ref[0], preferred_element_type=F32)
    if routed:
        y = y * jnp.sum(jnp.where(lane == e, cmb_scr[...], 0.0), axis=1, keepdims=True)
    acc_scr[...] += y

    @pl.when(e == pl.num_programs(1) - 1)
    def _():
        o_ref[...] = x_ref[...] + gt_ref[0] * acc_scr[...]


FFN_BLOCK = 1408


def ffn_block(x, shift, scale, gate, w_router, w_gate, w_up, w_down):
    t, d = x.shape
    routed = w_router is not None
    tm = min(ROW_TILE, t)
    sh, mod_spec = _mod_operand(shift, t, tm)
    sc, _ = _mod_operand(scale, t, tm)
    gt, _ = _mod_operand(gate, t, tm)
    if routed:
        n_e, _, f = w_gate.shape
        assert f == FFN_BLOCK and n_e == N_EXPERTS
        wr = jnp.pad(w_router, ((0, 0), (0, LANES - n_e))).astype(BF16)
        up_spec = pl.BlockSpec((1, d, f), lambda i, e: (e, 0, 0))
        down_spec = pl.BlockSpec((1, f, d), lambda i, e: (e, 0, 0))
    else:
        f_all = w_gate.shape[1]
        assert f_all % FFN_BLOCK == 0
        n_e, f = f_all // FFN_BLOCK, FFN_BLOCK
        wr = jnp.zeros((d, LANES), BF16)
        w_gate, w_up, w_down = w_gate[None], w_up[None], w_down[None]
        up_spec = pl.BlockSpec((1, d, f), lambda i, e: (0, 0, e))
        down_spec = pl.BlockSpec((1, f, d), lambda i, e: (0, e, 0))
    return pl.pallas_call(
        functools.partial(_ffn_kernel, routed),
        grid=(t // tm, n_e),
        in_specs=[pl.BlockSpec((tm, d), lambda i, e: (i, 0)), mod_spec, mod_spec, mod_spec,
                  pl.BlockSpec((d, LANES), lambda i, e: (0, 0)), up_spec, up_spec, down_spec],
        out_specs=pl.BlockSpec((tm, d), lambda i, e: (i, 0)),
        out_shape=jax.ShapeDtypeStruct((t, d), F32),
        scratch_shapes=[pltpu.VMEM((tm, d), BF16), pltpu.VMEM((tm, LANES), F32), pltpu.VMEM((tm, d), F32)],
        compiler_params=pltpu.CompilerParams(dimension_semantics=("arbitrary", "arbitrary"),
                                             vmem_limit_bytes=VMEM_LIMIT_BYTES),
        name="routed_experts" if routed else "dense_swiglu",
    )(x, sh, sc, gt, wr, w_gate.astype(BF16), w_up.astype(BF16), w_down.astype(BF16))


RW_CHUNK = 64


def _bdot(a, b):
    return jnp.dot(a.astype(BF16), b.astype(BF16), preferred_element_type=F32)


def _bdot_nt(a, b):
    return lax.dot_general(a.astype(BF16), b.astype(BF16), (((1,), (1,)), ((), ())), preferred_element_type=F32)


def _bdot_tn(a, b):
    return lax.dot_general(a.astype(BF16), b.astype(BF16), (((0,), (0,)), ((), ())), preferred_element_type=F32)


def _split3(x):
    hi = x.astype(BF16)
    r1 = x - hi.astype(F32)
    mid = r1.astype(BF16)
    lo = (r1 - mid.astype(F32)).astype(BF16)
    return hi, mid, lo


def _rwkv_chunk_heads(r, lw, k, v, a, b, h0, tri_bf, strict, incl, eye):
    nh = len(r)
    hs = range(nh)
    cum = []
    for i in hs:
        hi, mid, lo = _split3(lw[i])
        cum.append(jnp.dot(tri_bf, hi, preferred_element_type=F32) + jnp.dot(tri_bf, mid, preferred_element_type=F32)
                   + jnp.dot(tri_bf, lo, preferred_element_type=F32))
    cum_last = [cum[i][RW_CHUNK - 1:RW_CHUNK, :] for i in hs]
    e_neg = [jnp.exp(-cum[i]) for i in hs]
    e_end = [jnp.exp(cum_last[i] - cum[i]) for i in hs]
    a_t = [a[i] * jnp.exp(cum[i] - lw[i]) for i in hs]
    r_t = [r[i] * jnp.exp(cum[i]) for i in hs]
    k_t = [k[i] * e_neg[i] for i in hs]
    b_t = [b[i] * e_neg[i] for i in hs]
    k_e = [k[i] * e_end[i] for i in hs]
    b_e = [b[i] * e_end[i] for i in hs]
    a_ab = [jnp.where(strict, _bdot_nt(a_t[i], b_t[i]), 0.0) for i in hs]
    a_ak = [jnp.where(strict, _bdot_nt(a_t[i], k_t[i]), 0.0) for i in hs]
    a_rb = [jnp.where(incl, _bdot_nt(r_t[i], b_t[i]), 0.0) for i in hs]
    a_rk = [jnp.where(incl, _bdot_nt(r_t[i], k_t[i]), 0.0) for i in hs]
    x = [eye + a_ab[i] for i in hs]
    y = list(a_ab)
    for _ in range(int(math.log2(RW_CHUNK)) - 1):
        y = [_bdot(y[i], y[i]) for i in hs]
        x = [x[i] + _bdot(x[i], y[i]) for i in hs]
    a_p = [_bdot(x[i], a_t[i]) for i in hs]
    akv = [_bdot(a_ak[i], v[i]) for i in hs]
    v_p = [_bdot(x[i], akv[i]) for i in hs]
    r_p = [r_t[i] + _bdot(a_rb[i], a_p[i]) for i in hs]
    y_p = [_bdot(a_rk[i], v[i]) + _bdot(a_rb[i], v_p[i]) for i in hs]
    m_lr = [_bdot_tn(b_e[i], a_p[i]) for i in hs]
    g = [_bdot_tn(k_e[i], v[i]) + _bdot_tn(b_e[i], v_p[i]) for i in hs]
    w_col = [jnp.sum(jnp.where(eye, jnp.exp(cum_last[i]), 0.0), axis=1, keepdims=True) for i in hs]
    y_out = [_bdot(r_p[i], h0[i]) + y_p[i] for i in hs]
    h_new = [w_col[i] * h0[i] + _bdot(m_lr[i], h0[i]) + g[i] for i in hs]
    return y_out, h_new


RW_HEAD_GROUP = 16


def _rwkv_kernel(r_ref, lw_ref, k_ref, v_ref, a_ref, b_ref, h0_ref, y_ref, hout_ref, h_scr):
    c = pl.program_id(0)

    @pl.when(c == 0)
    def _():
        h_scr[...] = h0_ref[...]

    rows = lax.broadcasted_iota(jnp.int32, (RW_CHUNK, RW_CHUNK), 0)
    cols = lax.broadcasted_iota(jnp.int32, (RW_CHUNK, RW_CHUNK), 1)
    strict = rows > cols
    incl = rows >= cols
    eye = rows == cols
    tri_bf = jnp.where(incl, 1.0, 0.0).astype(BF16)
    n_b, _, width = r_ref.shape
    n = h_scr.shape[-1]
    heads = [(b, h) for b in range(n_b) for h in range(width // n)]
    for i0 in range(0, len(heads), RW_HEAD_GROUP):
        grp = heads[i0:i0 + RW_HEAD_GROUP]
        seqs = [[ref[b, :, h * n:(h + 1) * n] for b, h in grp] for ref in (r_ref, lw_ref, k_ref, v_ref, a_ref, b_ref)]
        states = [h_scr[b * (width // n) + h] for b, h in grp]
        y_out, h_new = _rwkv_chunk_heads(*seqs, states, tri_bf, strict, incl, eye)
        for j, (b, h) in enumerate(grp):
            y_ref[b, :, h * n:(h + 1) * n] = y_out[j]
            h_scr[b * (width // n) + h] = h_new[j]

    @pl.when(c == pl.num_programs(0) - 1)
    def _():
        hout_ref[...] = h_scr[...]


def rwkv7_chunked(r, lw, k, v, a, b, s0):
    bt, L, H, N = r.shape
    assert L % RW_CHUNK == 0
    flat = lambda t: t.reshape(bt, L, H * N)
    h0 = s0.transpose(0, 1, 3, 2).reshape(bt * H, N, N)
    seq_spec = pl.BlockSpec((bt, RW_CHUNK, H * N), lambda c: (0, c, 0))
    st_spec = pl.BlockSpec((bt * H, N, N), lambda c: (0, 0, 0))
    y, h_fin = pl.pallas_call(
        _rwkv_kernel,
        grid=(L // RW_CHUNK,),
        in_specs=[seq_spec] * 6 + [st_spec],
        out_specs=[seq_spec, st_spec],
        out_shape=[jax.ShapeDtypeStruct((bt, L, H * N), F32), jax.ShapeDtypeStruct((bt * H, N, N), F32)],
        scratch_shapes=[pltpu.VMEM((bt * H, N, N), F32)],
        compiler_params=pltpu.CompilerParams(dimension_semantics=("arbitrary",),
                                             vmem_limit_bytes=VMEM_LIMIT_BYTES),
        name="rwkv7_chunked",
    )(flat(r), flat(lw), flat(k), flat(v), flat(a), flat(b), h0)
    return y.reshape(bt, L, H, N), h_fin.reshape(bt, H, N, N).transpose(0, 1, 3, 2)


def _gla_kernel(q_ref, k_ref, v_ref, g_ref, s0_ref, o_ref, sout_ref, s_scr):
    c = pl.program_id(0)

    @pl.when(c == 0)
    def _():
        s_scr[...] = s0_ref[...]

    rows = lax.broadcasted_iota(jnp.int32, (GLA_CHUNK, GLA_CHUNK), 0)
    cols = lax.broadcasted_iota(jnp.int32, (GLA_CHUNK, GLA_CHUNK), 1)
    incl = rows >= cols
    eye = lax.broadcasted_iota(jnp.int32, (GLA_DK, GLA_DK), 0) == lax.broadcasted_iota(jnp.int32, (GLA_DK, GLA_DK), 1)
    tri_bf = jnp.where(incl, 1.0, 0.0).astype(BF16)
    n_b, _, qk_width = q_ref.shape
    n_h = qk_width // GLA_DK
    heads = [(b, h) for b in range(n_b) for h in range(n_h)]
    hs = range(len(heads))
    qs = [q_ref[b, :, h * GLA_DK:(h + 1) * GLA_DK] for b, h in heads]
    ks = [k_ref[b, :, h * GLA_DK:(h + 1) * GLA_DK] for b, h in heads]
    vs = [v_ref[b, :, h * GLA_DV:(h + 1) * GLA_DV] for b, h in heads]
    bcum = []
    for b, h in heads:
        hi, mid, lo = _split3(g_ref[b, :, h * GLA_DK:(h + 1) * GLA_DK])
        bcum.append(jnp.dot(tri_bf, hi, preferred_element_type=F32) + jnp.dot(tri_bf, mid, preferred_element_type=F32)
                    + jnp.dot(tri_bf, lo, preferred_element_type=F32))
    blast = [bcum[i][GLA_CHUNK - 1:GLA_CHUNK, :] for i in hs]
    q_e = [qs[i] * jnp.exp(bcum[i]) for i in hs]
    k_e = [ks[i] * jnp.exp(-bcum[i]) for i in hs]
    k_end = [ks[i] * jnp.exp(blast[i] - bcum[i]) for i in hs]
    att = [jnp.where(incl, _bdot_nt(q_e[i], k_e[i]), 0.0) for i in hs]
    s_old = [s_scr[i] for i in hs]
    o = [_bdot(att[i], vs[i]) + _bdot(q_e[i], s_old[i]) for i in hs]
    kv = [_bdot_tn(k_end[i], vs[i]) for i in hs]
    dec = [jnp.sum(jnp.where(eye, jnp.exp(blast[i]), 0.0), axis=1, keepdims=True) for i in hs]
    for i, (b, h) in enumerate(heads):
        o_ref[b, :, h * GLA_DV:(h + 1) * GLA_DV] = o[i]
        s_scr[i] = dec[i] * s_old[i] + kv[i]

    @pl.when(c == pl.num_programs(0) - 1)
    def _():
        sout_ref[...] = s_scr[...]


def gla_chunked_pallas(q, k, v, log_a, s0):
    bt, L, H, dk = q.shape
    dv = v.shape[-1]
    assert L % GLA_CHUNK == 0
    flat = lambda t: t.reshape(bt, L, H * t.shape[-1])
    qk_spec = pl.BlockSpec((bt, GLA_CHUNK, H * dk), lambda c: (0, c, 0))
    v_spec = pl.BlockSpec((bt, GLA_CHUNK, H * dv), lambda c: (0, c, 0))
    st_spec = pl.BlockSpec((bt * H, dk, dv), lambda c: (0, 0, 0))
    o, s_fin = pl.pallas_call(
        _gla_kernel,
        grid=(L // GLA_CHUNK,),
        in_specs=[qk_spec, qk_spec, v_spec, qk_spec, st_spec],
        out_specs=[v_spec, st_spec],
        out_shape=[jax.ShapeDtypeStruct((bt, L, H * dv), F32), jax.ShapeDtypeStruct((bt * H, dk, dv), F32)],
        scratch_shapes=[pltpu.VMEM((bt * H, dk, dv), F32)],
        compiler_params=pltpu.CompilerParams(dimension_semantics=("arbitrary",),
                                             vmem_limit_bytes=VMEM_LIMIT_BYTES),
        name="gla_chunked",
    )(flat(q), flat(k), flat(v), flat(log_a), s0.reshape(bt * H, dk, dv))
    return o.reshape(bt, L, H, dv), s_fin.reshape(bt, H, dk, dv)


DSA_QB = 128
DSA_KC = 512
MASK_NEG = -1e30
INT_MIN = -2 ** 31
INT16_MIN = -2 ** 15
ONE16 = np.int16(1)
ZERO16 = np.int16(0)
ONE = np.float32(1.0)
ZERO = np.float32(0.0)
G_PER_KV = ATT_HEADS // ATT_KV_HEADS


def _sortable_key(x):
    bits = lax.bitcast_convert_type(x + 0.0, jnp.int32)
    return bits ^ ((bits >> 31) & 0x7FFFFFFF)


def _dsa_kernel(topk, q_ref, qi_ref, wi_ref, k_ref, v_ref, kit_ref, upper_ref, o_ref,
                wb_scr, key_scr, khi_scr, klo_scr, m_scr, acc_scr):
    i = pl.program_id(1)
    n_chunks = (i * DSA_QB) // DSA_KC + 1
    nt = DSA_KC // LANES
    q_pos = i * DSA_QB + lax.broadcasted_iota(jnp.int32, (DSA_QB, LANES), 0)
    lane = lax.broadcasted_iota(jnp.int32, (DSA_QB, LANES), 1)

    for h in range(IDX_HEADS):
        wb_scr[h] = jnp.broadcast_to(wi_ref[0, :, h:h + 1], (DSA_QB, LANES))

    def score_chunk(c, carry):
        dots = jnp.dot(qi_ref[0, 0], kit_ref[0, c], preferred_element_type=F32)
        for jt in range(nt):
            sc = jnp.zeros((DSA_QB, LANES), F32)
            for h in range(IDX_HEADS):
                d = dots[h * DSA_QB:(h + 1) * DSA_QB, jt * LANES:(jt + 1) * LANES]
                sc = sc + jnp.maximum(d, 0.0) * wb_scr[h]
            k_pos = c * DSA_KC + jt * LANES + lane
            sc = jnp.where(k_pos <= q_pos, sc, -jnp.inf)
            key = _sortable_key(sc)
            key_scr[c, :, jt * LANES:(jt + 1) * LANES] = key
            khi_scr[c, :, jt * LANES:(jt + 1) * LANES] = (key >> 16).astype(jnp.int16)
        return carry

    lax.fori_loop(0, n_chunks, score_chunk, 0)

    def count16(scr, cand, strict):
        cand_b = jnp.broadcast_to(cand.astype(jnp.int16), (DSA_QB, LANES))

        def body(c, acc):
            for jt in range(nt):
                tile = scr[c, :, jt * LANES:(jt + 1) * LANES]
                hit = (tile > cand_b) if strict else (tile >= cand_b)
                acc = acc + jnp.where(hit, ONE16, ZERO16)
            return acc

        acc = lax.fori_loop(0, n_chunks, body, jnp.zeros((DSA_QB, LANES), jnp.int16))
        return jnp.sum(acc.astype(F32), axis=1, keepdims=True)

    def bisect16(scr, k_need):
        cur = jnp.where(count16(scr, jnp.zeros((DSA_QB, 1), jnp.int32), False) >= k_need, 0, INT16_MIN)
        cur = cur.astype(jnp.int32)

        def bit_step(p, cur):
            cand = cur | (jnp.int32(1) << (14 - p))
            return jnp.where(count16(scr, cand, False) >= k_need, cand, cur)

        return lax.fori_loop(0, 15, bit_step, cur)

    kf = float(topk)
    thr_hi = bisect16(khi_scr, kf)
    above = count16(khi_scr, thr_hi, True)
    thr_hi_b = jnp.broadcast_to(thr_hi, (DSA_QB, LANES))

    def low_halves(c, carry):
        for jt in range(nt):
            key = key_scr[c, :, jt * LANES:(jt + 1) * LANES]
            lo = (key & 0xFFFF) - 32768
            klo_scr[c, :, jt * LANES:(jt + 1) * LANES] = jnp.where((key >> 16) == thr_hi_b, lo, INT16_MIN).astype(jnp.int16)
        return carry

    lax.fori_loop(0, n_chunks, low_halves, 0)
    thr_lo = bisect16(klo_scr, kf - above)
    n_gt = above + count16(klo_scr, thr_lo, True)
    need = kf - n_gt
    thr = (thr_hi << 16) | (thr_lo + 32768)

    m_scr[...] = jnp.full(m_scr.shape, MASK_NEG, F32)
    acc_scr[...] = jnp.zeros(acc_scr.shape, F32)

    def attend_chunk(c, tie_carry):
        k0 = pl.multiple_of(c * DSA_KC, DSA_KC)
        keys = key_scr[c]
        thr_c = jnp.broadcast_to(thr, (DSA_QB, DSA_KC))
        eq = keys == thr_c
        eq_f = jnp.where(eq, ONE, ZERO)
        rank = tie_carry + jnp.dot(eq_f.astype(BF16), upper_ref[...], preferred_element_type=F32)
        k_pos = k0 + lax.broadcasted_iota(jnp.int32, (DSA_QB, DSA_KC), 1)
        qp = i * DSA_QB + lax.broadcasted_iota(jnp.int32, (DSA_QB, DSA_KC), 0)
        sel = jnp.where(keys > thr_c, ONE, jnp.where(rank < need, eq_f, ZERO))
        bias = jnp.where((sel > 0.5) & (k_pos <= qp), ZERO, np.float32(MASK_NEG))
        v_c = v_ref[0, pl.ds(k0, DSA_KC), :]
        rows = G_PER_KV * DSA_QB
        for n in range(ATT_KV_HEADS):
            s = lax.dot_general(q_ref[0, 0, n], k_ref[0, n, pl.ds(k0, DSA_KC), :], (((1,), (1,)), ((), ())),
                                preferred_element_type=F32)
            s = (s.reshape(G_PER_KV, DSA_QB, DSA_KC) + bias[None]).reshape(rows, DSA_KC)
            tiles = [s[:, jt * LANES:(jt + 1) * LANES] for jt in range(nt)]
            tile_max = functools.reduce(jnp.maximum, tiles)
            m_old = m_scr[n]
            m_new = jnp.maximum(m_old, jnp.max(tile_max, axis=1, keepdims=True))
            alpha = jnp.exp(m_old - m_new)
            p = jnp.concatenate([jnp.exp((t - m_new).astype(BF16)) for t in tiles], axis=1)
            pv = jnp.dot(p, v_c[:, n * LANES:(n + 1) * LANES], preferred_element_type=F32)
            acc_scr[n] = alpha * acc_scr[n] + pv
            m_scr[n] = m_new
        return tie_carry + jnp.sum(eq_f, axis=1, keepdims=True)

    lax.fori_loop(0, n_chunks, attend_chunk, jnp.zeros((DSA_QB, 1), F32))

    for n in range(ATT_KV_HEADS):
        acc = acc_scr[n]
        o_n = acc[:, :ATT_DH] / acc[:, ATT_DH:]
        for g in range(G_PER_KV):
            hd = n * G_PER_KV + g
            o_ref[0, :, hd * ATT_DH:(hd + 1) * ATT_DH] = o_n[g * DSA_QB:(g + 1) * DSA_QB, :]


def dsa_prompt_attention(q, k, v, qi, ki, wi):
    bt, L = q.shape[:2]
    assert L % DSA_KC == 0 and L // 4 >= 1
    topk = min(TOPK_MAX, L // 4)
    nb = L // DSA_QB
    nc = L // DSA_KC
    kit = ki.astype(BF16).reshape(bt, nc, DSA_KC, IDX_DH).transpose(0, 1, 3, 2)
    upper = jnp.triu(jnp.ones((DSA_KC, DSA_KC), BF16), 1)
    rows = G_PER_KV * DSA_QB
    q_blocks = (q * ATT_DH ** -0.5).astype(BF16).reshape(bt, nb, DSA_QB, ATT_KV_HEADS, G_PER_KV, ATT_DH)
    q_blocks = q_blocks.transpose(0, 1, 3, 4, 2, 5).reshape(bt, nb, ATT_KV_HEADS, rows, ATT_DH)
    qi_blocks = qi.astype(BF16).reshape(bt, nb, DSA_QB, IDX_HEADS, IDX_DH).transpose(0, 1, 3, 2, 4)
    qi_blocks = qi_blocks.reshape(bt, nb, IDX_HEADS * DSA_QB, IDX_DH)
    v_aug = jnp.concatenate([v.astype(BF16), jnp.ones(v.shape, BF16)], axis=-1).reshape(bt, L, ATT_KV_HEADS * LANES)
    return pl.pallas_call(
        functools.partial(_dsa_kernel, topk),
        grid=(bt, nb),
        in_specs=[
            pl.BlockSpec((1, 1, ATT_KV_HEADS, rows, ATT_DH), lambda b, i: (b, i, 0, 0, 0)),
            pl.BlockSpec((1, 1, IDX_HEADS * DSA_QB, IDX_DH), lambda b, i: (b, i, 0, 0)),
            pl.BlockSpec((1, DSA_QB, IDX_HEADS), lambda b, i: (b, i, 0)),
            pl.BlockSpec((1, ATT_KV_HEADS, L, ATT_DH), lambda b, i: (b, 0, 0, 0)),
            pl.BlockSpec((1, L, ATT_KV_HEADS * LANES), lambda b, i: (b, 0, 0)),
            pl.BlockSpec((1, nc, IDX_DH, DSA_KC), lambda b, i: (b, 0, 0, 0)),
            pl.BlockSpec((DSA_KC, DSA_KC), lambda b, i: (0, 0)),
        ],
        out_specs=pl.BlockSpec((1, DSA_QB, ATT_Q), lambda b, i: (b, i, 0)),
        out_shape=jax.ShapeDtypeStruct((bt, L, ATT_Q), F32),
        scratch_shapes=[
            pltpu.VMEM((IDX_HEADS, DSA_QB, LANES), F32),
            pltpu.VMEM((nc, DSA_QB, DSA_KC), jnp.int32),
            pltpu.VMEM((nc, DSA_QB, DSA_KC), jnp.int16),
            pltpu.VMEM((nc, DSA_QB, DSA_KC), jnp.int16),
            pltpu.VMEM((ATT_KV_HEADS, rows, LANES), F32),
            pltpu.VMEM((ATT_KV_HEADS, rows, LANES), F32),
        ],
        compiler_params=pltpu.CompilerParams(dimension_semantics=("arbitrary", "arbitrary"),
                                             vmem_limit_bytes=VMEM_LIMIT_BYTES),
        name="dsa_prompt_attention",
    )(q_blocks, qi_blocks, wi,
      k.astype(BF16).transpose(0, 2, 1, 3), v_aug, kit, upper)


NEW_PAD = 16
PAGES_PER_STEP = 4


def _dsa_sample_kernel(topk, n_new, pt_ref, qbd_ref, qi_ref, wrep_ref, knew_ref, vnew_ref, kinew_ref, *rest):
    pages = rest[:3 * PAGES_PER_STEP]
    upper_ref, o_ref, kbuf, vbuf, kibuf = rest[3 * PAGES_PER_STEP:]
    p = pl.program_id(1)
    n_steps = pl.num_programs(1)
    past = n_steps * PAGES_PER_STEP * PAGE_SIZE
    n_keys = kbuf.shape[0]
    nq = qbd_ref.shape[1] // ATT_HEADS
    for buf, refs in zip((kbuf, vbuf, kibuf), (pages[:PAGES_PER_STEP], pages[PAGES_PER_STEP:2 * PAGES_PER_STEP],
                                               pages[2 * PAGES_PER_STEP:])):
        for s, ref in enumerate(refs):
            row0 = pl.multiple_of((p * PAGES_PER_STEP + s) * PAGE_SIZE, PAGE_SIZE)
            buf[pl.ds(row0, PAGE_SIZE), :] = ref[0, 0].astype(BF16)

    @pl.when(p == n_steps - 1)
    def _():
        tail = n_keys - past
        kbuf[past:, :] = jnp.zeros((tail, ATT_KV), BF16)
        vbuf[past:, :] = jnp.zeros((tail, ATT_KV), BF16)
        kibuf[past:, :] = jnp.zeros((tail, IDX_DH), BF16)
        kbuf[past:past + NEW_PAD, :] = knew_ref[0].astype(BF16)
        vbuf[past:past + NEW_PAD, :] = vnew_ref[0].astype(BF16)
        kibuf[past:past + NEW_PAD, :] = kinew_ref[0].astype(BF16)

        dots = lax.dot_general(qi_ref[0], kibuf[...], (((1,), (1,)), ((), ())), preferred_element_type=F32)
        k_idx = lax.broadcasted_iota(jnp.int32, (8, n_keys), 1)
        q_row = lax.broadcasted_iota(jnp.int32, (8, n_keys), 0)
        valid = (k_idx - past <= q_row) & (q_row < nq)
        rows = []
        for qn in range(nq):
            d = jnp.maximum(dots[qn * IDX_HEADS:(qn + 1) * IDX_HEADS], 0.0)
            w = wrep_ref[0, qn * IDX_HEADS:(qn + 1) * IDX_HEADS, :]
            parts = []
            for jt in range(n_keys // LANES):
                parts.append(jnp.sum(d[:, jt * LANES:(jt + 1) * LANES] * w, axis=0, keepdims=True))
            rows.append(jnp.concatenate(parts, axis=1))
        rows.append(jnp.zeros((8 - nq, n_keys), F32))
        sc = jnp.where(valid, jnp.concatenate(rows, axis=0), -jnp.inf)
        keys = _sortable_key(sc)

        kf = float(topk)

        def count_ge(cand):
            return jnp.sum(jnp.where(keys >= cand, ONE, ZERO), axis=1, keepdims=True)

        cur = jnp.where(count_ge(jnp.zeros((8, 1), jnp.int32)) >= kf, 0, INT_MIN).astype(jnp.int32)

        def bit_step(i, cur):
            cand = cur | (jnp.int32(1) << (30 - i))
            return jnp.where(count_ge(cand) >= kf, cand, cur)

        thr = lax.fori_loop(0, 31, bit_step, cur)
        n_gt = jnp.sum(jnp.where(keys > thr, ONE, ZERO), axis=1, keepdims=True)
        need = kf - n_gt
        eq_f = jnp.where(keys == thr, ONE, ZERO)
        carry = jnp.zeros((8, 1), F32)
        ranks = []
        for jt in range(n_keys // LANES):
            e = eq_f[:, jt * LANES:(jt + 1) * LANES]
            ranks.append(carry + jnp.dot(e.astype(BF16), upper_ref[...], preferred_element_type=F32))
            carry = carry + jnp.sum(e, axis=1, keepdims=True)
        rank = jnp.concatenate(ranks, axis=1)
        sel = jnp.where(keys > thr, ONE, jnp.where(rank < need, eq_f, ZERO))
        bias = jnp.where((sel > 0.5) & valid, ZERO, np.float32(MASK_NEG))

        s = lax.dot_general(qbd_ref[0], kbuf[...], (((1,), (1,)), ((), ())), preferred_element_type=F32)
        s = jnp.concatenate([s[qn * ATT_HEADS:(qn + 1) * ATT_HEADS] + bias[qn:qn + 1] for qn in range(nq)], axis=0)
        m = jnp.max(s, axis=1, keepdims=True)
        pr = jnp.exp(s - m)
        l = jnp.sum(pr, axis=1, keepdims=True)
        o_ref[0] = jnp.dot(pr.astype(BF16), vbuf[...], preferred_element_type=F32) / l


def dsa_sample_attention(q, k_new, v_new, qi, ki_new, wi, cache_k, cache_v, cache_kidx, page_table, j):
    bt, nq = q.shape[:2]
    n_pages = page_table.shape[1]
    past = n_pages * PAGE_SIZE
    assert nq <= 8 and nq <= NEW_PAD
    topk = min(TOPK_MAX, (past + nq) // 4)
    n_keys = past + LANES
    n_pool = cache_k.shape[1]
    qs = (q * ATT_DH ** -0.5).astype(BF16).reshape(bt, nq, ATT_KV_HEADS, G_PER_KV, 1, ATT_DH)
    eye = jnp.eye(ATT_KV_HEADS, dtype=BF16)[None, None, :, None, :, None]
    qbd = (qs * eye).reshape(bt, nq * ATT_HEADS, ATT_KV)
    qi2 = qi.astype(BF16).reshape(bt, nq * IDX_HEADS, IDX_DH)
    wrep = jnp.broadcast_to(wi.reshape(bt, nq * IDX_HEADS, 1), (bt, nq * IDX_HEADS, LANES))
    pad = lambda t: jnp.pad(t.reshape(bt, nq, -1), ((0, 0), (0, NEW_PAD - nq), (0, 0)))
    upper = jnp.triu(jnp.ones((LANES, LANES), BF16), 1)
    per_b = lambda shape: pl.BlockSpec((1,) + shape, lambda b, p, pt: (b, 0, 0))
    assert n_pages % PAGES_PER_STEP == 0

    def pages(width):
        return [pl.BlockSpec((1, 1, PAGE_SIZE, width), lambda b, p, pt, s=s: (j, pt[b, p * PAGES_PER_STEP + s], 0, 0))
                for s in range(PAGES_PER_STEP)]

    o_all = pl.pallas_call(
        functools.partial(_dsa_sample_kernel, topk, nq),
        grid_spec=pltpu.PrefetchScalarGridSpec(
            num_scalar_prefetch=1,
            grid=(bt, n_pages // PAGES_PER_STEP),
            in_specs=[per_b((nq * ATT_HEADS, ATT_KV)), per_b((nq * IDX_HEADS, IDX_DH)), per_b((nq * IDX_HEADS, LANES)),
                      per_b((NEW_PAD, ATT_KV)), per_b((NEW_PAD, ATT_KV)), per_b((NEW_PAD, IDX_DH))]
            + pages(ATT_KV) + pages(ATT_KV) + pages(IDX_DH)
            + [pl.BlockSpec((LANES, LANES), lambda b, p, pt: (0, 0))],
            out_specs=per_b((nq * ATT_HEADS, ATT_KV)),
            scratch_shapes=[pltpu.VMEM((n_keys, ATT_KV), BF16), pltpu.VMEM((n_keys, ATT_KV), BF16),
                            pltpu.VMEM((n_keys, IDX_DH), BF16)]),
        out_shape=jax.ShapeDtypeStruct((bt, nq * ATT_HEADS, ATT_KV), F32),
        compiler_params=pltpu.CompilerParams(dimension_semantics=("arbitrary", "arbitrary"),
                                             vmem_limit_bytes=VMEM_LIMIT_BYTES),
        name="dsa_sample_attention",
    )(page_table, qbd, qi2, wrep, pad(k_new), pad(v_new), pad(ki_new),
      *([cache_k.reshape(cache_k.shape[0], n_pool, PAGE_SIZE, ATT_KV)] * PAGES_PER_STEP),
      *([cache_v.reshape(cache_v.shape[0], n_pool, PAGE_SIZE, ATT_KV)] * PAGES_PER_STEP),
      *([cache_kidx] * PAGES_PER_STEP), upper)
    o6 = o_all.reshape(bt, nq, ATT_KV_HEADS, G_PER_KV, ATT_KV_HEADS, ATT_DH)
    o = jnp.stack([o6[:, :, n, :, n, :] for n in range(ATT_KV_HEADS)], axis=2)
    return o.reshape(bt, nq, ATT_Q)


def rms_norm(x, eps=EPS):
    return x * lax.rsqrt(jnp.mean(x * x, axis=-1, keepdims=True) + eps)


def layer_norm(x, g, b, eps):
    mu = jnp.mean(x, axis=-1, keepdims=True)
    var = jnp.mean(jnp.square(x - mu), axis=-1, keepdims=True)
    return (x - mu) * lax.rsqrt(var + eps) * g + b


def rotary(x, pos):
    half = x.shape[-1] // 2
    inv_freq = jnp.power(ROPE_THETA, -jnp.arange(half, dtype=F32) / half)
    ang = pos.astype(F32)[:, None] * inv_freq[None, :]
    cos = jnp.cos(ang)[None, :, None, :]
    sin = jnp.sin(ang)[None, :, None, :]
    x1, x2 = x[..., :half], x[..., half:]
    return jnp.concatenate([x1 * cos - x2 * sin, x2 * cos + x1 * sin], axis=-1)


def gla_chunked(q, k, v, log_a, s0):
    bt, L, H, _ = q.shape
    dv = v.shape[-1]
    C = min(GLA_CHUNK, L)
    n = -(-L // C)
    pad = n * C - L

    def prep(t):
        t = jnp.pad(t, ((0, 0), (0, pad), (0, 0), (0, 0)))
        return t.reshape(bt, n, C, H, t.shape[-1]).transpose(1, 0, 3, 2, 4)

    qc, kc, vc, gc = prep(q), prep(k), prep(v), prep(log_a)
    bcum = jnp.cumsum(gc, axis=3)
    blast = bcum[:, :, :, -1:, :]
    q_e = qc * jnp.exp(bcum)
    k_e = kc * jnp.exp(-bcum)
    k_end = kc * jnp.exp(blast - bcum)
    causal = jnp.tril(jnp.ones((C, C), dtype=bool))
    att = jnp.where(causal, jnp.einsum('nbhtd,nbhsd->nbhts', q_e, k_e), 0.0)
    o_intra = jnp.einsum('nbhts,nbhsv->nbhtv', att, vc)
    decay = jnp.exp(blast[:, :, :, 0, :])
    kv = jnp.einsum('nbhsd,nbhsv->nbhdv', k_end, vc)

    def step(s, inp):
        dec, kv_i = inp
        return dec[..., None] * s + kv_i, s

    s_fin, s_start = lax.scan(step, s0, (decay, kv))
    o = o_intra + jnp.einsum('nbhtd,nbhdv->nbhtv', q_e, s_start)
    o = o.transpose(1, 0, 3, 2, 4).reshape(bt, n * C, H, dv)[:, :L]
    return o, s_fin


def rwkv7_scan(r, w, k, v, a, b, s0):
    xs = tuple(t.transpose(1, 0, 2, 3) for t in (r, w, k, v, a, b))

    def step(s, inp):
        r_t, w_t, k_t, v_t, a_t, b_t = inp
        sa = jnp.einsum('bhvk,bhk->bhv', s, a_t)
        s = s * w_t[:, :, None, :] + sa[..., None] * b_t[:, :, None, :] + v_t[..., None] * k_t[:, :, None, :]
        return s, jnp.einsum('bhvk,bhk->bhv', s, r_t)

    s_fin, y = lax.scan(step, s0, xs)
    return y.transpose(1, 0, 2, 3), s_fin


def even_mixer(proj, shift_prev, s_gla, s_rw, j, P):
    bt, L, _ = proj.shape
    pa, pb_raw = proj[..., :P_A], proj[..., P_A:]
    q_a, k_a, v_a, r_a, g_low = jnp.split(pa, [GLA_QK, 2 * GLA_QK, 2 * GLA_QK + GLA_V, 2 * GLA_QK + 2 * GLA_V], axis=-1)
    ga = lambda t, d: t.reshape(bt, L, GLA_HEADS, d)
    log_alpha = jax.nn.log_sigmoid(g_low @ P['gla_w_gate'][j] + P['gla_b_gate'][j]) / GLA_TAU
    gla = gla_chunked_pallas if L % GLA_CHUNK == 0 else gla_chunked
    o_a, s_gla_new = gla(ga(q_a, GLA_DK) * GLA_DK ** -0.5, ga(k_a, GLA_DK), ga(v_a, GLA_DV),
                         ga(log_alpha, GLA_DK), s_gla)
    o_a = (rms_norm(o_a) * P['gla_norm_g'][j]).reshape(bt, L, GLA_V) * jax.nn.silu(r_a)
    prev = jnp.concatenate([shift_prev[:, None, :], pb_raw[:, :-1]], axis=1)
    pb = pb_raw + (prev - pb_raw) * P['rw_mu'][j]
    o1 = 3 * RW_W + RW_W_LORA
    r_b, k_b, v_b, w_low, a_low, gt_low = jnp.split(pb, [RW_W, 2 * RW_W, 3 * RW_W, o1, o1 + RW_A_LORA], axis=-1)
    w_log = -jax.nn.softplus(-(P['rw_w0'][j] + jnp.tanh(w_low) @ P['rw_w_decay'][j])) - 0.5
    log_decay = -jnp.exp(w_log)
    iclr = jax.nn.sigmoid(P['rw_a0'][j] + a_low @ P['rw_w_iclr'][j])
    gate = jax.nn.sigmoid(gt_low) @ P['rw_w_gate'][j]
    hb = lambda t: t.reshape(bt, L, RW_HEADS, RW_DH)
    kk = hb(k_b * P['rw_k_k'][j])
    kk = kk / jnp.maximum(jnp.sqrt(jnp.sum(kk * kk, axis=-1, keepdims=True)), 1e-12)
    k_b = k_b * (1 + (iclr - 1) * P['rw_k_a'][j])
    r_h, k_h, v_h, a_h = hb(r_b), hb(k_b), hb(v_b), hb(iclr)
    if L % RW_CHUNK == 0:
        y_b, s_rw_new = rwkv7_chunked(r_h, hb(log_decay), k_h, v_h, -kk, kk * a_h, s_rw)
    else:
        y_b, s_rw_new = rwkv7_scan(r_h, hb(jnp.exp(log_decay)), k_h, v_h, -kk, kk * a_h, s_rw)
    y_b = layer_norm(y_b, P['rw_ln_g'][j].reshape(RW_HEADS, RW_DH), P['rw_ln_b'][j].reshape(RW_HEADS, RW_DH), RW_LN_EPS)
    y_b = y_b + jnp.sum(r_h * k_h * P['rw_r_k'][j], axis=-1, keepdims=True) * v_h
    o_b = y_b.reshape(bt, L, RW_W) * gate
    return jnp.concatenate([o_a, o_b], axis=-1), pb_raw[:, -1], s_gla_new, s_rw_new


def odd_qkv(proj, pos, j, P):
    bt, L, _ = proj.shape
    o2 = ATT_Q + ATT_KV
    o3 = o2 + ATT_KV
    o4 = o3 + IDX_HEADS * IDX_DH
    q, k, v, qi, ki, wi = jnp.split(proj, [ATT_Q, o2, o3, o4, o4 + IDX_DH], axis=-1)
    q = rotary(rms_norm(q.reshape(bt, L, ATT_HEADS, ATT_DH)) * P['q_norm_g'][j], pos)
    k = rotary(rms_norm(k.reshape(bt, L, ATT_KV_HEADS, ATT_DH)) * P['k_norm_g'][j], pos)
    v = v.reshape(bt, L, ATT_KV_HEADS, ATT_DH)
    qi = rotary(qi.reshape(bt, L, IDX_HEADS, IDX_DH), pos)
    ki = rotary(layer_norm(ki, P['kidx_ln_g'][j], P['kidx_ln_b'][j], EPS)[:, :, None, :], pos)[:, :, 0, :]
    wi = wi * (IDX_HEADS * IDX_DH) ** -0.5
    return q, k, v, qi, ki, wi


def trunk(x, mods, pos, shift0, gla0, rw0, attend, P):
    bt, L, D = x.shape
    x2 = x.reshape(bt * L, D)
    glas, rws, shifts, ks, vs, kis = [], [], [], [], [], []
    for layer, mod in enumerate(mods):
        sh1, sc1, g1, sh2, sc2, g2 = jnp.split(mod, 6, axis=-1)
        j = layer // 2
        if layer % 2 == 0:
            proj = mod_mm(x2, sh1, sc1, P['w_in_even'][j]).reshape(bt, L, -1)
            mix, shf, sg, sr = even_mixer(proj, shift0[j], gla0[j], rw0[j], j, P)
            glas.append(sg)
            rws.append(sr)
            shifts.append(shf)
            x2 = mm_res(mix.reshape(bt * L, -1), P['w_out_even'][j], x2, g1)
            x2 = ffn_block(x2, sh2, sc2, g2, None, P['ffn_w_gate'][j], P['ffn_w_up'][j], P['ffn_w_down'][j])
        else:
            proj = mod_mm(x2, sh1, sc1, P['w_in_odd'][j]).reshape(bt, L, -1)
            q, k, v, qi, ki, wi = odd_qkv(proj, pos, j, P)
            ks.append(k)
            vs.append(v)
            kis.append(ki)
            x2 = mm_res(attend(j, q, k, v, qi, ki, wi).reshape(bt * L, -1), P['w_out_odd'][j], x2, g1)
            x2 = ffn_block(x2, sh2, sc2, g2, P['moe_router'][j], P['moe_w_gate'][j], P['moe_w_up'][j],
                           P['moe_w_down'][j])
    return (x2.reshape(bt, L, D), jnp.stack(glas), jnp.stack(rws), jnp.stack(shifts), jnp.stack(ks), jnp.stack(vs),
            jnp.stack(kis))


def kernel(x_prompt, x_sample, state_gla, state_rwkv, state_shift, cache_k, cache_v, cache_kidx, page_table, c_prompt, c_sample, w_ada, b_ada, w_in_even, gla_w_gate, gla_b_gate, gla_norm_g, rw_mu, rw_w0, rw_w_decay, rw_a0, rw_w_iclr, rw_w_gate, rw_k_k, rw_k_a, rw_r_k, rw_ln_g, rw_ln_b, w_out_even, w_in_odd, q_norm_g, k_norm_g, kidx_ln_g, kidx_ln_b, w_out_odd, ffn_w_gate, ffn_w_up, ffn_w_down, moe_router, moe_w_gate, moe_w_up, moe_w_down):
    P = dict(w_ada=w_ada, b_ada=b_ada, w_in_even=w_in_even, gla_w_gate=gla_w_gate, gla_b_gate=gla_b_gate,
             gla_norm_g=gla_norm_g, rw_mu=rw_mu, rw_w0=rw_w0, rw_w_decay=rw_w_decay, rw_a0=rw_a0,
             rw_w_iclr=rw_w_iclr, rw_w_gate=rw_w_gate, rw_k_k=rw_k_k, rw_k_a=rw_k_a, rw_r_k=rw_r_k,
             rw_ln_g=rw_ln_g, rw_ln_b=rw_ln_b, w_out_even=w_out_even, w_in_odd=w_in_odd,
             q_norm_g=q_norm_g, k_norm_g=k_norm_g, kidx_ln_g=kidx_ln_g, kidx_ln_b=kidx_ln_b,
             w_out_odd=w_out_odd, ffn_w_gate=ffn_w_gate, ffn_w_up=ffn_w_up, ffn_w_down=ffn_w_down,
             moe_router=moe_router, moe_w_gate=moe_w_gate, moe_w_up=moe_w_up, moe_w_down=moe_w_down)
    n_even = state_gla.shape[0]
    b_p, seq = x_prompt.shape[0], x_prompt.shape[1]
    zero_gla = jnp.zeros((n_even, b_p) + state_gla.shape[2:], state_gla.dtype)
    zero_rw = jnp.zeros((n_even, b_p) + state_rwkv.shape[2:], state_rwkv.dtype)
    zero_shift = jnp.zeros((n_even, b_p) + state_shift.shape[2:], state_shift.dtype)
    pos_p = jnp.arange(seq, dtype=jnp.int32)
    past = page_table.shape[1] * PAGE_SIZE
    pos_s = past + jnp.arange(x_sample.shape[1], dtype=jnp.int32)

    def prompt_attend(j, q, k, v, qi, ki, wi):
        return dsa_prompt_attention(q, k, v, qi, ki, wi)

    def sample_attend(j, q, k, v, qi, ki, wi):
        return dsa_sample_attention(q, k, v, qi, ki, wi, cache_k, cache_v, cache_kidx, page_table, j)

    silu_c = jax.nn.silu(jnp.concatenate([c_prompt, c_sample], axis=0))
    mods = [mm(silu_c, w_ada[layer]) + b_ada[layer] for layer in range(w_ada.shape[0])]
    mods_p = [m[:b_p] for m in mods]
    mods_s = [m[b_p:] for m in mods]

    y_prompt, p_gla, p_rw, p_shift, p_k, p_v, p_kidx = trunk(
        x_prompt, mods_p, pos_p, zero_shift, zero_gla, zero_rw, prompt_attend, P)
    y_sample, s_gla, s_rw, s_shift, s_k, s_v, s_kidx = trunk(
        x_sample, mods_s, pos_s, state_shift, state_gla, state_rwkv, sample_attend, P)
    return (y_prompt, y_sample, p_gla, p_rw, p_shift, p_k, p_v, p_kidx, s_gla, s_rw, s_shift, s_k, s_v, s_kidx)
```

```python
import functools
import math

import jax
import jax.numpy as jnp
import numpy as np
from jax import lax
from jax.experimental import pallas as pl
from jax.experimental.pallas import tpu as pltpu

F32 = jnp.float32
BF16 = jnp.bfloat16

PAGE_SIZE = 128
GLA_HEADS, GLA_DK, GLA_DV, GLA_GATE_RANK, GLA_TAU, GLA_CHUNK = 4, 64, 128, 16, 16.0, 64
RW_HEADS, RW_DH, RW_W_LORA, RW_A_LORA, RW_G_LORA, RW_LN_EPS = 8, 64, 32, 32, 96, 64e-5
ATT_HEADS, ATT_KV_HEADS, ATT_DH = 16, 4, 64
IDX_HEADS, IDX_DH = 8, 64
TOPK_MAX, Q_BLOCK, ROPE_THETA = 256, 128, 10000.0
N_EXPERTS, TOP_K_EXPERTS = 8, 2
EPS = 1e-6
GLA_QK = GLA_HEADS * GLA_DK
GLA_V = GLA_HEADS * GLA_DV
RW_W = RW_HEADS * RW_DH
ATT_Q = ATT_HEADS * ATT_DH
ATT_KV = ATT_KV_HEADS * ATT_DH

VMEM_LIMIT_BYTES = 56 * 1024 * 1024
LANES = 128
ROW_TILE = 512


def _mm_kernel(x_ref, w_ref, o_ref):
    o_ref[...] = jnp.dot(x_ref[...].astype(BF16), w_ref[...], preferred_element_type=F32)


def mm(x, w):
    m, k = x.shape
    n = w.shape[1]
    tm = min(ROW_TILE, m)
    assert m % tm == 0
    return pl.pallas_call(
        _mm_kernel,
        grid=(m // tm,),
        in_specs=[pl.BlockSpec((tm, k), lambda i: (i, 0)), pl.BlockSpec((k, n), lambda i: (0, 0))],
        out_specs=pl.BlockSpec((tm, n), lambda i: (i, 0)),
        out_shape=jax.ShapeDtypeStruct((m, n), F32),
        compiler_params=pltpu.CompilerParams(dimension_semantics=("arbitrary",),
                                             vmem_limit_bytes=VMEM_LIMIT_BYTES),
    )(x, w.astype(BF16))


def _mod_operand(mod, n_rows, tm):
    b, d = mod.shape
    per_b = n_rows // b
    if per_b % tm == 0:
        tiles_per_b = per_b // tm
        return mod[:, None, :], pl.BlockSpec((1, 1, d), lambda i, *_: (i // tiles_per_b, 0, 0))
    rows = jnp.repeat(mod, per_b, axis=0).reshape(n_rows // tm, tm, d)
    return rows, pl.BlockSpec((1, tm, d), lambda i, *_: (i, 0, 0))


def _modulated(x, shift, scale):
    xn = x * lax.rsqrt(jnp.mean(x * x, axis=-1, keepdims=True) + EPS)
    return xn * (1.0 + scale) + shift


def _aligned_offsets(widths):
    offs, off = [], 0
    for w in widths:
        offs.append(off)
        off += -(-w // LANES) * LANES
    return offs, off


def _mod_mm_kernel(widths, x_ref, sh_ref, sc_ref, w_ref, *o_refs):
    h = _modulated(x_ref[...], sh_ref[0], sc_ref[0])
    y = jnp.dot(h.astype(BF16), w_ref[...], preferred_element_type=F32)
    for o_ref, off, width in zip(o_refs, _aligned_offsets(widths)[0], widths):
        o_ref[...] = y[:, off:off + width]


def mod_mm(x, shift, scale, w, widths):
    t, d = x.shape
    assert sum(widths) == w.shape[1]
    tm = min(ROW_TILE, t)
    sh, mod_spec = _mod_operand(shift, t, tm)
    sc, _ = _mod_operand(scale, t, tm)
    offs, n_pad = _aligned_offsets(widths)
    starts = np.cumsum([0] + list(widths))
    cols = [jnp.pad(w[:, starts[i]:starts[i + 1]], ((0, 0), (0, -widths[i] % LANES))) for i in range(len(widths))]
    w_pad = jnp.concatenate(cols, axis=1).astype(BF16)
    return pl.pallas_call(
        functools.partial(_mod_mm_kernel, tuple(widths)),
        grid=(t // tm,),
        in_specs=[pl.BlockSpec((tm, d), lambda i: (i, 0)), mod_spec, mod_spec, pl.BlockSpec((d, n_pad), lambda i: (0, 0))],
        out_specs=[pl.BlockSpec((tm, wd), lambda i: (i, 0)) for wd in widths],
        out_shape=[jax.ShapeDtypeStruct((t, wd), F32) for wd in widths],
        compiler_params=pltpu.CompilerParams(dimension_semantics=("arbitrary",),
                                             vmem_limit_bytes=VMEM_LIMIT_BYTES),
        name="modulated_in_proj",
    )(x, sh, sc, w_pad)


def _even_out_kernel(oa_ref, ra_ref, yb_ref, rb_ref, kb_ref, vb_ref, gate_ref, gn_ref, lng_ref, lnb_ref, rk_ref,
                     w_ref, x_ref, g_ref, o_ref):
    pieces = []
    for h in range(GLA_HEADS):
        sl = slice(h * GLA_DV, (h + 1) * GLA_DV)
        o = oa_ref[:, sl]
        r = ra_ref[:, sl]
        o = o * lax.rsqrt(jnp.mean(o * o, axis=-1, keepdims=True) + EPS) * gn_ref[:, sl]
        pieces.append(o * (r * (1.0 / (1.0 + jnp.exp(-r)))))
    for h in range(RW_HEADS):
        sl = slice(h * RW_DH, (h + 1) * RW_DH)
        y = yb_ref[:, sl]
        mu = jnp.mean(y, axis=-1, keepdims=True)
        yc = y - mu
        var = jnp.mean(yc * yc, axis=-1, keepdims=True)
        y = yc * lax.rsqrt(var + RW_LN_EPS) * lng_ref[:, sl] + lnb_ref[:, sl]
        bonus = jnp.sum(rb_ref[:, sl] * kb_ref[:, sl] * rk_ref[:, sl], axis=-1, keepdims=True)
        pieces.append((y + bonus * vb_ref[:, sl]) * gate_ref[:, sl])
    mix = jnp.concatenate(pieces, axis=1).astype(BF16)
    o_ref[...] = x_ref[...] + g_ref[0] * jnp.dot(mix, w_ref[...], preferred_element_type=F32)


def even_out(o_a, r_a, y_b, r_b, k_b, v_b, gate_b, gla_norm_g, ln_g, ln_b, r_k, w, x, gate):
    t, d = x.shape
    tm = min(ROW_TILE, t)
    g, mod_spec = _mod_operand(gate, t, tm)
    seq = pl.BlockSpec((tm, GLA_V), lambda i: (i, 0))
    par = pl.BlockSpec((1, GLA_V), lambda i: (0, 0))
    assert GLA_V == RW_W
    return pl.pallas_call(
        _even_out_kernel,
        grid=(t // tm,),
        in_specs=[seq] * 7 + [par] * 4 + [pl.BlockSpec((GLA_V + RW_W, d), lambda i: (0, 0)),
                                         pl.BlockSpec((tm, d), lambda i: (i, 0)), mod_spec],
        out_specs=pl.BlockSpec((tm, d), lambda i: (i, 0)),
        out_shape=jax.ShapeDtypeStruct((t, d), F32),
        compiler_params=pltpu.CompilerParams(dimension_semantics=("arbitrary",),
                                             vmem_limit_bytes=VMEM_LIMIT_BYTES),
        name="even_out_proj_residual",
    )(o_a, r_a, y_b, r_b, k_b, v_b, gate_b, jnp.tile(gla_norm_g, GLA_HEADS)[None], ln_g[None], ln_b[None],
      r_k.reshape(1, RW_W), w.astype(BF16), x, g)


def _mm_res_kernel(a_ref, w_ref, x_ref, g_ref, o_ref):
    y = jnp.dot(a_ref[...].astype(BF16), w_ref[...], preferred_element_type=F32)
    o_ref[...] = x_ref[...] + g_ref[0] * y


def mm_res(a, w, x, gate):
    t, k = a.shape
    d = w.shape[1]
    tm = min(ROW_TILE, t)
    g, mod_spec = _mod_operand(gate, t, tm)
    return pl.pallas_call(
        _mm_res_kernel,
        grid=(t // tm,),
        in_specs=[pl.BlockSpec((tm, k), lambda i: (i, 0)), pl.BlockSpec((k, d), lambda i: (0, 0)),
                  pl.BlockSpec((tm, d), lambda i: (i, 0)), mod_spec],
        out_specs=pl.BlockSpec((tm, d), lambda i: (i, 0)),
        out_shape=jax.ShapeDtypeStruct((t, d), F32),
        compiler_params=pltpu.CompilerParams(dimension_semantics=("arbitrary",),
                                             vmem_limit_bytes=VMEM_LIMIT_BYTES),
        name="out_proj_residual",
    )(a, w.astype(BF16), x, g)


def _ffn_kernel(routed, x_ref, sh_ref, sc_ref, gt_ref, wr_ref, wg_ref, wu_ref, wd_ref, o_ref, h_scr, cmb_scr, acc_scr):
    e = pl.program_id(1)
    lane = lax.broadcasted_iota(jnp.int32, cmb_scr.shape, 1)

    @pl.when(e == 0)
    def _():
        hb = _modulated(x_ref[...], sh_ref[0], sc_ref[0]).astype(BF16)
        h_scr[...] = hb
        acc_scr[...] = jnp.zeros(acc_scr.shape, F32)
        if routed:
            logits = jnp.dot(hb, wr_ref[...], preferred_element_type=F32)
            lg = jnp.where(lane < N_EXPERTS, logits, -jnp.inf)
            m1 = jnp.max(lg, axis=1, keepdims=True)
            i1 = jnp.min(jnp.where(lg == m1, lane, LANES), axis=1, keepdims=True)
            lg2 = jnp.where(lane == i1, -jnp.inf, lg)
            m2 = jnp.max(lg2, axis=1, keepdims=True)
            i2 = jnp.min(jnp.where(lg2 == m2, lane, LANES), axis=1, keepdims=True)
            t = jnp.exp(m2 - m1)
            p1 = 1.0 / (1.0 + t)
            cmb_scr[...] = jnp.where(lane == i1, p1, 0.0) + jnp.where(lane == i2, t * p1, 0.0)

    hb = h_scr[...]
    g = jnp.dot(hb, wg_ref[0], preferred_element_type=F32)
    u = jnp.dot(hb, wu_ref[0], preferred_element_type=F32)
    act = g * (1.0 / (1.0 + jnp.exp(-g))) * u
    y = jnp.dot(act.astype(BF16), wd_ref[0], preferred_element_type=F32)
    if routed:
        y = y * jnp.sum(jnp.where(lane == e, cmb_scr[...], 0.0), axis=1, keepdims=True)
    acc_scr[...] += y

    @pl.when(e == pl.num_programs(1) - 1)
    def _():
        o_ref[...] = x_ref[...] + gt_ref[0] * acc_scr[...]


FFN_BLOCK = 1408


def ffn_block(x, shift, scale, gate, w_router, w_gate, w_up, w_down):
    t, d = x.shape
    routed = w_router is not None
    tm = min(ROW_TILE, t)
    sh, mod_spec = _mod_operand(shift, t, tm)
    sc, _ = _mod_operand(scale, t, tm)
    gt, _ = _mod_operand(gate, t, tm)
    if routed:
        n_e, _, f = w_gate.shape
        assert f == FFN_BLOCK and n_e == N_EXPERTS
        wr = jnp.pad(w_router, ((0, 0), (0, LANES - n_e))).astype(BF16)
        up_spec = pl.BlockSpec((1, d, f), lambda i, e: (e, 0, 0))
        down_spec = pl.BlockSpec((1, f, d), lambda i, e: (e, 0, 0))
    else:
        f_all = w_gate.shape[1]
        assert f_all % FFN_BLOCK == 0
        n_e, f = f_all // FFN_BLOCK, FFN_BLOCK
        wr = jnp.zeros((d, LANES), BF16)
        w_gate, w_up, w_down = w_gate[None], w_up[None], w_down[None]
        up_spec = pl.BlockSpec((1, d, f), lambda i, e: (0, 0, e))
        down_spec = pl.BlockSpec((1, f, d), lambda i, e: (0, e, 0))
    return pl.pallas_call(
        functools.partial(_ffn_kernel, routed),
        grid=(t // tm, n_e),
        in_specs=[pl.BlockSpec((tm, d), lambda i, e: (i, 0)), mod_spec, mod_spec, mod_spec,
                  pl.BlockSpec((d, LANES), lambda i, e: (0, 0)), up_spec, up_spec, down_spec],
        out_specs=pl.BlockSpec((tm, d), lambda i, e: (i, 0)),
        out_shape=jax.ShapeDtypeStruct((t, d), F32),
        scratch_shapes=[pltpu.VMEM((tm, d), BF16), pltpu.VMEM((tm, LANES), F32), pltpu.VMEM((tm, d), F32)],
        compiler_params=pltpu.CompilerParams(dimension_semantics=("arbitrary", "arbitrary"),
                                             vmem_limit_bytes=VMEM_LIMIT_BYTES),
        name="routed_experts" if routed else "dense_swiglu",
    )(x, sh, sc, gt, wr, w_gate.astype(BF16), w_up.astype(BF16), w_down.astype(BF16))


RW_CHUNK = 64


def _bdot(a, b):
    return jnp.dot(a.astype(BF16), b.astype(BF16), preferred_element_type=F32)


def _bdot_nt(a, b):
    return lax.dot_general(a.astype(BF16), b.astype(BF16), (((1,), (1,)), ((), ())), preferred_element_type=F32)


def _bdot_tn(a, b):
    return lax.dot_general(a.astype(BF16), b.astype(BF16), (((0,), (0,)), ((), ())), preferred_element_type=F32)


def _split3(x):
    hi = x.astype(BF16)
    r1 = x - hi.astype(F32)
    mid = r1.astype(BF16)
    lo = (r1 - mid.astype(F32)).astype(BF16)
    return hi, mid, lo


def _rwkv_chunk_heads(r, lw, k, v, a, b, h0, tri_bf, strict, incl, eye):
    nh = len(r)
    hs = range(nh)
    cum = []
    for i in hs:
        hi, mid, lo = _split3(lw[i])
        cum.append(jnp.dot(tri_bf, hi, preferred_element_type=F32) + jnp.dot(tri_bf, mid, preferred_element_type=F32)
                   + jnp.dot(tri_bf, lo, preferred_element_type=F32))
    cum_last = [cum[i][RW_CHUNK - 1:RW_CHUNK, :] for i in hs]
    e_neg = [jnp.exp(-cum[i]) for i in hs]
    e_end = [jnp.exp(cum_last[i] - cum[i]) for i in hs]
    a_t = [a[i] * jnp.exp(cum[i] - lw[i]) for i in hs]
    r_t = [r[i] * jnp.exp(cum[i]) for i in hs]
    k_t = [k[i] * e_neg[i] for i in hs]
    b_t = [b[i] * e_neg[i] for i in hs]
    k_e = [k[i] * e_end[i] for i in hs]
    b_e = [b[i] * e_end[i] for i in hs]
    a_ab = [jnp.where(strict, _bdot_nt(a_t[i], b_t[i]), 0.0) for i in hs]
    a_ak = [jnp.where(strict, _bdot_nt(a_t[i], k_t[i]), 0.0) for i in hs]
    a_rb = [jnp.where(incl, _bdot_nt(r_t[i], b_t[i]), 0.0) for i in hs]
    a_rk = [jnp.where(incl, _bdot_nt(r_t[i], k_t[i]), 0.0) for i in hs]
    x = [eye + a_ab[i] for i in hs]
    y = list(a_ab)
    for _ in range(int(math.log2(RW_CHUNK)) - 1):
        y = [_bdot(y[i], y[i]) for i in hs]
        x = [x[i] + _bdot(x[i], y[i]) for i in hs]
    a_p = [_bdot(x[i], a_t[i]) for i in hs]
    akv = [_bdot(a_ak[i], v[i]) for i in hs]
    v_p = [_bdot(x[i], akv[i]) for i in hs]
    r_p = [r_t[i] + _bdot(a_rb[i], a_p[i]) for i in hs]
    y_p = [_bdot(a_rk[i], v[i]) + _bdot(a_rb[i], v_p[i]) for i in hs]
    m_lr = [_bdot_tn(b_e[i], a_p[i]) for i in hs]
    g = [_bdot_tn(k_e[i], v[i]) + _bdot_tn(b_e[i], v_p[i]) for i in hs]
    w_col = [jnp.sum(jnp.where(eye, jnp.exp(cum_last[i]), 0.0), axis=1, keepdims=True) for i in hs]
    y_out = [_bdot(r_p[i], h0[i]) + y_p[i] for i in hs]
    h_new = [w_col[i] * h0[i] + _bdot(m_lr[i], h0[i]) + g[i] for i in hs]
    return y_out, h_new


RW_HEAD_GROUP = 16


def _rwkv_kernel(r_ref, lw_ref, k_ref, v_ref, a_ref, b_ref, h0_ref, y_ref, hout_ref, h_scr):
    c = pl.program_id(0)

    @pl.when(c == 0)
    def _():
        h_scr[...] = h0_ref[...]

    rows = lax.broadcasted_iota(jnp.int32, (RW_CHUNK, RW_CHUNK), 0)
    cols = lax.broadcasted_iota(jnp.int32, (RW_CHUNK, RW_CHUNK), 1)
    strict = rows > cols
    incl = rows >= cols
    eye = rows == cols
    tri_bf = jnp.where(incl, 1.0, 0.0).astype(BF16)
    n_b, _, width = r_ref.shape
    n = h_scr.shape[-1]
    heads = [(b, h) for b in range(n_b) for h in range(width // n)]
    for i0 in range(0, len(heads), RW_HEAD_GROUP):
        grp = heads[i0:i0 + RW_HEAD_GROUP]
        seqs = [[ref[b, :, h * n:(h + 1) * n] for b, h in grp] for ref in (r_ref, lw_ref, k_ref, v_ref, a_ref, b_ref)]
        states = [h_scr[b * (width // n) + h] for b, h in grp]
        y_out, h_new = _rwkv_chunk_heads(*seqs, states, tri_bf, strict, incl, eye)
        for j, (b, h) in enumerate(grp):
            y_ref[b, :, h * n:(h + 1) * n] = y_out[j]
            h_scr[b * (width // n) + h] = h_new[j]

    @pl.when(c == pl.num_programs(0) - 1)
    def _():
        hout_ref[...] = h_scr[...]


def rwkv7_chunked(r, lw, k, v, a, b, s0):
    bt, L, H, N = r.shape
    assert L % RW_CHUNK == 0
    flat = lambda t: t.reshape(bt, L, H * N)
    h0 = s0.transpose(0, 1, 3, 2).reshape(bt * H, N, N)
    seq_spec = pl.BlockSpec((bt, RW_CHUNK, H * N), lambda c: (0, c, 0))
    st_spec = pl.BlockSpec((bt * H, N, N), lambda c: (0, 0, 0))
    y, h_fin = pl.pallas_call(
        _rwkv_kernel,
        grid=(L // RW_CHUNK,),
        in_specs=[seq_spec] * 6 + [st_spec],
        out_specs=[seq_spec, st_spec],
        out_shape=[jax.ShapeDtypeStruct((bt, L, H * N), F32), jax.ShapeDtypeStruct((bt * H, N, N), F32)],
        scratch_shapes=[pltpu.VMEM((bt * H, N, N), F32)],
        compiler_params=pltpu.CompilerParams(dimension_semantics=("arbitrary",),
                                             vmem_limit_bytes=VMEM_LIMIT_BYTES),
        name="rwkv7_chunked",
    )(flat(r), flat(lw), flat(k), flat(v), flat(a), flat(b), h0)
    return y.reshape(bt, L, H, N), h_fin.reshape(bt, H, N, N).transpose(0, 1, 3, 2)


def _gla_kernel(q_ref, k_ref, v_ref, g_ref, s0_ref, o_ref, sout_ref, s_scr):
    c = pl.program_id(0)

    @pl.when(c == 0)
    def _():
        s_scr[...] = s0_ref[...]

    rows = lax.broadcasted_iota(jnp.int32, (GLA_CHUNK, GLA_CHUNK), 0)
    cols = lax.broadcasted_iota(jnp.int32, (GLA_CHUNK, GLA_CHUNK), 1)
    incl = rows >= cols
    eye = lax.broadcasted_iota(jnp.int32, (GLA_DK, GLA_DK), 0) == lax.broadcasted_iota(jnp.int32, (GLA_DK, GLA_DK), 1)
    tri_bf = jnp.where(incl, 1.0, 0.0).astype(BF16)
    n_b, _, qk_width = q_ref.shape
    n_h = qk_width // GLA_DK
    heads = [(b, h) for b in range(n_b) for h in range(n_h)]
    hs = range(len(heads))
    qs = [q_ref[b, :, h * GLA_DK:(h + 1) * GLA_DK] for b, h in heads]
    ks = [k_ref[b, :, h * GLA_DK:(h + 1) * GLA_DK] for b, h in heads]
    vs = [v_ref[b, :, h * GLA_DV:(h + 1) * GLA_DV] for b, h in heads]
    bcum = []
    for b, h in heads:
        hi, mid, lo = _split3(g_ref[b, :, h * GLA_DK:(h + 1) * GLA_DK])
        bcum.append(jnp.dot(tri_bf, hi, preferred_element_type=F32) + jnp.dot(tri_bf, mid, preferred_element_type=F32)
                    + jnp.dot(tri_bf, lo, preferred_element_type=F32))
    blast = [bcum[i][GLA_CHUNK - 1:GLA_CHUNK, :] for i in hs]
    q_e = [qs[i] * jnp.exp(bcum[i]) for i in hs]
    k_e = [ks[i] * jnp.exp(-bcum[i]) for i in hs]
    k_end = [ks[i] * jnp.exp(blast[i] - bcum[i]) for i in hs]
    att = [jnp.where(incl, _bdot_nt(q_e[i], k_e[i]), 0.0) for i in hs]
    s_old = [s_scr[i] for i in hs]
    o = [_bdot(att[i], vs[i]) + _bdot(q_e[i], s_old[i]) for i in hs]
    kv = [_bdot_tn(k_end[i], vs[i]) for i in hs]
    dec = [jnp.sum(jnp.where(eye, jnp.exp(blast[i]), 0.0), axis=1, keepdims=True) for i in hs]
    for i, (b, h) in enumerate(heads):
        o_ref[b, :, h * GLA_DV:(h + 1) * GLA_DV] = o[i]
        s_scr[i] = dec[i] * s_old[i] + kv[i]

    @pl.when(c == pl.num_programs(0) - 1)
    def _():
        sout_ref[...] = s_scr[...]


def gla_chunked_pallas(q, k, v, log_a, s0):
    bt, L, H, dk = q.shape
    dv = v.shape[-1]
    assert L % GLA_CHUNK == 0
    flat = lambda t: t.reshape(bt, L, H * t.shape[-1])
    qk_spec = pl.BlockSpec((bt, GLA_CHUNK, H * dk), lambda c: (0, c, 0))
    v_spec = pl.BlockSpec((bt, GLA_CHUNK, H * dv), lambda c: (0, c, 0))
    st_spec = pl.BlockSpec((bt * H, dk, dv), lambda c: (0, 0, 0))
    o, s_fin = pl.pallas_call(
        _gla_kernel,
        grid=(L // GLA_CHUNK,),
        in_specs=[qk_spec, qk_spec, v_spec, qk_spec, st_spec],
        out_specs=[v_spec, st_spec],
        out_shape=[jax.ShapeDtypeStruct((bt, L, H * dv), F32), jax.ShapeDtypeStruct((bt * H, dk, dv), F32)],
        scratch_shapes=[pltpu.VMEM((bt * H, dk, dv), F32)],
        compiler_params=pltpu.CompilerParams(dimension_semantics=("arbitrary",),
                                             vmem_limit_bytes=VMEM_LIMIT_BYTES),
        name="gla_chunked",
    )(flat(q), flat(k), flat(v), flat(log_a), s0.reshape(bt * H, dk, dv))
    return o.reshape(bt, L, H, dv), s_fin.reshape(bt, H, dk, dv)


DSA_QB = 128
DSA_KC = 512
MASK_NEG = -1e30
INT_MIN = -2 ** 31
INT16_MIN = -2 ** 15
ONE16 = np.int16(1)
ZERO16 = np.int16(0)
ONE = np.float32(1.0)
ZERO = np.float32(0.0)
G_PER_KV = ATT_HEADS // ATT_KV_HEADS


def _sortable_key(x):
    bits = lax.bitcast_convert_type(x + 0.0, jnp.int32)
    return bits ^ ((bits >> 31) & 0x7FFFFFFF)


def _dsa_kernel(topk, q_ref, qi_ref, wi_ref, k_ref, v_ref, kit_ref, upper_ref, o_ref,
                wb_scr, key_scr, khi_scr, klo_scr, m_scr, acc_scr):
    i = pl.program_id(1)
    n_chunks = (i * DSA_QB) // DSA_KC + 1
    nt = DSA_KC // LANES
    q_pos = i * DSA_QB + lax.broadcasted_iota(jnp.int32, (DSA_QB, LANES), 0)
    lane = lax.broadcasted_iota(jnp.int32, (DSA_QB, LANES), 1)

    for h in range(IDX_HEADS):
        wb_scr[h] = jnp.broadcast_to(wi_ref[0, :, h:h + 1], (DSA_QB, LANES))

    def score_chunk(c, carry):
        dots = jnp.dot(qi_ref[0, 0], kit_ref[0, c], preferred_element_type=F32)
        for jt in range(nt):
            sc = jnp.zeros((DSA_QB, LANES), F32)
            for h in range(IDX_HEADS):
                d = dots[h * DSA_QB:(h + 1) * DSA_QB, jt * LANES:(jt + 1) * LANES]
                sc = sc + jnp.maximum(d, 0.0) * wb_scr[h]
            k_pos = c * DSA_KC + jt * LANES + lane
            sc = jnp.where(k_pos <= q_pos, sc, -jnp.inf)
            key = _sortable_key(sc)
            key_scr[c, :, jt * LANES:(jt + 1) * LANES] = key
            khi_scr[c, :, jt * LANES:(jt + 1) * LANES] = (key >> 16).astype(jnp.int16)
        return carry

    lax.fori_loop(0, n_chunks, score_chunk, 0)

    def count16(scr, cand, strict):
        cand_b = jnp.broadcast_to(cand.astype(jnp.int16), (DSA_QB, LANES))

        def body(c, acc):
            for jt in range(nt):
                tile = scr[c, :, jt * LANES:(jt + 1) * LANES]
                hit = (tile > cand_b) if strict else (tile >= cand_b)
                acc = acc + jnp.where(hit, ONE16, ZERO16)
            return acc

        acc = lax.fori_loop(0, n_chunks, body, jnp.zeros((DSA_QB, LANES), jnp.int16))
        return jnp.sum(acc.astype(F32), axis=1, keepdims=True)

    def bisect16(scr, k_need):
        cur = jnp.where(count16(scr, jnp.zeros((DSA_QB, 1), jnp.int32), False) >= k_need, 0, INT16_MIN)
        cur = cur.astype(jnp.int32)

        def bit_step(p, cur):
            cand = cur | (jnp.int32(1) << (14 - p))
            return jnp.where(count16(scr, cand, False) >= k_need, cand, cur)

        return lax.fori_loop(0, 15, bit_step, cur)

    kf = float(topk)
    thr_hi = bisect16(khi_scr, kf)
    above = count16(khi_scr, thr_hi, True)
    thr_hi_b = jnp.broadcast_to(thr_hi, (DSA_QB, LANES))

    def low_halves(c, carry):
        for jt in range(nt):
            key = key_scr[c, :, jt * LANES:(jt + 1) * LANES]
            lo = (key & 0xFFFF) - 32768
            klo_scr[c, :, jt * LANES:(jt + 1) * LANES] = jnp.where((key >> 16) == thr_hi_b, lo, INT16_MIN).astype(jnp.int16)
        return carry

    lax.fori_loop(0, n_chunks, low_halves, 0)
    thr_lo = bisect16(klo_scr, kf - above)
    n_gt = above + count16(klo_scr, thr_lo, True)
    need = kf - n_gt
    thr = (thr_hi << 16) | (thr_lo + 32768)

    m_scr[...] = jnp.full(m_scr.shape, MASK_NEG, F32)
    acc_scr[...] = jnp.zeros(acc_scr.shape, F32)

    def attend_chunk(c, tie_carry):
        k0 = pl.multiple_of(c * DSA_KC, DSA_KC)
        keys = key_scr[c]
        thr_c = jnp.broadcast_to(thr, (DSA_QB, DSA_KC))
        eq = keys == thr_c
        eq_f = jnp.where(eq, ONE, ZERO)
        rank = tie_carry + jnp.dot(eq_f.astype(BF16), upper_ref[...], preferred_element_type=F32)
        k_pos = k0 + lax.broadcasted_iota(jnp.int32, (DSA_QB, DSA_KC), 1)
        qp = i * DSA_QB + lax.broadcasted_iota(jnp.int32, (DSA_QB, DSA_KC), 0)
        sel = jnp.where(keys > thr_c, ONE, jnp.where(rank < need, eq_f, ZERO))
        bias = jnp.where((sel > 0.5) & (k_pos <= qp), ZERO, np.float32(MASK_NEG))
        v_c = v_ref[0, pl.ds(k0, DSA_KC), :]
        rows = G_PER_KV * DSA_QB
        for n in range(ATT_KV_HEADS):
            s = lax.dot_general(q_ref[0, 0, n], k_ref[0, n, pl.ds(k0, DSA_KC), :], (((1,), (1,)), ((), ())),
                                preferred_element_type=F32)
            s = (s.reshape(G_PER_KV, DSA_QB, DSA_KC) + bias[None]).reshape(rows, DSA_KC)
            tiles = [s[:, jt * LANES:(jt + 1) * LANES] for jt in range(nt)]
            tile_max = functools.reduce(jnp.maximum, tiles)
            m_old = m_scr[n]
            m_new = jnp.maximum(m_old, jnp.max(tile_max, axis=1, keepdims=True))
            alpha = jnp.exp(m_old - m_new)
            p = jnp.concatenate([jnp.exp((t - m_new).astype(BF16)) for t in tiles], axis=1)
            pv = jnp.dot(p, v_c[:, n * LANES:(n + 1) * LANES], preferred_element_type=F32)
            acc_scr[n] = alpha * acc_scr[n] + pv
            m_scr[n] = m_new
        return tie_carry + jnp.sum(eq_f, axis=1, keepdims=True)

    lax.fori_loop(0, n_chunks, attend_chunk, jnp.zeros((DSA_QB, 1), F32))

    for n in range(ATT_KV_HEADS):
        acc = acc_scr[n]
        o_n = acc[:, :ATT_DH] / acc[:, ATT_DH:]
        for g in range(G_PER_KV):
            hd = n * G_PER_KV + g
            o_ref[0, :, hd * ATT_DH:(hd + 1) * ATT_DH] = o_n[g * DSA_QB:(g + 1) * DSA_QB, :]


def dsa_prompt_attention(q, k, v, qi, ki, wi):
    bt, L = q.shape[:2]
    assert L % DSA_KC == 0 and L // 4 >= 1
    topk = min(TOPK_MAX, L // 4)
    nb = L // DSA_QB
    nc = L // DSA_KC
    kit = ki.astype(BF16).reshape(bt, nc, DSA_KC, IDX_DH).transpose(0, 1, 3, 2)
    upper = jnp.triu(jnp.ones((DSA_KC, DSA_KC), BF16), 1)
    rows = G_PER_KV * DSA_QB
    q_blocks = (q * ATT_DH ** -0.5).astype(BF16).reshape(bt, nb, DSA_QB, ATT_KV_HEADS, G_PER_KV, ATT_DH)
    q_blocks = q_blocks.transpose(0, 1, 3, 4, 2, 5).reshape(bt, nb, ATT_KV_HEADS, rows, ATT_DH)
    qi_blocks = qi.astype(BF16).reshape(bt, nb, DSA_QB, IDX_HEADS, IDX_DH).transpose(0, 1, 3, 2, 4)
    qi_blocks = qi_blocks.reshape(bt, nb, IDX_HEADS * DSA_QB, IDX_DH)
    v_aug = jnp.concatenate([v.astype(BF16), jnp.ones(v.shape, BF16)], axis=-1).reshape(bt, L, ATT_KV_HEADS * LANES)
    return pl.pallas_call(
        functools.partial(_dsa_kernel, topk),
        grid=(bt, nb),
        in_specs=[
            pl.BlockSpec((1, 1, ATT_KV_HEADS, rows, ATT_DH), lambda b, i: (b, i, 0, 0, 0)),
            pl.BlockSpec((1, 1, IDX_HEADS * DSA_QB, IDX_DH), lambda b, i: (b, i, 0, 0)),
            pl.BlockSpec((1, DSA_QB, IDX_HEADS), lambda b, i: (b, i, 0)),
            pl.BlockSpec((1, ATT_KV_HEADS, L, ATT_DH), lambda b, i: (b, 0, 0, 0)),
            pl.BlockSpec((1, L, ATT_KV_HEADS * LANES), lambda b, i: (b, 0, 0)),
            pl.BlockSpec((1, nc, IDX_DH, DSA_KC), lambda b, i: (b, 0, 0, 0)),
            pl.BlockSpec((DSA_KC, DSA_KC), lambda b, i: (0, 0)),
        ],
        out_specs=pl.BlockSpec((1, DSA_QB, ATT_Q), lambda b, i: (b, i, 0)),
        out_shape=jax.ShapeDtypeStruct((bt, L, ATT_Q), F32),
        scratch_shapes=[
            pltpu.VMEM((IDX_HEADS, DSA_QB, LANES), F32),
            pltpu.VMEM((nc, DSA_QB, DSA_KC), jnp.int32),
            pltpu.VMEM((nc, DSA_QB, DSA_KC), jnp.int16),
            pltpu.VMEM((nc, DSA_QB, DSA_KC), jnp.int16),
            pltpu.VMEM((ATT_KV_HEADS, rows, LANES), F32),
            pltpu.VMEM((ATT_KV_HEADS, rows, LANES), F32),
        ],
        compiler_params=pltpu.CompilerParams(dimension_semantics=("arbitrary", "arbitrary"),
                                             vmem_limit_bytes=VMEM_LIMIT_BYTES),
        name="dsa_prompt_attention",
    )(q_blocks, qi_blocks, wi,
      k.astype(BF16).transpose(0, 2, 1, 3), v_aug, kit, upper)


NEW_PAD = 16
PAGES_PER_STEP = 4


def _dsa_sample_kernel(topk, n_new, pt_ref, qbd_ref, qi_ref, wrep_ref, knew_ref, vnew_ref, kinew_ref, *rest):
    pages = rest[:3 * PAGES_PER_STEP]
    upper_ref, o_ref, kbuf, vbuf, kibuf = rest[3 * PAGES_PER_STEP:]
    p = pl.program_id(1)
    n_steps = pl.num_programs(1)
    past = n_steps * PAGES_PER_STEP * PAGE_SIZE
    n_keys = kbuf.shape[0]
    nq = qbd_ref.shape[1] // ATT_HEADS
    for buf, refs in zip((kbuf, vbuf, kibuf), (pages[:PAGES_PER_STEP], pages[PAGES_PER_STEP:2 * PAGES_PER_STEP],
                                               pages[2 * PAGES_PER_STEP:])):
        for s, ref in enumerate(refs):
            row0 = pl.multiple_of((p * PAGES_PER_STEP + s) * PAGE_SIZE, PAGE_SIZE)
            buf[pl.ds(row0, PAGE_SIZE), :] = ref[0, 0].astype(BF16)

    @pl.when(p == n_steps - 1)
    def _():
        tail = n_keys - past
        kbuf[past:, :] = jnp.zeros((tail, ATT_KV), BF16)
        vbuf[past:, :] = jnp.zeros((tail, ATT_KV), BF16)
        kibuf[past:, :] = jnp.zeros((tail, IDX_DH), BF16)
        kbuf[past:past + NEW_PAD, :] = knew_ref[0].astype(BF16)
        vbuf[past:past + NEW_PAD, :] = vnew_ref[0].astype(BF16)
        kibuf[past:past + NEW_PAD, :] = kinew_ref[0].astype(BF16)

        dots = lax.dot_general(qi_ref[0], kibuf[...], (((1,), (1,)), ((), ())), preferred_element_type=F32)
        k_idx = lax.broadcasted_iota(jnp.int32, (8, n_keys), 1)
        q_row = lax.broadcasted_iota(jnp.int32, (8, n_keys), 0)
        valid = (k_idx - past <= q_row) & (q_row < nq)
        rows = []
        for qn in range(nq):
            d = jnp.maximum(dots[qn * IDX_HEADS:(qn + 1) * IDX_HEADS], 0.0)
            w = wrep_ref[0, qn * IDX_HEADS:(qn + 1) * IDX_HEADS, :]
            parts = []
            for jt in range(n_keys // LANES):
                parts.append(jnp.sum(d[:, jt * LANES:(jt + 1) * LANES] * w, axis=0, keepdims=True))
            rows.append(jnp.concatenate(parts, axis=1))
        rows.append(jnp.zeros((8 - nq, n_keys), F32))
        sc = jnp.where(valid, jnp.concatenate(rows, axis=0), -jnp.inf)
        keys = _sortable_key(sc)

        kf = float(topk)

        def count_ge(cand):
            return jnp.sum(jnp.where(keys >= cand, ONE, ZERO), axis=1, keepdims=True)

        cur = jnp.where(count_ge(jnp.zeros((8, 1), jnp.int32)) >= kf, 0, INT_MIN).astype(jnp.int32)

        def bit_step(i, cur):
            cand = cur | (jnp.int32(1) << (30 - i))
            return jnp.where(count_ge(cand) >= kf, cand, cur)

        thr = lax.fori_loop(0, 31, bit_step, cur)
        n_gt = jnp.sum(jnp.where(keys > thr, ONE, ZERO), axis=1, keepdims=True)
        need = kf - n_gt
        eq_f = jnp.where(keys == thr, ONE, ZERO)
        carry = jnp.zeros((8, 1), F32)
        ranks = []
        for jt in range(n_keys // LANES):
            e = eq_f[:, jt * LANES:(jt + 1) * LANES]
            ranks.append(carry + jnp.dot(e.astype(BF16), upper_ref[...], preferred_element_type=F32))
            carry = carry + jnp.sum(e, axis=1, keepdims=True)
        rank = jnp.concatenate(ranks, axis=1)
        sel = jnp.where(keys > thr, ONE, jnp.where(rank < need, eq_f, ZERO))
        bias = jnp.where((sel > 0.5) & valid, ZERO, np.float32(MASK_NEG))

        s = lax.dot_general(qbd_ref[0], kbuf[...], (((1,), (1,)), ((), ())), preferred_element_type=F32)
        s = jnp.concatenate([s[qn * ATT_HEADS:(qn + 1) * ATT_HEADS] + bias[qn:qn + 1] for qn in range(nq)], axis=0)
        m = jnp.max(s, axis=1, keepdims=True)
        pr = jnp.exp(s - m)
        l = jnp.sum(pr, axis=1, keepdims=True)
        o_ref[0] = jnp.dot(pr.astype(BF16), vbuf[...], preferred_element_type=F32) / l


def dsa_sample_attention(q, k_new, v_new, qi, ki_new, wi, cache_k, cache_v, cache_kidx, page_table, j):
    bt, nq = q.shape[:2]
    n_pages = page_table.shape[1]
    past = n_pages * PAGE_SIZE
    assert nq <= 8 and nq <= NEW_PAD
    topk = min(TOPK_MAX, (past + nq) // 4)
    n_keys = past + LANES
    n_pool = cache_k.shape[1]
    qs = (q * ATT_DH ** -0.5).astype(BF16).reshape(bt, nq, ATT_KV_HEADS, G_PER_KV, 1, ATT_DH)
    eye = jnp.eye(ATT_KV_HEADS, dtype=BF16)[None, None, :, None, :, None]
    qbd = (qs * eye).reshape(bt, nq * ATT_HEADS, ATT_KV)
    qi2 = qi.astype(BF16).reshape(bt, nq * IDX_HEADS, IDX_DH)
    wrep = jnp.broadcast_to(wi.reshape(bt, nq * IDX_HEADS, 1), (bt, nq * IDX_HEADS, LANES))
    pad = lambda t: jnp.pad(t.reshape(bt, nq, -1), ((0, 0), (0, NEW_PAD - nq), (0, 0)))
    upper = jnp.triu(jnp.ones((LANES, LANES), BF16), 1)
    per_b = lambda shape: pl.BlockSpec((1,) + shape, lambda b, p, pt: (b, 0, 0))
    assert n_pages % PAGES_PER_STEP == 0

    def pages(width):
        return [pl.BlockSpec((1, 1, PAGE_SIZE, width), lambda b, p, pt, s=s: (j, pt[b, p * PAGES_PER_STEP + s], 0, 0))
                for s in range(PAGES_PER_STEP)]

    o_all = pl.pallas_call(
        functools.partial(_dsa_sample_kernel, topk, nq),
        grid_spec=pltpu.PrefetchScalarGridSpec(
            num_scalar_prefetch=1,
            grid=(bt, n_pages // PAGES_PER_STEP),
            in_specs=[per_b((nq * ATT_HEADS, ATT_KV)), per_b((nq * IDX_HEADS, IDX_DH)), per_b((nq * IDX_HEADS, LANES)),
                      per_b((NEW_PAD, ATT_KV)), per_b((NEW_PAD, ATT_KV)), per_b((NEW_PAD, IDX_DH))]
            + pages(ATT_KV) + pages(ATT_KV) + pages(IDX_DH)
            + [pl.BlockSpec((LANES, LANES), lambda b, p, pt: (0, 0))],
            out_specs=per_b((nq * ATT_HEADS, ATT_KV)),
            scratch_shapes=[pltpu.VMEM((n_keys, ATT_KV), BF16), pltpu.VMEM((n_keys, ATT_KV), BF16),
                            pltpu.VMEM((n_keys, IDX_DH), BF16)]),
        out_shape=jax.ShapeDtypeStruct((bt, nq * ATT_HEADS, ATT_KV), F32),
        compiler_params=pltpu.CompilerParams(dimension_semantics=("arbitrary", "arbitrary"),
                                             vmem_limit_bytes=VMEM_LIMIT_BYTES),
        name="dsa_sample_attention",
    )(page_table, qbd, qi2, wrep, pad(k_new), pad(v_new), pad(ki_new),
      *([cache_k.reshape(cache_k.shape[0], n_pool, PAGE_SIZE, ATT_KV)] * PAGES_PER_STEP),
      *([cache_v.reshape(cache_v.shape[0], n_pool, PAGE_SIZE, ATT_KV)] * PAGES_PER_STEP),
      *([cache_kidx] * PAGES_PER_STEP), upper)
    o6 = o_all.reshape(bt, nq, ATT_KV_HEADS, G_PER_KV, ATT_KV_HEADS, ATT_DH)
    o = jnp.stack([o6[:, :, n, :, n, :] for n in range(ATT_KV_HEADS)], axis=2)
    return o.reshape(bt, nq, ATT_Q)


def rms_norm(x, eps=EPS):
    return x * lax.rsqrt(jnp.mean(x * x, axis=-1, keepdims=True) + eps)


def layer_norm(x, g, b, eps):
    mu = jnp.mean(x, axis=-1, keepdims=True)
    var = jnp.mean(jnp.square(x - mu), axis=-1, keepdims=True)
    return (x - mu) * lax.rsqrt(var + eps) * g + b


def rotary(x, pos):
    half = x.shape[-1] // 2
    inv_freq = jnp.power(ROPE_THETA, -jnp.arange(half, dtype=F32) / half)
    ang = pos.astype(F32)[:, None] * inv_freq[None, :]
    cos = jnp.cos(ang)[None, :, None, :]
    sin = jnp.sin(ang)[None, :, None, :]
    x1, x2 = x[..., :half], x[..., half:]
    return jnp.concatenate([x1 * cos - x2 * sin, x2 * cos + x1 * sin], axis=-1)


def gla_chunked(q, k, v, log_a, s0):
    bt, L, H, _ = q.shape
    dv = v.shape[-1]
    C = min(GLA_CHUNK, L)
    n = -(-L // C)
    pad = n * C - L

    def prep(t):
        t = jnp.pad(t, ((0, 0), (0, pad), (0, 0), (0, 0)))
        return t.reshape(bt, n, C, H, t.shape[-1]).transpose(1, 0, 3, 2, 4)

    qc, kc, vc, gc = prep(q), prep(k), prep(v), prep(log_a)
    bcum = jnp.cumsum(gc, axis=3)
    blast = bcum[:, :, :, -1:, :]
    q_e = qc * jnp.exp(bcum)
    k_e = kc * jnp.exp(-bcum)
    k_end = kc * jnp.exp(blast - bcum)
    causal = jnp.tril(jnp.ones((C, C), dtype=bool))
    att = jnp.where(causal, jnp.einsum('nbhtd,nbhsd->nbhts', q_e, k_e), 0.0)
    o_intra = jnp.einsum('nbhts,nbhsv->nbhtv', att, vc)
    decay = jnp.exp(blast[:, :, :, 0, :])
    kv = jnp.einsum('nbhsd,nbhsv->nbhdv', k_end, vc)

    def step(s, inp):
        dec, kv_i = inp
        return dec[..., None] * s + kv_i, s

    s_fin, starts = s0, []
    for i in range(n):
        s_fin, s_i = step(s_fin, (decay[i], kv[i]))
        starts.append(s_i)
    s_start = jnp.stack(starts)
    o = o_intra + jnp.einsum('nbhtd,nbhdv->nbhtv', q_e, s_start)
    o = o.transpose(1, 0, 3, 2, 4).reshape(bt, n * C, H, dv)[:, :L]
    return o, s_fin


def rwkv7_scan(r, w, k, v, a, b, s0):
    xs = tuple(t.transpose(1, 0, 2, 3) for t in (r, w, k, v, a, b))

    def step(s, inp):
        r_t, w_t, k_t, v_t, a_t, b_t = inp
        sa = jnp.einsum('bhvk,bhk->bhv', s, a_t)
        s = s * w_t[:, :, None, :] + sa[..., None] * b_t[:, :, None, :] + v_t[..., None] * k_t[:, :, None, :]
        return s, jnp.einsum('bhvk,bhk->bhv', s, r_t)

    s, ys = s0, []
    for t in range(r.shape[1]):
        s, y_t = step(s, tuple(x[t] for x in xs))
        ys.append(y_t)
    return jnp.stack(ys, axis=1), s


EVEN_SPLIT = (GLA_QK, GLA_QK, GLA_V, GLA_V, GLA_GATE_RANK, 3 * RW_W + RW_W_LORA + RW_A_LORA + RW_G_LORA)
ODD_SPLIT = (ATT_Q, ATT_KV, ATT_KV, IDX_HEADS * IDX_DH, IDX_DH, IDX_HEADS)


def even_mixer(parts, bt, L, shift_prev, s_gla, s_rw, j, P):
    q_a, k_a, v_a, r_a, g_low, pb_raw = [t.reshape(bt, L, -1) for t in parts]
    ga = lambda t, d: t.reshape(bt, L, GLA_HEADS, d)
    log_alpha = jax.nn.log_sigmoid(g_low @ P['gla_w_gate'][j] + P['gla_b_gate'][j]) / GLA_TAU
    gla = gla_chunked_pallas if L % GLA_CHUNK == 0 else gla_chunked
    o_a, s_gla_new = gla(ga(q_a, GLA_DK) * GLA_DK ** -0.5, ga(k_a, GLA_DK), ga(v_a, GLA_DV),
                         ga(log_alpha, GLA_DK), s_gla)
    prev = jnp.concatenate([shift_prev[:, None, :], pb_raw[:, :-1]], axis=1)
    pb = pb_raw + (prev - pb_raw) * P['rw_mu'][j]
    o1 = 3 * RW_W + RW_W_LORA
    r_b, k_b, v_b, w_low, a_low, gt_low = jnp.split(pb, [RW_W, 2 * RW_W, 3 * RW_W, o1, o1 + RW_A_LORA], axis=-1)
    w_log = -jax.nn.softplus(-(P['rw_w0'][j] + jnp.tanh(w_low) @ P['rw_w_decay'][j])) - 0.5
    log_decay = -jnp.exp(w_log)
    iclr = jax.nn.sigmoid(P['rw_a0'][j] + a_low @ P['rw_w_iclr'][j])
    gate = jax.nn.sigmoid(gt_low) @ P['rw_w_gate'][j]
    hb = lambda t: t.reshape(bt, L, RW_HEADS, RW_DH)
    kk = hb(k_b * P['rw_k_k'][j])
    kk = kk / jnp.maximum(jnp.sqrt(jnp.sum(kk * kk, axis=-1, keepdims=True)), 1e-12)
    k_b = k_b * (1 + (iclr - 1) * P['rw_k_a'][j])
    r_h, k_h, v_h, a_h = hb(r_b), hb(k_b), hb(v_b), hb(iclr)
    if L % RW_CHUNK == 0:
        y_b, s_rw_new = rwkv7_chunked(r_h, hb(log_decay), k_h, v_h, -kk, kk * a_h, s_rw)
    else:
        y_b, s_rw_new = rwkv7_scan(r_h, hb(jnp.exp(log_decay)), k_h, v_h, -kk, kk * a_h, s_rw)
    flat = lambda t: t.reshape(bt * L, -1)
    return [flat(t) for t in (o_a, r_a, y_b, r_b, k_b, v_b, gate)], pb_raw[:, -1], s_gla_new, s_rw_new


def odd_qkv(parts, bt, L, pos, j, P):
    q, k, v, qi, ki, wi = [t.reshape(bt, L, -1) for t in parts]
    q = rotary(rms_norm(q.reshape(bt, L, ATT_HEADS, ATT_DH)) * P['q_norm_g'][j], pos)
    k = rotary(rms_norm(k.reshape(bt, L, ATT_KV_HEADS, ATT_DH)) * P['k_norm_g'][j], pos)
    v = v.reshape(bt, L, ATT_KV_HEADS, ATT_DH)
    qi = rotary(qi.reshape(bt, L, IDX_HEADS, IDX_DH), pos)
    ki = rotary(layer_norm(ki, P['kidx_ln_g'][j], P['kidx_ln_b'][j], EPS)[:, :, None, :], pos)[:, :, 0, :]
    wi = wi * (IDX_HEADS * IDX_DH) ** -0.5
    return q, k, v, qi, ki, wi


def trunk(x, mods, pos, shift0, gla0, rw0, attend, P):
    bt, L, D = x.shape
    x2 = x.reshape(bt * L, D)
    glas, rws, shifts, ks, vs, kis = [], [], [], [], [], []
    for layer, mod in enumerate(mods):
        sh1, sc1, g1, sh2, sc2, g2 = jnp.split(mod, 6, axis=-1)
        j = layer // 2
        if layer % 2 == 0:
            parts = mod_mm(x2, sh1, sc1, P['w_in_even'][j], EVEN_SPLIT)
            mix_ops, shf, sg, sr = even_mixer(parts, bt, L, shift0[j], gla0[j], rw0[j], j, P)
            glas.append(sg)
            rws.append(sr)
            shifts.append(shf)
            x2 = even_out(*mix_ops, P['gla_norm_g'][j], P['rw_ln_g'][j], P['rw_ln_b'][j], P['rw_r_k'][j],
                          P['w_out_even'][j], x2, g1)
            x2 = ffn_block(x2, sh2, sc2, g2, None, P['ffn_w_gate'][j], P['ffn_w_up'][j], P['ffn_w_down'][j])
        else:
            parts = mod_mm(x2, sh1, sc1, P['w_in_odd'][j], ODD_SPLIT)
            q, k, v, qi, ki, wi = odd_qkv(parts, bt, L, pos, j, P)
            ks.append(k)
            vs.append(v)
            kis.append(ki)
            x2 = mm_res(attend(j, q, k, v, qi, ki, wi).reshape(bt * L, -1), P['w_out_odd'][j], x2, g1)
            x2 = ffn_block(x2, sh2, sc2, g2, P['moe_router'][j], P['moe_w_gate'][j], P['moe_w_up'][j],
                           P['moe_w_down'][j])
    return (x2.reshape(bt, L, D), jnp.stack(glas), jnp.stack(rws), jnp.stack(shifts), jnp.stack(ks), jnp.stack(vs),
            jnp.stack(kis))


def kernel(x_prompt, x_sample, state_gla, state_rwkv, state_shift, cache_k, cache_v, cache_kidx, page_table, c_prompt, c_sample, w_ada, b_ada, w_in_even, gla_w_gate, gla_b_gate, gla_norm_g, rw_mu, rw_w0, rw_w_decay, rw_a0, rw_w_iclr, rw_w_gate, rw_k_k, rw_k_a, rw_r_k, rw_ln_g, rw_ln_b, w_out_even, w_in_odd, q_norm_g, k_norm_g, kidx_ln_g, kidx_ln_b, w_out_odd, ffn_w_gate, ffn_w_up, ffn_w_down, moe_router, moe_w_gate, moe_w_up, moe_w_down):
    P = dict(w_ada=w_ada, b_ada=b_ada, w_in_even=w_in_even, gla_w_gate=gla_w_gate, gla_b_gate=gla_b_gate,
             gla_norm_g=gla_norm_g, rw_mu=rw_mu, rw_w0=rw_w0, rw_w_decay=rw_w_decay, rw_a0=rw_a0,
             rw_w_iclr=rw_w_iclr, rw_w_gate=rw_w_gate, rw_k_k=rw_k_k, rw_k_a=rw_k_a, rw_r_k=rw_r_k,
             rw_ln_g=rw_ln_g, rw_ln_b=rw_ln_b, w_out_even=w_out_even, w_in_odd=w_in_odd,
             q_norm_g=q_norm_g, k_norm_g=k_norm_g, kidx_ln_g=kidx_ln_g, kidx_ln_b=kidx_ln_b,
             w_out_odd=w_out_odd, ffn_w_gate=ffn_w_gate, ffn_w_up=ffn_w_up, ffn_w_down=ffn_w_down,
             moe_router=moe_router, moe_w_gate=moe_w_gate, moe_w_up=moe_w_up, moe_w_down=moe_w_down)
    n_even = state_gla.shape[0]
    b_p, seq = x_prompt.shape[0], x_prompt.shape[1]
    zero_gla = jnp.zeros((n_even, b_p) + state_gla.shape[2:], state_gla.dtype)
    zero_rw = jnp.zeros((n_even, b_p) + state_rwkv.shape[2:], state_rwkv.dtype)
    zero_shift = jnp.zeros((n_even, b_p) + state_shift.shape[2:], state_shift.dtype)
    pos_p = jnp.arange(seq, dtype=jnp.int32)
    past = page_table.shape[1] * PAGE_SIZE
    pos_s = past + jnp.arange(x_sample.shape[1], dtype=jnp.int32)

    def prompt_attend(j, q, k, v, qi, ki, wi):
        return dsa_prompt_attention(q, k, v, qi, ki, wi)

    def sample_attend(j, q, k, v, qi, ki, wi):
        return dsa_sample_attention(q, k, v, qi, ki, wi, cache_k, cache_v, cache_kidx, page_table, j)

    silu_c = jax.nn.silu(jnp.concatenate([c_prompt, c_sample], axis=0))
    mods = [mm(silu_c, w_ada[layer]) + b_ada[layer] for layer in range(w_ada.shape[0])]
    mods_p = [m[:b_p] for m in mods]
    mods_s = [m[b_p:] for m in mods]

    y_prompt, p_gla, p_rw, p_shift, p_k, p_v, p_kidx = trunk(
        x_prompt, mods_p, pos_p, zero_shift, zero_gla, zero_rw, prompt_attend, P)
    y_sample, s_gla, s_rw, s_shift, s_k, s_v, s_kidx = trunk(
        x_sample, mods_s, pos_s, state_shift, state_gla, state_rwkv, sample_attend, P)
    return (y_prompt, y_sample, p_gla, p_rw, p_shift, p_k, p_v, p_kidx, s_gla, s_rw, s_shift, s_k, s_v, s_kidx)
```

```python
import functools
import math

import jax
import jax.numpy as jnp
import numpy as np
from jax import lax
from jax.experimental import pallas as pl
from jax.experimental.pallas import tpu as pltpu

F32 = jnp.float32
BF16 = jnp.bfloat16

PAGE_SIZE = 128
GLA_HEADS, GLA_DK, GLA_DV, GLA_GATE_RANK, GLA_TAU, GLA_CHUNK = 4, 64, 128, 16, 16.0, 64
RW_HEADS, RW_DH, RW_W_LORA, RW_A_LORA, RW_G_LORA, RW_LN_EPS = 8, 64, 32, 32, 96, 64e-5
ATT_HEADS, ATT_KV_HEADS, ATT_DH = 16, 4, 64
IDX_HEADS, IDX_DH = 8, 64
TOPK_MAX, Q_BLOCK, ROPE_THETA = 256, 128, 10000.0
N_EXPERTS, TOP_K_EXPERTS = 8, 2
EPS = 1e-6
GLA_QK = GLA_HEADS * GLA_DK
GLA_V = GLA_HEADS * GLA_DV
RW_W = RW_HEADS * RW_DH
ATT_Q = ATT_HEADS * ATT_DH
ATT_KV = ATT_KV_HEADS * ATT_DH

VMEM_LIMIT_BYTES = 56 * 1024 * 1024
LANES = 128
ROW_TILE = 512


def _mm_kernel(x_ref, w_ref, o_ref):
    o_ref[...] = jnp.dot(x_ref[...].astype(BF16), w_ref[...], preferred_element_type=F32)


def mm(x, w):
    m, k = x.shape
    n = w.shape[1]
    tm = min(ROW_TILE, m)
    assert m % tm == 0
    return pl.pallas_call(
        _mm_kernel,
        grid=(m // tm,),
        in_specs=[pl.BlockSpec((tm, k), lambda i: (i, 0)), pl.BlockSpec((k, n), lambda i: (0, 0))],
        out_specs=pl.BlockSpec((tm, n), lambda i: (i, 0)),
        out_shape=jax.ShapeDtypeStruct((m, n), F32),
        compiler_params=pltpu.CompilerParams(dimension_semantics=("arbitrary",),
                                             vmem_limit_bytes=VMEM_LIMIT_BYTES),
    )(x, w.astype(BF16))


def _mod_operand(mod, n_rows, tm):
    b, d = mod.shape
    per_b = n_rows // b
    if per_b % tm == 0:
        tiles_per_b = per_b // tm
        return mod[:, None, :], pl.BlockSpec((1, 1, d), lambda i, *_: (i // tiles_per_b, 0, 0))
    rows = jnp.repeat(mod, per_b, axis=0).reshape(n_rows // tm, tm, d)
    return rows, pl.BlockSpec((1, tm, d), lambda i, *_: (i, 0, 0))


def _modulated(x, shift, scale):
    xn = x * lax.rsqrt(jnp.mean(x * x, axis=-1, keepdims=True) + EPS)
    return xn * (1.0 + scale) + shift


def _aligned_offsets(widths):
    offs, off = [], 0
    for w in widths:
        offs.append(off)
        off += -(-w // LANES) * LANES
    return offs, off


def _mod_mm_kernel(widths, x_ref, sh_ref, sc_ref, w_ref, *o_refs):
    h = _modulated(x_ref[...], sh_ref[0], sc_ref[0])
    y = jnp.dot(h.astype(BF16), w_ref[...], preferred_element_type=F32)
    for o_ref, off, width in zip(o_refs, _aligned_offsets(widths)[0], widths):
        o_ref[...] = y[:, off:off + width]


def mod_mm(x, shift, scale, w, widths):
    t, d = x.shape
    assert sum(widths) == w.shape[1]
    tm = min(ROW_TILE, t)
    sh, mod_spec = _mod_operand(shift, t, tm)
    sc, _ = _mod_operand(scale, t, tm)
    offs, n_pad = _aligned_offsets(widths)
    starts = np.cumsum([0] + list(widths))
    cols = [jnp.pad(w[:, starts[i]:starts[i + 1]], ((0, 0), (0, -widths[i] % LANES))) for i in range(len(widths))]
    w_pad = jnp.concatenate(cols, axis=1).astype(BF16)
    return pl.pallas_call(
        functools.partial(_mod_mm_kernel, tuple(widths)),
        grid=(t // tm,),
        in_specs=[pl.BlockSpec((tm, d), lambda i: (i, 0)), mod_spec, mod_spec, pl.BlockSpec((d, n_pad), lambda i: (0, 0))],
        out_specs=[pl.BlockSpec((tm, wd), lambda i: (i, 0)) for wd in widths],
        out_shape=[jax.ShapeDtypeStruct((t, wd), F32) for wd in widths],
        compiler_params=pltpu.CompilerParams(dimension_semantics=("arbitrary",),
                                             vmem_limit_bytes=VMEM_LIMIT_BYTES),
        name="modulated_in_proj",
    )(x, sh, sc, w_pad)


def _even_out_kernel(oa_ref, ra_ref, yb_ref, rb_ref, kb_ref, vb_ref, gate_ref, gn_ref, lng_ref, lnb_ref, rk_ref,
                     w_ref, x_ref, g_ref, o_ref):
    pieces = []
    for h in range(GLA_HEADS):
        sl = slice(h * GLA_DV, (h + 1) * GLA_DV)
        o = oa_ref[:, sl]
        r = ra_ref[:, sl]
        o = o * lax.rsqrt(jnp.mean(o * o, axis=-1, keepdims=True) + EPS) * gn_ref[:, sl]
        pieces.append(o * (r * (1.0 / (1.0 + jnp.exp(-r)))))
    for h in range(RW_HEADS):
        sl = slice(h * RW_DH, (h + 1) * RW_DH)
        y = yb_ref[:, sl]
        mu = jnp.mean(y, axis=-1, keepdims=True)
        yc = y - mu
        var = jnp.mean(yc * yc, axis=-1, keepdims=True)
        y = yc * lax.rsqrt(var + RW_LN_EPS) * lng_ref[:, sl] + lnb_ref[:, sl]
        bonus = jnp.sum(rb_ref[:, sl] * kb_ref[:, sl] * rk_ref[:, sl], axis=-1, keepdims=True)
        pieces.append((y + bonus * vb_ref[:, sl]) * gate_ref[:, sl])
    mix = jnp.concatenate(pieces, axis=1).astype(BF16)
    o_ref[...] = x_ref[...] + g_ref[0] * jnp.dot(mix, w_ref[...], preferred_element_type=F32)


def even_out(o_a, r_a, y_b, r_b, k_b, v_b, gate_b, gla_norm_g, ln_g, ln_b, r_k, w, x, gate):
    t, d = x.shape
    tm = min(ROW_TILE, t)
    g, mod_spec = _mod_operand(gate, t, tm)
    seq = pl.BlockSpec((tm, GLA_V), lambda i: (i, 0))
    par = pl.BlockSpec((1, GLA_V), lambda i: (0, 0))
    assert GLA_V == RW_W
    return pl.pallas_call(
        _even_out_kernel,
        grid=(t // tm,),
        in_specs=[seq] * 7 + [par] * 4 + [pl.BlockSpec((GLA_V + RW_W, d), lambda i: (0, 0)),
                                         pl.BlockSpec((tm, d), lambda i: (i, 0)), mod_spec],
        out_specs=pl.BlockSpec((tm, d), lambda i: (i, 0)),
        out_shape=jax.ShapeDtypeStruct((t, d), F32),
        compiler_params=pltpu.CompilerParams(dimension_semantics=("arbitrary",),
                                             vmem_limit_bytes=VMEM_LIMIT_BYTES),
        name="even_out_proj_residual",
    )(o_a, r_a, y_b, r_b, k_b, v_b, gate_b, jnp.tile(gla_norm_g, GLA_HEADS)[None], ln_g[None], ln_b[None],
      r_k.reshape(1, RW_W), w.astype(BF16), x, g)


def _mm_res_kernel(a_ref, w_ref, x_ref, g_ref, o_ref):
    y = jnp.dot(a_ref[...].astype(BF16), w_ref[...], preferred_element_type=F32)
    o_ref[...] = x_ref[...] + g_ref[0] * y


def mm_res(a, w, x, gate):
    t, k = a.shape
    d = w.shape[1]
    tm = min(ROW_TILE, t)
    g, mod_spec = _mod_operand(gate, t, tm)
    return pl.pallas_call(
        _mm_res_kernel,
        grid=(t // tm,),
        in_specs=[pl.BlockSpec((tm, k), lambda i: (i, 0)), pl.BlockSpec((k, d), lambda i: (0, 0)),
                  pl.BlockSpec((tm, d), lambda i: (i, 0)), mod_spec],
        out_specs=pl.BlockSpec((tm, d), lambda i: (i, 0)),
        out_shape=jax.ShapeDtypeStruct((t, d), F32),
        compiler_params=pltpu.CompilerParams(dimension_semantics=("arbitrary",),
                                             vmem_limit_bytes=VMEM_LIMIT_BYTES),
        name="out_proj_residual",
    )(a, w.astype(BF16), x, g)


def _ffn_kernel(routed, x_ref, sh_ref, sc_ref, gt_ref, wr_ref, wg_ref, wu_ref, wd_ref, o_ref, h_scr, cmb_scr, acc_scr):
    e = pl.program_id(1)
    lane = lax.broadcasted_iota(jnp.int32, cmb_scr.shape, 1)

    @pl.when(e == 0)
    def _():
        hb = _modulated(x_ref[...], sh_ref[0], sc_ref[0]).astype(BF16)
        h_scr[...] = hb
        acc_scr[...] = jnp.zeros(acc_scr.shape, F32)
        if routed:
            logits = jnp.dot(hb, wr_ref[...], preferred_element_type=F32)
            lg = jnp.where(lane < N_EXPERTS, logits, -jnp.inf)
            m1 = jnp.max(lg, axis=1, keepdims=True)
            i1 = jnp.min(jnp.where(lg == m1, lane, LANES), axis=1, keepdims=True)
            lg2 = jnp.where(lane == i1, -jnp.inf, lg)
            m2 = jnp.max(lg2, axis=1, keepdims=True)
            i2 = jnp.min(jnp.where(lg2 == m2, lane, LANES), axis=1, keepdims=True)
            t = jnp.exp(m2 - m1)
            p1 = 1.0 / (1.0 + t)
            cmb_scr[...] = jnp.where(lane == i1, p1, 0.0) + jnp.where(lane == i2, t * p1, 0.0)

    hb = h_scr[...]
    g = jnp.dot(hb, wg_ref[0], preferred_element_type=F32)
    u = jnp.dot(hb, wu_ref[0], preferred_element_type=F32)
    act = g * (1.0 / (1.0 + jnp.exp(-g))) * u
    y = jnp.dot(act.astype(BF16), wd_ref[0], preferred_element_type=F32)
    if routed:
        y = y * jnp.sum(jnp.where(lane == e, cmb_scr[...], 0.0), axis=1, keepdims=True)
    acc_scr[...] += y

    @pl.when(e == pl.num_programs(1) - 1)
    def _():
        o_ref[...] = x_ref[...] + gt_ref[0] * acc_scr[...]


FFN_BLOCK = 1408


def ffn_block(x, shift, scale, gate, w_router, w_gate, w_up, w_down):
    t, d = x.shape
    routed = w_router is not None
    tm = min(ROW_TILE, t)
    sh, mod_spec = _mod_operand(shift, t, tm)
    sc, _ = _mod_operand(scale, t, tm)
    gt, _ = _mod_operand(gate, t, tm)
    if routed:
        n_e, _, f = w_gate.shape
        assert f == FFN_BLOCK and n_e == N_EXPERTS
        wr = jnp.pad(w_router, ((0, 0), (0, LANES - n_e))).astype(BF16)
        up_spec = pl.BlockSpec((1, d, f), lambda i, e: (e, 0, 0))
        down_spec = pl.BlockSpec((1, f, d), lambda i, e: (e, 0, 0))
    else:
        f_all = w_gate.shape[1]
        assert f_all % FFN_BLOCK == 0
        n_e, f = f_all // FFN_BLOCK, FFN_BLOCK
        wr = jnp.zeros((d, LANES), BF16)
        w_gate, w_up, w_down = w_gate[None], w_up[None], w_down[None]
        up_spec = pl.BlockSpec((1, d, f), lambda i, e: (0, 0, e))
        down_spec = pl.BlockSpec((1, f, d), lambda i, e: (0, e, 0))
    return pl.pallas_call(
        functools.partial(_ffn_kernel, routed),
        grid=(t // tm, n_e),
        in_specs=[pl.BlockSpec((tm, d), lambda i, e: (i, 0)), mod_spec, mod_spec, mod_spec,
                  pl.BlockSpec((d, LANES), lambda i, e: (0, 0)), up_spec, up_spec, down_spec],
        out_specs=pl.BlockSpec((tm, d), lambda i, e: (i, 0)),
        out_shape=jax.ShapeDtypeStruct((t, d), F32),
        scratch_shapes=[pltpu.VMEM((tm, d), BF16), pltpu.VMEM((tm, LANES), F32), pltpu.VMEM((tm, d), F32)],
        compiler_params=pltpu.CompilerParams(dimension_semantics=("arbitrary", "arbitrary"),
                                             vmem_limit_bytes=VMEM_LIMIT_BYTES),
        name="routed_experts" if routed else "dense_swiglu",
    )(x, sh, sc, gt, wr, w_gate.astype(BF16), w_up.astype(BF16), w_down.astype(BF16))


RW_CHUNK = 64


def _bdot(a, b):
    return jnp.dot(a.astype(BF16), b.astype(BF16), preferred_element_type=F32)


def _bdot_nt(a, b):
    return lax.dot_general(a.astype(BF16), b.astype(BF16), (((1,), (1,)), ((), ())), preferred_element_type=F32)


def _bdot_tn(a, b):
    return lax.dot_general(a.astype(BF16), b.astype(BF16), (((0,), (0,)), ((), ())), preferred_element_type=F32)


def _split3(x):
    hi = x.astype(BF16)
    r1 = x - hi.astype(F32)
    mid = r1.astype(BF16)
    lo = (r1 - mid.astype(F32)).astype(BF16)
    return hi, mid, lo


def _rwkv_chunk_heads(r, lw, k, v, a, b, h0, tri_bf, strict, incl, eye):
    nh = len(r)
    hs = range(nh)
    cum = []
    for i in hs:
        hi, mid, lo = _split3(lw[i])
        cum.append(jnp.dot(tri_bf, hi, preferred_element_type=F32) + jnp.dot(tri_bf, mid, preferred_element_type=F32)
                   + jnp.dot(tri_bf, lo, preferred_element_type=F32))
    cum_last = [cum[i][RW_CHUNK - 1:RW_CHUNK, :] for i in hs]
    e_neg = [jnp.exp(-cum[i]) for i in hs]
    e_end = [jnp.exp(cum_last[i] - cum[i]) for i in hs]
    a_t = [a[i] * jnp.exp(cum[i] - lw[i]) for i in hs]
    r_t = [r[i] * jnp.exp(cum[i]) for i in hs]
    k_t = [k[i] * e_neg[i] for i in hs]
    b_t = [b[i] * e_neg[i] for i in hs]
    k_e = [k[i] * e_end[i] for i in hs]
    b_e = [b[i] * e_end[i] for i in hs]
    a_ab = [jnp.where(strict, _bdot_nt(a_t[i], b_t[i]), 0.0) for i in hs]
    a_ak = [jnp.where(strict, _bdot_nt(a_t[i], k_t[i]), 0.0) for i in hs]
    a_rb = [jnp.where(incl, _bdot_nt(r_t[i], b_t[i]), 0.0) for i in hs]
    a_rk = [jnp.where(incl, _bdot_nt(r_t[i], k_t[i]), 0.0) for i in hs]
    x = [eye + a_ab[i] for i in hs]
    y = list(a_ab)
    for _ in range(int(math.log2(RW_CHUNK)) - 1):
        y = [_bdot(y[i], y[i]) for i in hs]
        x = [x[i] + _bdot(x[i], y[i]) for i in hs]
    a_p = [_bdot(x[i], a_t[i]) for i in hs]
    akv = [_bdot(a_ak[i], v[i]) for i in hs]
    v_p = [_bdot(x[i], akv[i]) for i in hs]
    r_p = [r_t[i] + _bdot(a_rb[i], a_p[i]) for i in hs]
    y_p = [_bdot(a_rk[i], v[i]) + _bdot(a_rb[i], v_p[i]) for i in hs]
    m_lr = [_bdot_tn(b_e[i], a_p[i]) for i in hs]
    g = [_bdot_tn(k_e[i], v[i]) + _bdot_tn(b_e[i], v_p[i]) for i in hs]
    w_col = [jnp.sum(jnp.where(eye, jnp.exp(cum_last[i]), 0.0), axis=1, keepdims=True) for i in hs]
    y_out = [_bdot(r_p[i], h0[i]) + y_p[i] for i in hs]
    h_new = [w_col[i] * h0[i] + _bdot(m_lr[i], h0[i]) + g[i] for i in hs]
    return y_out, h_new


RW_HEAD_GROUP = 16


def _rwkv_kernel(r_ref, lw_ref, k_ref, v_ref, a_ref, b_ref, h0_ref, y_ref, hout_ref, h_scr):
    c = pl.program_id(0)

    @pl.when(c == 0)
    def _():
        h_scr[...] = h0_ref[...]

    rows = lax.broadcasted_iota(jnp.int32, (RW_CHUNK, RW_CHUNK), 0)
    cols = lax.broadcasted_iota(jnp.int32, (RW_CHUNK, RW_CHUNK), 1)
    strict = rows > cols
    incl = rows >= cols
    eye = rows == cols
    tri_bf = jnp.where(incl, 1.0, 0.0).astype(BF16)
    n_b, _, width = r_ref.shape
    n = h_scr.shape[-1]
    heads = [(b, h) for b in range(n_b) for h in range(width // n)]
    for i0 in range(0, len(heads), RW_HEAD_GROUP):
        grp = heads[i0:i0 + RW_HEAD_GROUP]
        seqs = [[ref[b, :, h * n:(h + 1) * n] for b, h in grp] for ref in (r_ref, lw_ref, k_ref, v_ref, a_ref, b_ref)]
        states = [h_scr[b * (width // n) + h] for b, h in grp]
        y_out, h_new = _rwkv_chunk_heads(*seqs, states, tri_bf, strict, incl, eye)
        for j, (b, h) in enumerate(grp):
            y_ref[b, :, h * n:(h + 1) * n] = y_out[j]
            h_scr[b * (width // n) + h] = h_new[j]

    @pl.when(c == pl.num_programs(0) - 1)
    def _():
        hout_ref[...] = h_scr[...]


def rwkv7_chunked(r, lw, k, v, a, b, s0):
    bt, L, H, N = r.shape
    assert L % RW_CHUNK == 0
    flat = lambda t: t.reshape(bt, L, H * N)
    h0 = s0.transpose(0, 1, 3, 2).reshape(bt * H, N, N)
    seq_spec = pl.BlockSpec((bt, RW_CHUNK, H * N), lambda c: (0, c, 0))
    st_spec = pl.BlockSpec((bt * H, N, N), lambda c: (0, 0, 0))
    y, h_fin = pl.pallas_call(
        _rwkv_kernel,
        grid=(L // RW_CHUNK,),
        in_specs=[seq_spec] * 6 + [st_spec],
        out_specs=[seq_spec, st_spec],
        out_shape=[jax.ShapeDtypeStruct((bt, L, H * N), F32), jax.ShapeDtypeStruct((bt * H, N, N), F32)],
        scratch_shapes=[pltpu.VMEM((bt * H, N, N), F32)],
        compiler_params=pltpu.CompilerParams(dimension_semantics=("arbitrary",),
                                             vmem_limit_bytes=VMEM_LIMIT_BYTES),
        name="rwkv7_chunked",
    )(flat(r), flat(lw), flat(k), flat(v), flat(a), flat(b), h0)
    return y.reshape(bt, L, H, N), h_fin.reshape(bt, H, N, N).transpose(0, 1, 3, 2)


def _gla_kernel(q_ref, k_ref, v_ref, g_ref, s0_ref, o_ref, sout_ref, s_scr):
    c = pl.program_id(0)

    @pl.when(c == 0)
    def _():
        s_scr[...] = s0_ref[...]

    rows = lax.broadcasted_iota(jnp.int32, (GLA_CHUNK, GLA_CHUNK), 0)
    cols = lax.broadcasted_iota(jnp.int32, (GLA_CHUNK, GLA_CHUNK), 1)
    incl = rows >= cols
    eye = lax.broadcasted_iota(jnp.int32, (GLA_DK, GLA_DK), 0) == lax.broadcasted_iota(jnp.int32, (GLA_DK, GLA_DK), 1)
    tri_bf = jnp.where(incl, 1.0, 0.0).astype(BF16)
    n_b, _, qk_width = q_ref.shape
    n_h = qk_width // GLA_DK
    heads = [(b, h) for b in range(n_b) for h in range(n_h)]
    hs = range(len(heads))
    qs = [q_ref[b, :, h * GLA_DK:(h + 1) * GLA_DK] for b, h in heads]
    ks = [k_ref[b, :, h * GLA_DK:(h + 1) * GLA_DK] for b, h in heads]
    vs = [v_ref[b, :, h * GLA_DV:(h + 1) * GLA_DV] for b, h in heads]
    bcum = []
    for b, h in heads:
        hi, mid, lo = _split3(g_ref[b, :, h * GLA_DK:(h + 1) * GLA_DK])
        bcum.append(jnp.dot(tri_bf, hi, preferred_element_type=F32) + jnp.dot(tri_bf, mid, preferred_element_type=F32)
                    + jnp.dot(tri_bf, lo, preferred_element_type=F32))
    blast = [bcum[i][GLA_CHUNK - 1:GLA_CHUNK, :] for i in hs]
    q_e = [qs[i] * jnp.exp(bcum[i]) for i in hs]
    k_e = [ks[i] * jnp.exp(-bcum[i]) for i in hs]
    k_end = [ks[i] * jnp.exp(blast[i] - bcum[i]) for i in hs]
    att = [jnp.where(incl, _bdot_nt(q_e[i], k_e[i]), 0.0) for i in hs]
    s_old = [s_scr[i] for i in hs]
    o = [_bdot(att[i], vs[i]) + _bdot(q_e[i], s_old[i]) for i in hs]
    kv = [_bdot_tn(k_end[i], vs[i]) for i in hs]
    dec = [jnp.sum(jnp.where(eye, jnp.exp(blast[i]), 0.0), axis=1, keepdims=True) for i in hs]
    for i, (b, h) in enumerate(heads):
        o_ref[b, :, h * GLA_DV:(h + 1) * GLA_DV] = o[i]
        s_scr[i] = dec[i] * s_old[i] + kv[i]

    @pl.when(c == pl.num_programs(0) - 1)
    def _():
        sout_ref[...] = s_scr[...]


def gla_chunked_pallas(q, k, v, log_a, s0):
    bt, L, H, dk = q.shape
    dv = v.shape[-1]
    assert L % GLA_CHUNK == 0
    flat = lambda t: t.reshape(bt, L, H * t.shape[-1])
    qk_spec = pl.BlockSpec((bt, GLA_CHUNK, H * dk), lambda c: (0, c, 0))
    v_spec = pl.BlockSpec((bt, GLA_CHUNK, H * dv), lambda c: (0, c, 0))
    st_spec = pl.BlockSpec((bt * H, dk, dv), lambda c: (0, 0, 0))
    o, s_fin = pl.pallas_call(
        _gla_kernel,
        grid=(L // GLA_CHUNK,),
        in_specs=[qk_spec, qk_spec, v_spec, qk_spec, st_spec],
        out_specs=[v_spec, st_spec],
        out_shape=[jax.ShapeDtypeStruct((bt, L, H * dv), F32), jax.ShapeDtypeStruct((bt * H, dk, dv), F32)],
        scratch_shapes=[pltpu.VMEM((bt * H, dk, dv), F32)],
        compiler_params=pltpu.CompilerParams(dimension_semantics=("arbitrary",),
                                             vmem_limit_bytes=VMEM_LIMIT_BYTES),
        name="gla_chunked",
    )(flat(q), flat(k), flat(v), flat(log_a), s0.reshape(bt * H, dk, dv))
    return o.reshape(bt, L, H, dv), s_fin.reshape(bt, H, dk, dv)


DSA_QB = 128
DSA_KC = 512
MASK_NEG = -1e30
INT_MIN = -2 ** 31
INT16_MIN = -2 ** 15
ONE16 = np.int16(1)
ZERO16 = np.int16(0)
ONE = np.float32(1.0)
ZERO = np.float32(0.0)
G_PER_KV = ATT_HEADS // ATT_KV_HEADS


def _sortable_key(x):
    bits = lax.bitcast_convert_type(x + 0.0, jnp.int32)
    return bits ^ ((bits >> 31) & 0x7FFFFFFF)


def _dsa_kernel(topk, q_ref, qi_ref, wi_ref, k_ref, v_ref, kit_ref, upper_ref, o_ref,
                wb_scr, key_scr, khi_scr, klo_scr, m_scr, acc_scr):
    i = pl.program_id(1)
    n_chunks = (i * DSA_QB) // DSA_KC + 1
    nt = DSA_KC // LANES
    q_pos = i * DSA_QB + lax.broadcasted_iota(jnp.int32, (DSA_QB, LANES), 0)
    lane = lax.broadcasted_iota(jnp.int32, (DSA_QB, LANES), 1)

    for h in range(IDX_HEADS):
        wb_scr[h] = jnp.broadcast_to(wi_ref[0, :, h:h + 1], (DSA_QB, LANES))

    def score_chunk(c, carry):
        dots = jnp.dot(qi_ref[0, 0], kit_ref[0, c], preferred_element_type=F32)
        for jt in range(nt):
            sc = jnp.zeros((DSA_QB, LANES), F32)
            for h in range(IDX_HEADS):
                d = dots[h * DSA_QB:(h + 1) * DSA_QB, jt * LANES:(jt + 1) * LANES]
                sc = sc + jnp.maximum(d, 0.0) * wb_scr[h]
            k_pos = c * DSA_KC + jt * LANES + lane
            sc = jnp.where(k_pos <= q_pos, sc, -jnp.inf)
            key = _sortable_key(sc)
            key_scr[c, :, jt * LANES:(jt + 1) * LANES] = key
            khi_scr[c, :, jt * LANES:(jt + 1) * LANES] = (key >> 16).astype(jnp.int16)
        return carry

    lax.fori_loop(0, n_chunks, score_chunk, 0)

    def count16(scr, cand, strict):
        cand_b = jnp.broadcast_to(cand.astype(jnp.int16), (DSA_QB, LANES))

        def body(c, acc):
            for jt in range(nt):
                tile = scr[c, :, jt * LANES:(jt + 1) * LANES]
                hit = (tile > cand_b) if strict else (tile >= cand_b)
                acc = acc + jnp.where(hit, ONE16, ZERO16)
            return acc

        acc = lax.fori_loop(0, n_chunks, body, jnp.zeros((DSA_QB, LANES), jnp.int16))
        return jnp.sum(acc.astype(F32), axis=1, keepdims=True)

    def bisect16(scr, k_need):
        cur = jnp.where(count16(scr, jnp.zeros((DSA_QB, 1), jnp.int32), False) >= k_need, 0, INT16_MIN)
        cur = cur.astype(jnp.int32)

        def bit_step(p, cur):
            cand = cur | (jnp.int32(1) << (14 - p))
            return jnp.where(count16(scr, cand, False) >= k_need, cand, cur)

        return lax.fori_loop(0, 15, bit_step, cur)

    kf = float(topk)
    thr_hi = bisect16(khi_scr, kf)
    above = count16(khi_scr, thr_hi, True)
    thr_hi_b = jnp.broadcast_to(thr_hi, (DSA_QB, LANES))

    def low_halves(c, carry):
        for jt in range(nt):
            key = key_scr[c, :, jt * LANES:(jt + 1) * LANES]
            lo = (key & 0xFFFF) - 32768
            klo_scr[c, :, jt * LANES:(jt + 1) * LANES] = jnp.where((key >> 16) == thr_hi_b, lo, INT16_MIN).astype(jnp.int16)
        return carry

    lax.fori_loop(0, n_chunks, low_halves, 0)
    thr_lo = bisect16(klo_scr, kf - above)
    n_gt = above + count16(klo_scr, thr_lo, True)
    need = kf - n_gt
    thr = (thr_hi << 16) | (thr_lo + 32768)

    m_scr[...] = jnp.full(m_scr.shape, MASK_NEG, F32)
    acc_scr[...] = jnp.zeros(acc_scr.shape, F32)

    def attend_chunk(c, tie_carry):
        k0 = pl.multiple_of(c * DSA_KC, DSA_KC)
        keys = key_scr[c]
        thr_c = jnp.broadcast_to(thr, (DSA_QB, DSA_KC))
        eq = keys == thr_c
        eq_f = jnp.where(eq, ONE, ZERO)
        rank = tie_carry + jnp.dot(eq_f.astype(BF16), upper_ref[...], preferred_element_type=F32)
        k_pos = k0 + lax.broadcasted_iota(jnp.int32, (DSA_QB, DSA_KC), 1)
        qp = i * DSA_QB + lax.broadcasted_iota(jnp.int32, (DSA_QB, DSA_KC), 0)
        sel = jnp.where(keys > thr_c, ONE, jnp.where(rank < need, eq_f, ZERO))
        bias = jnp.where((sel > 0.5) & (k_pos <= qp), ZERO, np.float32(MASK_NEG))
        v_c = v_ref[0, pl.ds(k0, DSA_KC), :]
        rows = G_PER_KV * DSA_QB
        for n in range(ATT_KV_HEADS):
            s = lax.dot_general(q_ref[0, 0, n], k_ref[0, n, pl.ds(k0, DSA_KC), :], (((1,), (1,)), ((), ())),
                                preferred_element_type=F32)
            s = (s.reshape(G_PER_KV, DSA_QB, DSA_KC) + bias[None]).reshape(rows, DSA_KC)
            tiles = [s[:, jt * LANES:(jt + 1) * LANES] for jt in range(nt)]
            tile_max = functools.reduce(jnp.maximum, tiles)
            m_old = m_scr[n]
            m_new = jnp.maximum(m_old, jnp.max(tile_max, axis=1, keepdims=True))
            alpha = jnp.exp(m_old - m_new)
            p = jnp.concatenate([jnp.exp((t - m_new).astype(BF16)) for t in tiles], axis=1)
            pv = jnp.dot(p, v_c[:, n * LANES:(n + 1) * LANES], preferred_element_type=F32)
            acc_scr[n] = alpha * acc_scr[n] + pv
            m_scr[n] = m_new
        return tie_carry + jnp.sum(eq_f, axis=1, keepdims=True)

    lax.fori_loop(0, n_chunks, attend_chunk, jnp.zeros((DSA_QB, 1), F32))

    for n in range(ATT_KV_HEADS):
        acc = acc_scr[n]
        o_n = acc[:, :ATT_DH] / acc[:, ATT_DH:]
        for g in range(G_PER_KV):
            hd = n * G_PER_KV + g
            o_ref[0, :, hd * ATT_DH:(hd + 1) * ATT_DH] = o_n[g * DSA_QB:(g + 1) * DSA_QB, :]


def dsa_prompt_attention(q, k, v, qi, ki, wi):
    bt, L = q.shape[:2]
    assert L % DSA_KC == 0 and L // 4 >= 1
    topk = min(TOPK_MAX, L // 4)
    nb = L // DSA_QB
    nc = L // DSA_KC
    kit = ki.astype(BF16).reshape(bt, nc, DSA_KC, IDX_DH).transpose(0, 1, 3, 2)
    upper = jnp.triu(jnp.ones((DSA_KC, DSA_KC), BF16), 1)
    rows = G_PER_KV * DSA_QB
    q_blocks = (q * ATT_DH ** -0.5).astype(BF16).reshape(bt, nb, DSA_QB, ATT_KV_HEADS, G_PER_KV, ATT_DH)
    q_blocks = q_blocks.transpose(0, 1, 3, 4, 2, 5).reshape(bt, nb, ATT_KV_HEADS, rows, ATT_DH)
    qi_blocks = qi.astype(BF16).reshape(bt, nb, DSA_QB, IDX_HEADS, IDX_DH).transpose(0, 1, 3, 2, 4)
    qi_blocks = qi_blocks.reshape(bt, nb, IDX_HEADS * DSA_QB, IDX_DH)
    v_aug = jnp.concatenate([v.astype(BF16), jnp.ones(v.shape, BF16)], axis=-1).reshape(bt, L, ATT_KV_HEADS * LANES)
    return pl.pallas_call(
        functools.partial(_dsa_kernel, topk),
        grid=(bt, nb),
        in_specs=[
            pl.BlockSpec((1, 1, ATT_KV_HEADS, rows, ATT_DH), lambda b, i: (b, i, 0, 0, 0)),
            pl.BlockSpec((1, 1, IDX_HEADS * DSA_QB, IDX_DH), lambda b, i: (b, i, 0, 0)),
            pl.BlockSpec((1, DSA_QB, IDX_HEADS), lambda b, i: (b, i, 0)),
            pl.BlockSpec((1, ATT_KV_HEADS, L, ATT_DH), lambda b, i: (b, 0, 0, 0)),
            pl.BlockSpec((1, L, ATT_KV_HEADS * LANES), lambda b, i: (b, 0, 0)),
            pl.BlockSpec((1, nc, IDX_DH, DSA_KC), lambda b, i: (b, 0, 0, 0)),
            pl.BlockSpec((DSA_KC, DSA_KC), lambda b, i: (0, 0)),
        ],
        out_specs=pl.BlockSpec((1, DSA_QB, ATT_Q), lambda b, i: (b, i, 0)),
        out_shape=jax.ShapeDtypeStruct((bt, L, ATT_Q), F32),
        scratch_shapes=[
            pltpu.VMEM((IDX_HEADS, DSA_QB, LANES), F32),
            pltpu.VMEM((nc, DSA_QB, DSA_KC), jnp.int32),
            pltpu.VMEM((nc, DSA_QB, DSA_KC), jnp.int16),
            pltpu.VMEM((nc, DSA_QB, DSA_KC), jnp.int16),
            pltpu.VMEM((ATT_KV_HEADS, rows, LANES), F32),
            pltpu.VMEM((ATT_KV_HEADS, rows, LANES), F32),
        ],
        compiler_params=pltpu.CompilerParams(dimension_semantics=("arbitrary", "arbitrary"),
                                             vmem_limit_bytes=VMEM_LIMIT_BYTES),
        name="dsa_prompt_attention",
    )(q_blocks, qi_blocks, wi,
      k.astype(BF16).transpose(0, 2, 1, 3), v_aug, kit, upper)


NEW_PAD = 16
PAGES_PER_STEP = 16


def _dsa_sample_kernel(topk, n_new, pt_ref, qbd_ref, qi_ref, wrep_ref, knew_ref, vnew_ref, kinew_ref, *rest):
    pages = rest[:3 * PAGES_PER_STEP]
    upper_ref, o_ref, kbuf, vbuf, kibuf = rest[3 * PAGES_PER_STEP:]
    p = pl.program_id(1)
    n_steps = pl.num_programs(1)
    past = n_steps * PAGES_PER_STEP * PAGE_SIZE
    n_keys = kbuf.shape[0]
    nq = qbd_ref.shape[1] // ATT_HEADS
    for buf, refs in zip((kbuf, vbuf, kibuf), (pages[:PAGES_PER_STEP], pages[PAGES_PER_STEP:2 * PAGES_PER_STEP],
                                               pages[2 * PAGES_PER_STEP:])):
        for s, ref in enumerate(refs):
            row0 = pl.multiple_of((p * PAGES_PER_STEP + s) * PAGE_SIZE, PAGE_SIZE)
            buf[pl.ds(row0, PAGE_SIZE), :] = ref[0, 0].astype(BF16)

    @pl.when(p == n_steps - 1)
    def _():
        tail = n_keys - past
        kbuf[past:, :] = jnp.zeros((tail, ATT_KV), BF16)
        vbuf[past:, :] = jnp.zeros((tail, ATT_KV), BF16)
        kibuf[past:, :] = jnp.zeros((tail, IDX_DH), BF16)
        kbuf[past:past + NEW_PAD, :] = knew_ref[0].astype(BF16)
        vbuf[past:past + NEW_PAD, :] = vnew_ref[0].astype(BF16)
        kibuf[past:past + NEW_PAD, :] = kinew_ref[0].astype(BF16)

        dots = lax.dot_general(qi_ref[0], kibuf[...], (((1,), (1,)), ((), ())), preferred_element_type=F32)
        k_idx = lax.broadcasted_iota(jnp.int32, (8, n_keys), 1)
        q_row = lax.broadcasted_iota(jnp.int32, (8, n_keys), 0)
        valid = (k_idx - past <= q_row) & (q_row < nq)
        rows = []
        for qn in range(nq):
            d = jnp.maximum(dots[qn * IDX_HEADS:(qn + 1) * IDX_HEADS], 0.0)
            w = wrep_ref[0, qn * IDX_HEADS:(qn + 1) * IDX_HEADS, :]
            parts = []
            for jt in range(n_keys // LANES):
                parts.append(jnp.sum(d[:, jt * LANES:(jt + 1) * LANES] * w, axis=0, keepdims=True))
            rows.append(jnp.concatenate(parts, axis=1))
        rows.append(jnp.zeros((8 - nq, n_keys), F32))
        sc = jnp.where(valid, jnp.concatenate(rows, axis=0), -jnp.inf)
        keys = _sortable_key(sc)

        kf = float(topk)

        def count_ge(cand):
            return jnp.sum(jnp.where(keys >= cand, ONE, ZERO), axis=1, keepdims=True)

        cur = jnp.where(count_ge(jnp.zeros((8, 1), jnp.int32)) >= kf, 0, INT_MIN).astype(jnp.int32)

        def bit_step(i, cur):
            cand = cur | (jnp.int32(1) << (30 - i))
            return jnp.where(count_ge(cand) >= kf, cand, cur)

        thr = lax.fori_loop(0, 31, bit_step, cur)
        n_gt = jnp.sum(jnp.where(keys > thr, ONE, ZERO), axis=1, keepdims=True)
        need = kf - n_gt
        eq_f = jnp.where(keys == thr, ONE, ZERO)
        carry = jnp.zeros((8, 1), F32)
        ranks = []
        for jt in range(n_keys // LANES):
            e = eq_f[:, jt * LANES:(jt + 1) * LANES]
            ranks.append(carry + jnp.dot(e.astype(BF16), upper_ref[...], preferred_element_type=F32))
            carry = carry + jnp.sum(e, axis=1, keepdims=True)
        rank = jnp.concatenate(ranks, axis=1)
        sel = jnp.where(keys > thr, ONE, jnp.where(rank < need, eq_f, ZERO))
        bias = jnp.where((sel > 0.5) & valid, ZERO, np.float32(MASK_NEG))

        s = lax.dot_general(qbd_ref[0], kbuf[...], (((1,), (1,)), ((), ())), preferred_element_type=F32)
        s = jnp.concatenate([s[qn * ATT_HEADS:(qn + 1) * ATT_HEADS] + bias[qn:qn + 1] for qn in range(nq)], axis=0)
        m = jnp.max(s, axis=1, keepdims=True)
        pr = jnp.exp(s - m)
        l = jnp.sum(pr, axis=1, keepdims=True)
        o_ref[0] = jnp.dot(pr.astype(BF16), vbuf[...], preferred_element_type=F32) / l


def dsa_sample_attention(q, k_new, v_new, qi, ki_new, wi, cache_k, cache_v, cache_kidx, page_table, j):
    bt, nq = q.shape[:2]
    n_pages = page_table.shape[1]
    past = n_pages * PAGE_SIZE
    assert nq <= 8 and nq <= NEW_PAD
    topk = min(TOPK_MAX, (past + nq) // 4)
    n_keys = past + LANES
    n_pool = cache_k.shape[1]
    qs = (q * ATT_DH ** -0.5).astype(BF16).reshape(bt, nq, ATT_KV_HEADS, G_PER_KV, 1, ATT_DH)
    eye = jnp.eye(ATT_KV_HEADS, dtype=BF16)[None, None, :, None, :, None]
    qbd = (qs * eye).reshape(bt, nq * ATT_HEADS, ATT_KV)
    qi2 = qi.astype(BF16).reshape(bt, nq * IDX_HEADS, IDX_DH)
    wrep = jnp.broadcast_to(wi.reshape(bt, nq * IDX_HEADS, 1), (bt, nq * IDX_HEADS, LANES))
    pad = lambda t: jnp.pad(t.reshape(bt, nq, -1), ((0, 0), (0, NEW_PAD - nq), (0, 0)))
    upper = jnp.triu(jnp.ones((LANES, LANES), BF16), 1)
    per_b = lambda shape: pl.BlockSpec((1,) + shape, lambda b, p, pt: (b, 0, 0))
    assert n_pages % PAGES_PER_STEP == 0

    def pages(width):
        return [pl.BlockSpec((1, 1, PAGE_SIZE, width), lambda b, p, pt, s=s: (j, pt[b, p * PAGES_PER_STEP + s], 0, 0))
                for s in range(PAGES_PER_STEP)]

    o_all = pl.pallas_call(
        functools.partial(_dsa_sample_kernel, topk, nq),
        grid_spec=pltpu.PrefetchScalarGridSpec(
            num_scalar_prefetch=1,
            grid=(bt, n_pages // PAGES_PER_STEP),
            in_specs=[per_b((nq * ATT_HEADS, ATT_KV)), per_b((nq * IDX_HEADS, IDX_DH)), per_b((nq * IDX_HEADS, LANES)),
                      per_b((NEW_PAD, ATT_KV)), per_b((NEW_PAD, ATT_KV)), per_b((NEW_PAD, IDX_DH))]
            + pages(ATT_KV) + pages(ATT_KV) + pages(IDX_DH)
            + [pl.BlockSpec((LANES, LANES), lambda b, p, pt: (0, 0))],
            out_specs=per_b((nq * ATT_HEADS, ATT_KV)),
            scratch_shapes=[pltpu.VMEM((n_keys, ATT_KV), BF16), pltpu.VMEM((n_keys, ATT_KV), BF16),
                            pltpu.VMEM((n_keys, IDX_DH), BF16)]),
        out_shape=jax.ShapeDtypeStruct((bt, nq * ATT_HEADS, ATT_KV), F32),
        compiler_params=pltpu.CompilerParams(dimension_semantics=("arbitrary", "arbitrary"),
                                             vmem_limit_bytes=VMEM_LIMIT_BYTES),
        name="dsa_sample_attention",
    )(page_table, qbd, qi2, wrep, pad(k_new), pad(v_new), pad(ki_new),
      *([cache_k.reshape(cache_k.shape[0], n_pool, PAGE_SIZE, ATT_KV)] * PAGES_PER_STEP),
      *([cache_v.reshape(cache_v.shape[0], n_pool, PAGE_SIZE, ATT_KV)] * PAGES_PER_STEP),
      *([cache_kidx] * PAGES_PER_STEP), upper)
    o6 = o_all.reshape(bt, nq, ATT_KV_HEADS, G_PER_KV, ATT_KV_HEADS, ATT_DH)
    o = jnp.stack([o6[:, :, n, :, n, :] for n in range(ATT_KV_HEADS)], axis=2)
    return o.reshape(bt, nq, ATT_Q)


def rms_norm(x, eps=EPS):
    return x * lax.rsqrt(jnp.mean(x * x, axis=-1, keepdims=True) + eps)


def layer_norm(x, g, b, eps):
    mu = jnp.mean(x, axis=-1, keepdims=True)
    var = jnp.mean(jnp.square(x - mu), axis=-1, keepdims=True)
    return (x - mu) * lax.rsqrt(var + eps) * g + b


def rotary(x, pos):
    half = x.shape[-1] // 2
    inv_freq = jnp.power(ROPE_THETA, -jnp.arange(half, dtype=F32) / half)
    ang = pos.astype(F32)[:, None] * inv_freq[None, :]
    cos = jnp.cos(ang)[None, :, None, :]
    sin = jnp.sin(ang)[None, :, None, :]
    x1, x2 = x[..., :half], x[..., half:]
    return jnp.concatenate([x1 * cos - x2 * sin, x2 * cos + x1 * sin], axis=-1)


def gla_chunked(q, k, v, log_a, s0):
    bt, L, H, _ = q.shape
    dv = v.shape[-1]
    C = min(GLA_CHUNK, L)
    n = -(-L // C)
    pad = n * C - L

    def prep(t):
        t = jnp.pad(t, ((0, 0), (0, pad), (0, 0), (0, 0)))
        return t.reshape(bt, n, C, H, t.shape[-1]).transpose(1, 0, 3, 2, 4)

    qc, kc, vc, gc = prep(q), prep(k), prep(v), prep(log_a)
    bcum = jnp.cumsum(gc, axis=3)
    blast = bcum[:, :, :, -1:, :]
    q_e = qc * jnp.exp(bcum)
    k_e = kc * jnp.exp(-bcum)
    k_end = kc * jnp.exp(blast - bcum)
    causal = jnp.tril(jnp.ones((C, C), dtype=bool))
    att = jnp.where(causal, jnp.einsum('nbhtd,nbhsd->nbhts', q_e, k_e), 0.0)
    o_intra = jnp.einsum('nbhts,nbhsv->nbhtv', att, vc)
    decay = jnp.exp(blast[:, :, :, 0, :])
    kv = jnp.einsum('nbhsd,nbhsv->nbhdv', k_end, vc)

    def step(s, inp):
        dec, kv_i = inp
        return dec[..., None] * s + kv_i, s

    s_fin, starts = s0, []
    for i in range(n):
        s_fin, s_i = step(s_fin, (decay[i], kv[i]))
        starts.append(s_i)
    s_start = jnp.stack(starts)
    o = o_intra + jnp.einsum('nbhtd,nbhdv->nbhtv', q_e, s_start)
    o = o.transpose(1, 0, 3, 2, 4).reshape(bt, n * C, H, dv)[:, :L]
    return o, s_fin


def rwkv7_scan(r, w, k, v, a, b, s0):
    xs = tuple(t.transpose(1, 0, 2, 3) for t in (r, w, k, v, a, b))

    def step(s, inp):
        r_t, w_t, k_t, v_t, a_t, b_t = inp
        sa = jnp.einsum('bhvk,bhk->bhv', s, a_t)
        s = s * w_t[:, :, None, :] + sa[..., None] * b_t[:, :, None, :] + v_t[..., None] * k_t[:, :, None, :]
        return s, jnp.einsum('bhvk,bhk->bhv', s, r_t)

    s, ys = s0, []
    for t in range(r.shape[1]):
        s, y_t = step(s, tuple(x[t] for x in xs))
        ys.append(y_t)
    return jnp.stack(ys, axis=1), s


EVEN_SPLIT = (GLA_QK, GLA_QK, GLA_V, GLA_V, GLA_GATE_RANK, 3 * RW_W + RW_W_LORA + RW_A_LORA + RW_G_LORA)
ODD_SPLIT = (ATT_Q, ATT_KV, ATT_KV, IDX_HEADS * IDX_DH, IDX_DH, IDX_HEADS)


def even_mixer(parts, bt, L, shift_prev, s_gla, s_rw, j, P):
    q_a, k_a, v_a, r_a, g_low, pb_raw = [t.reshape(bt, L, -1) for t in parts]
    ga = lambda t, d: t.reshape(bt, L, GLA_HEADS, d)
    log_alpha = jax.nn.log_sigmoid(g_low @ P['gla_w_gate'][j] + P['gla_b_gate'][j]) / GLA_TAU
    gla = gla_chunked_pallas if L % GLA_CHUNK == 0 else gla_chunked
    o_a, s_gla_new = gla(ga(q_a, GLA_DK) * GLA_DK ** -0.5, ga(k_a, GLA_DK), ga(v_a, GLA_DV),
                         ga(log_alpha, GLA_DK), s_gla)
    prev = jnp.concatenate([shift_prev[:, None, :], pb_raw[:, :-1]], axis=1)
    pb = pb_raw + (prev - pb_raw) * P['rw_mu'][j]
    o1 = 3 * RW_W + RW_W_LORA
    r_b, k_b, v_b, w_low, a_low, gt_low = jnp.split(pb, [RW_W, 2 * RW_W, 3 * RW_W, o1, o1 + RW_A_LORA], axis=-1)
    w_log = -jax.nn.softplus(-(P['rw_w0'][j] + jnp.tanh(w_low) @ P['rw_w_decay'][j])) - 0.5
    log_decay = -jnp.exp(w_log)
    iclr = jax.nn.sigmoid(P['rw_a0'][j] + a_low @ P['rw_w_iclr'][j])
    gate = jax.nn.sigmoid(gt_low) @ P['rw_w_gate'][j]
    hb = lambda t: t.reshape(bt, L, RW_HEADS, RW_DH)
    kk = hb(k_b * P['rw_k_k'][j])
    kk = kk / jnp.maximum(jnp.sqrt(jnp.sum(kk * kk, axis=-1, keepdims=True)), 1e-12)
    k_b = k_b * (1 + (iclr - 1) * P['rw_k_a'][j])
    r_h, k_h, v_h, a_h = hb(r_b), hb(k_b), hb(v_b), hb(iclr)
    if L % RW_CHUNK == 0:
        y_b, s_rw_new = rwkv7_chunked(r_h, hb(log_decay), k_h, v_h, -kk, kk * a_h, s_rw)
    else:
        y_b, s_rw_new = rwkv7_scan(r_h, hb(jnp.exp(log_decay)), k_h, v_h, -kk, kk * a_h, s_rw)
    flat = lambda t: t.reshape(bt * L, -1)
    return [flat(t) for t in (o_a, r_a, y_b, r_b, k_b, v_b, gate)], pb_raw[:, -1], s_gla_new, s_rw_new


def odd_qkv(parts, bt, L, pos, j, P):
    q, k, v, qi, ki, wi = [t.reshape(bt, L, -1) for t in parts]
    q = rotary(rms_norm(q.reshape(bt, L, ATT_HEADS, ATT_DH)) * P['q_norm_g'][j], pos)
    k = rotary(rms_norm(k.reshape(bt, L, ATT_KV_HEADS, ATT_DH)) * P['k_norm_g'][j], pos)
    v = v.reshape(bt, L, ATT_KV_HEADS, ATT_DH)
    qi = rotary(qi.reshape(bt, L, IDX_HEADS, IDX_DH), pos)
    ki = rotary(layer_norm(ki, P['kidx_ln_g'][j], P['kidx_ln_b'][j], EPS)[:, :, None, :], pos)[:, :, 0, :]
    wi = wi * (IDX_HEADS * IDX_DH) ** -0.5
    return q, k, v, qi, ki, wi


def trunk(x, mods, pos, shift0, gla0, rw0, attend, P):
    bt, L, D = x.shape
    x2 = x.reshape(bt * L, D)
    glas, rws, shifts, ks, vs, kis = [], [], [], [], [], []
    for layer, mod in enumerate(mods):
        sh1, sc1, g1, sh2, sc2, g2 = jnp.split(mod, 6, axis=-1)
        j = layer // 2
        if layer % 2 == 0:
            parts = mod_mm(x2, sh1, sc1, P['w_in_even'][j], EVEN_SPLIT)
            mix_ops, shf, sg, sr = even_mixer(parts, bt, L, shift0[j], gla0[j], rw0[j], j, P)
            glas.append(sg)
            rws.append(sr)
            shifts.append(shf)
            x2 = even_out(*mix_ops, P['gla_norm_g'][j], P['rw_ln_g'][j], P['rw_ln_b'][j], P['rw_r_k'][j],
                          P['w_out_even'][j], x2, g1)
            x2 = ffn_block(x2, sh2, sc2, g2, None, P['ffn_w_gate'][j], P['ffn_w_up'][j], P['ffn_w_down'][j])
        else:
            parts = mod_mm(x2, sh1, sc1, P['w_in_odd'][j], ODD_SPLIT)
            q, k, v, qi, ki, wi = odd_qkv(parts, bt, L, pos, j, P)
            ks.append(k)
            vs.append(v)
            kis.append(ki)
            x2 = mm_res(attend(j, q, k, v, qi, ki, wi).reshape(bt * L, -1), P['w_out_odd'][j], x2, g1)
            x2 = ffn_block(x2, sh2, sc2, g2, P['moe_router'][j], P['moe_w_gate'][j], P['moe_w_up'][j],
                           P['moe_w_down'][j])
    return (x2.reshape(bt, L, D), jnp.stack(glas), jnp.stack(rws), jnp.stack(shifts), jnp.stack(ks), jnp.stack(vs),
            jnp.stack(kis))


def kernel(x_prompt, x_sample, state_gla, state_rwkv, state_shift, cache_k, cache_v, cache_kidx, page_table, c_prompt, c_sample, w_ada, b_ada, w_in_even, gla_w_gate, gla_b_gate, gla_norm_g, rw_mu, rw_w0, rw_w_decay, rw_a0, rw_w_iclr, rw_w_gate, rw_k_k, rw_k_a, rw_r_k, rw_ln_g, rw_ln_b, w_out_even, w_in_odd, q_norm_g, k_norm_g, kidx_ln_g, kidx_ln_b, w_out_odd, ffn_w_gate, ffn_w_up, ffn_w_down, moe_router, moe_w_gate, moe_w_up, moe_w_down):
    P = dict(w_ada=w_ada, b_ada=b_ada, w_in_even=w_in_even, gla_w_gate=gla_w_gate, gla_b_gate=gla_b_gate,
             gla_norm_g=gla_norm_g, rw_mu=rw_mu, rw_w0=rw_w0, rw_w_decay=rw_w_decay, rw_a0=rw_a0,
             rw_w_iclr=rw_w_iclr, rw_w_gate=rw_w_gate, rw_k_k=rw_k_k, rw_k_a=rw_k_a, rw_r_k=rw_r_k,
             rw_ln_g=rw_ln_g, rw_ln_b=rw_ln_b, w_out_even=w_out_even, w_in_odd=w_in_odd,
             q_norm_g=q_norm_g, k_norm_g=k_norm_g, kidx_ln_g=kidx_ln_g, kidx_ln_b=kidx_ln_b,
             w_out_odd=w_out_odd, ffn_w_gate=ffn_w_gate, ffn_w_up=ffn_w_up, ffn_w_down=ffn_w_down,
             moe_router=moe_router, moe_w_gate=moe_w_gate, moe_w_up=moe_w_up, moe_w_down=moe_w_down)
    n_even = state_gla.shape[0]
    b_p, seq = x_prompt.shape[0], x_prompt.shape[1]
    zero_gla = jnp.zeros((n_even, b_p) + state_gla.shape[2:], state_gla.dtype)
    zero_rw = jnp.zeros((n_even, b_p) + state_rwkv.shape[2:], state_rwkv.dtype)
    zero_shift = jnp.zeros((n_even, b_p) + state_shift.shape[2:], state_shift.dtype)
    pos_p = jnp.arange(seq, dtype=jnp.int32)
    past = page_table.shape[1] * PAGE_SIZE
    pos_s = past + jnp.arange(x_sample.shape[1], dtype=jnp.int32)

    def prompt_attend(j, q, k, v, qi, ki, wi):
        return dsa_prompt_attention(q, k, v, qi, ki, wi)

    def sample_attend(j, q, k, v, qi, ki, wi):
        return dsa_sample_attention(q, k, v, qi, ki, wi, cache_k, cache_v, cache_kidx, page_table, j)

    silu_c = jax.nn.silu(jnp.concatenate([c_prompt, c_sample], axis=0))
    mods = [mm(silu_c, w_ada[layer]) + b_ada[layer] for layer in range(w_ada.shape[0])]
    mods_p = [m[:b_p] for m in mods]
    mods_s = [m[b_p:] for m in mods]

    y_prompt, p_gla, p_rw, p_shift, p_k, p_v, p_kidx = trunk(
        x_prompt, mods_p, pos_p, zero_shift, zero_gla, zero_rw, prompt_attend, P)
    y_sample, s_gla, s_rw, s_shift, s_k, s_v, s_kidx = trunk(
        x_sample, mods_s, pos_s, state_shift, state_gla, state_rwkv, sample_attend, P)
    return (y_prompt, y_sample, p_gla, p_rw, p_shift, p_k, p_v, p_kidx, s_gla, s_rw, s_shift, s_k, s_v, s_kidx)
```

```python
import functools
import math

import jax
import jax.numpy as jnp
import numpy as np
from jax import lax
from jax.experimental import pallas as pl
from jax.experimental.pallas import tpu as pltpu

F32 = jnp.float32
BF16 = jnp.bfloat16

PAGE_SIZE = 128
GLA_HEADS, GLA_DK, GLA_DV, GLA_GATE_RANK, GLA_TAU, GLA_CHUNK = 4, 64, 128, 16, 16.0, 64
RW_HEADS, RW_DH, RW_W_LORA, RW_A_LORA, RW_G_LORA, RW_LN_EPS = 8, 64, 32, 32, 96, 64e-5
ATT_HEADS, ATT_KV_HEADS, ATT_DH = 16, 4, 64
IDX_HEADS, IDX_DH = 8, 64
TOPK_MAX, Q_BLOCK, ROPE_THETA = 256, 128, 10000.0
N_EXPERTS, TOP_K_EXPERTS = 8, 2
EPS = 1e-6
GLA_QK = GLA_HEADS * GLA_DK
GLA_V = GLA_HEADS * GLA_DV
RW_W = RW_HEADS * RW_DH
ATT_Q = ATT_HEADS * ATT_DH
ATT_KV = ATT_KV_HEADS * ATT_DH

VMEM_LIMIT_BYTES = 56 * 1024 * 1024
LANES = 128
ROW_TILE = 512


def _mm_kernel(x_ref, w_ref, o_ref):
    o_ref[...] = jnp.dot(x_ref[...].astype(BF16), w_ref[...], preferred_element_type=F32)


def mm(x, w):
    m, k = x.shape
    n = w.shape[1]
    tm = min(ROW_TILE, m)
    assert m % tm == 0
    return pl.pallas_call(
        _mm_kernel,
        grid=(m // tm,),
        in_specs=[pl.BlockSpec((tm, k), lambda i: (i, 0)), pl.BlockSpec((k, n), lambda i: (0, 0))],
        out_specs=pl.BlockSpec((tm, n), lambda i: (i, 0)),
        out_shape=jax.ShapeDtypeStruct((m, n), F32),
        compiler_params=pltpu.CompilerParams(dimension_semantics=("arbitrary",),
                                             vmem_limit_bytes=VMEM_LIMIT_BYTES),
    )(x, w.astype(BF16))


def _mod_operand(mod, n_rows, tm):
    b, d = mod.shape
    per_b = n_rows // b
    if per_b % tm == 0:
        tiles_per_b = per_b // tm
        return mod[:, None, :], pl.BlockSpec((1, 1, d), lambda i, *_: (i // tiles_per_b, 0, 0))
    rows = jnp.repeat(mod, per_b, axis=0).reshape(n_rows // tm, tm, d)
    return rows, pl.BlockSpec((1, tm, d), lambda i, *_: (i, 0, 0))


def _modulated(x, shift, scale):
    xn = x * lax.rsqrt(jnp.mean(x * x, axis=-1, keepdims=True) + EPS)
    return xn * (1.0 + scale) + shift


def _aligned_offsets(widths):
    offs, off = [], 0
    for w in widths:
        offs.append(off)
        off += -(-w // LANES) * LANES
    return offs, off


def _mod_mm_kernel(widths, x_ref, sh_ref, sc_ref, w_ref, *o_refs):
    h = _modulated(x_ref[...], sh_ref[0], sc_ref[0])
    y = jnp.dot(h.astype(BF16), w_ref[...], preferred_element_type=F32)
    for o_ref, off, width in zip(o_refs, _aligned_offsets(widths)[0], widths):
        o_ref[...] = y[:, off:off + width]


def mod_mm(x, shift, scale, w, widths):
    t, d = x.shape
    assert sum(widths) == w.shape[1]
    tm = min(ROW_TILE, t)
    sh, mod_spec = _mod_operand(shift, t, tm)
    sc, _ = _mod_operand(scale, t, tm)
    offs, n_pad = _aligned_offsets(widths)
    starts = np.cumsum([0] + list(widths))
    cols = [jnp.pad(w[:, starts[i]:starts[i + 1]], ((0, 0), (0, -widths[i] % LANES))) for i in range(len(widths))]
    w_pad = jnp.concatenate(cols, axis=1).astype(BF16)
    return pl.pallas_call(
        functools.partial(_mod_mm_kernel, tuple(widths)),
        grid=(t // tm,),
        in_specs=[pl.BlockSpec((tm, d), lambda i: (i, 0)), mod_spec, mod_spec, pl.BlockSpec((d, n_pad), lambda i: (0, 0))],
        out_specs=[pl.BlockSpec((tm, wd), lambda i: (i, 0)) for wd in widths],
        out_shape=[jax.ShapeDtypeStruct((t, wd), F32) for wd in widths],
        compiler_params=pltpu.CompilerParams(dimension_semantics=("arbitrary",),
                                             vmem_limit_bytes=VMEM_LIMIT_BYTES),
        name="modulated_in_proj",
    )(x, sh, sc, w_pad)


def _even_out_kernel(oa_ref, ra_ref, yb_ref, rb_ref, kb_ref, vb_ref, gate_ref, gn_ref, lng_ref, lnb_ref, rk_ref,
                     w_ref, x_ref, g_ref, o_ref):
    pieces = []
    for h in range(GLA_HEADS):
        sl = slice(h * GLA_DV, (h + 1) * GLA_DV)
        o = oa_ref[:, sl]
        r = ra_ref[:, sl]
        o = o * lax.rsqrt(jnp.mean(o * o, axis=-1, keepdims=True) + EPS) * gn_ref[:, sl]
        pieces.append(o * (r * (1.0 / (1.0 + jnp.exp(-r)))))
    for h in range(RW_HEADS):
        sl = slice(h * RW_DH, (h + 1) * RW_DH)
        y = yb_ref[:, sl]
        mu = jnp.mean(y, axis=-1, keepdims=True)
        yc = y - mu
        var = jnp.mean(yc * yc, axis=-1, keepdims=True)
        y = yc * lax.rsqrt(var + RW_LN_EPS) * lng_ref[:, sl] + lnb_ref[:, sl]
        bonus = jnp.sum(rb_ref[:, sl] * kb_ref[:, sl] * rk_ref[:, sl], axis=-1, keepdims=True)
        pieces.append((y + bonus * vb_ref[:, sl]) * gate_ref[:, sl])
    mix = jnp.concatenate(pieces, axis=1).astype(BF16)
    o_ref[...] = x_ref[...] + g_ref[0] * jnp.dot(mix, w_ref[...], preferred_element_type=F32)


def even_out(o_a, r_a, y_b, r_b, k_b, v_b, gate_b, gla_norm_g, ln_g, ln_b, r_k, w, x, gate):
    t, d = x.shape
    tm = min(ROW_TILE, t)
    g, mod_spec = _mod_operand(gate, t, tm)
    seq = pl.BlockSpec((tm, GLA_V), lambda i: (i, 0))
    par = pl.BlockSpec((1, GLA_V), lambda i: (0, 0))
    assert GLA_V == RW_W
    return pl.pallas_call(
        _even_out_kernel,
        grid=(t // tm,),
        in_specs=[seq] * 7 + [par] * 4 + [pl.BlockSpec((GLA_V + RW_W, d), lambda i: (0, 0)),
                                         pl.BlockSpec((tm, d), lambda i: (i, 0)), mod_spec],
        out_specs=pl.BlockSpec((tm, d), lambda i: (i, 0)),
        out_shape=jax.ShapeDtypeStruct((t, d), F32),
        compiler_params=pltpu.CompilerParams(dimension_semantics=("arbitrary",),
                                             vmem_limit_bytes=VMEM_LIMIT_BYTES),
        name="even_out_proj_residual",
    )(o_a, r_a, y_b, r_b, k_b, v_b, gate_b, jnp.tile(gla_norm_g, GLA_HEADS)[None], ln_g[None], ln_b[None],
      r_k.reshape(1, RW_W), w.astype(BF16), x, g)


def _mm_res_kernel(a_ref, w_ref, x_ref, g_ref, o_ref):
    y = jnp.dot(a_ref[...].astype(BF16), w_ref[...], preferred_element_type=F32)
    o_ref[...] = x_ref[...] + g_ref[0] * y


def mm_res(a, w, x, gate):
    t, k = a.shape
    d = w.shape[1]
    tm = min(ROW_TILE, t)
    g, mod_spec = _mod_operand(gate, t, tm)
    return pl.pallas_call(
        _mm_res_kernel,
        grid=(t // tm,),
        in_specs=[pl.BlockSpec((tm, k), lambda i: (i, 0)), pl.BlockSpec((k, d), lambda i: (0, 0)),
                  pl.BlockSpec((tm, d), lambda i: (i, 0)), mod_spec],
        out_specs=pl.BlockSpec((tm, d), lambda i: (i, 0)),
        out_shape=jax.ShapeDtypeStruct((t, d), F32),
        compiler_params=pltpu.CompilerParams(dimension_semantics=("arbitrary",),
                                             vmem_limit_bytes=VMEM_LIMIT_BYTES),
        name="out_proj_residual",
    )(a, w.astype(BF16), x, g)


def _ffn_kernel(routed, x_ref, sh_ref, sc_ref, gt_ref, wr_ref, wg_ref, wu_ref, wd_ref, o_ref, h_scr, cmb_scr, acc_scr):
    e = pl.program_id(1)
    lane = lax.broadcasted_iota(jnp.int32, cmb_scr.shape, 1)

    @pl.when(e == 0)
    def _():
        hb = _modulated(x_ref[...], sh_ref[0], sc_ref[0]).astype(BF16)
        h_scr[...] = hb
        acc_scr[...] = jnp.zeros(acc_scr.shape, F32)
        if routed:
            logits = jnp.dot(hb, wr_ref[...], preferred_element_type=F32)
            lg = jnp.where(lane < N_EXPERTS, logits, -jnp.inf)
            m1 = jnp.max(lg, axis=1, keepdims=True)
            i1 = jnp.min(jnp.where(lg == m1, lane, LANES), axis=1, keepdims=True)
            lg2 = jnp.where(lane == i1, -jnp.inf, lg)
            m2 = jnp.max(lg2, axis=1, keepdims=True)
            i2 = jnp.min(jnp.where(lg2 == m2, lane, LANES), axis=1, keepdims=True)
            t = jnp.exp(m2 - m1)
            p1 = 1.0 / (1.0 + t)
            cmb_scr[...] = jnp.where(lane == i1, p1, 0.0) + jnp.where(lane == i2, t * p1, 0.0)

    hb = h_scr[...]
    g = jnp.dot(hb, wg_ref[0], preferred_element_type=F32)
    u = jnp.dot(hb, wu_ref[0], preferred_element_type=F32)
    act = g * (1.0 / (1.0 + jnp.exp(-g))) * u
    y = jnp.dot(act.astype(BF16), wd_ref[0], preferred_element_type=F32)
    if routed:
        y = y * jnp.sum(jnp.where(lane == e, cmb_scr[...], 0.0), axis=1, keepdims=True)
    acc_scr[...] += y

    @pl.when(e == pl.num_programs(1) - 1)
    def _():
        o_ref[...] = x_ref[...] + gt_ref[0] * acc_scr[...]


FFN_BLOCK = 1408


def ffn_block(x, shift, scale, gate, w_router, w_gate, w_up, w_down):
    t, d = x.shape
    routed = w_router is not None
    tm = min(ROW_TILE, t)
    sh, mod_spec = _mod_operand(shift, t, tm)
    sc, _ = _mod_operand(scale, t, tm)
    gt, _ = _mod_operand(gate, t, tm)
    if routed:
        n_e, _, f = w_gate.shape
        assert f == FFN_BLOCK and n_e == N_EXPERTS
        wr = jnp.pad(w_router, ((0, 0), (0, LANES - n_e))).astype(BF16)
        up_spec = pl.BlockSpec((1, d, f), lambda i, e: (e, 0, 0))
        down_spec = pl.BlockSpec((1, f, d), lambda i, e: (e, 0, 0))
    else:
        f_all = w_gate.shape[1]
        assert f_all % FFN_BLOCK == 0
        n_e, f = f_all // FFN_BLOCK, FFN_BLOCK
        wr = jnp.zeros((d, LANES), BF16)
        w_gate, w_up, w_down = w_gate[None], w_up[None], w_down[None]
        up_spec = pl.BlockSpec((1, d, f), lambda i, e: (0, 0, e))
        down_spec = pl.BlockSpec((1, f, d), lambda i, e: (0, e, 0))
    return pl.pallas_call(
        functools.partial(_ffn_kernel, routed),
        grid=(t // tm, n_e),
        in_specs=[pl.BlockSpec((tm, d), lambda i, e: (i, 0)), mod_spec, mod_spec, mod_spec,
                  pl.BlockSpec((d, LANES), lambda i, e: (0, 0)), up_spec, up_spec, down_spec],
        out_specs=pl.BlockSpec((tm, d), lambda i, e: (i, 0)),
        out_shape=jax.ShapeDtypeStruct((t, d), F32),
        scratch_shapes=[pltpu.VMEM((tm, d), BF16), pltpu.VMEM((tm, LANES), F32), pltpu.VMEM((tm, d), F32)],
        compiler_params=pltpu.CompilerParams(dimension_semantics=("arbitrary", "arbitrary"),
                                             vmem_limit_bytes=VMEM_LIMIT_BYTES),
        name="routed_experts" if routed else "dense_swiglu",
    )(x, sh, sc, gt, wr, w_gate.astype(BF16), w_up.astype(BF16), w_down.astype(BF16))


RW_CHUNK = 64


def _bdot(a, b):
    return jnp.dot(a.astype(BF16), b.astype(BF16), preferred_element_type=F32)


def _bdot_nt(a, b):
    return lax.dot_general(a.astype(BF16), b.astype(BF16), (((1,), (1,)), ((), ())), preferred_element_type=F32)


def _bdot_tn(a, b):
    return lax.dot_general(a.astype(BF16), b.astype(BF16), (((0,), (0,)), ((), ())), preferred_element_type=F32)


def _split3(x):
    hi = x.astype(BF16)
    r1 = x - hi.astype(F32)
    mid = r1.astype(BF16)
    lo = (r1 - mid.astype(F32)).astype(BF16)
    return hi, mid, lo


def _rwkv_chunk_heads(r, lw, k, v, a, b, h0, tri_bf, strict, incl, eye):
    nh = len(r)
    hs = range(nh)
    cum = []
    for i in hs:
        hi, mid, lo = _split3(lw[i])
        cum.append(jnp.dot(tri_bf, hi, preferred_element_type=F32) + jnp.dot(tri_bf, mid, preferred_element_type=F32)
                   + jnp.dot(tri_bf, lo, preferred_element_type=F32))
    cum_last = [cum[i][RW_CHUNK - 1:RW_CHUNK, :] for i in hs]
    e_neg = [jnp.exp(-cum[i]) for i in hs]
    e_end = [jnp.exp(cum_last[i] - cum[i]) for i in hs]
    a_t = [a[i] * jnp.exp(cum[i] - lw[i]) for i in hs]
    r_t = [r[i] * jnp.exp(cum[i]) for i in hs]
    k_t = [k[i] * e_neg[i] for i in hs]
    b_t = [b[i] * e_neg[i] for i in hs]
    k_e = [k[i] * e_end[i] for i in hs]
    b_e = [b[i] * e_end[i] for i in hs]
    a_ab = [jnp.where(strict, _bdot_nt(a_t[i], b_t[i]), 0.0) for i in hs]
    a_ak = [jnp.where(strict, _bdot_nt(a_t[i], k_t[i]), 0.0) for i in hs]
    a_rb = [jnp.where(incl, _bdot_nt(r_t[i], b_t[i]), 0.0) for i in hs]
    a_rk = [jnp.where(incl, _bdot_nt(r_t[i], k_t[i]), 0.0) for i in hs]
    x = [eye + a_ab[i] for i in hs]
    y = list(a_ab)
    for _ in range(int(math.log2(RW_CHUNK)) - 1):
        y = [_bdot(y[i], y[i]) for i in hs]
        x = [x[i] + _bdot(x[i], y[i]) for i in hs]
    a_p = [_bdot(x[i], a_t[i]) for i in hs]
    akv = [_bdot(a_ak[i], v[i]) for i in hs]
    v_p = [_bdot(x[i], akv[i]) for i in hs]
    r_p = [r_t[i] + _bdot(a_rb[i], a_p[i]) for i in hs]
    y_p = [_bdot(a_rk[i], v[i]) + _bdot(a_rb[i], v_p[i]) for i in hs]
    m_lr = [_bdot_tn(b_e[i], a_p[i]) for i in hs]
    g = [_bdot_tn(k_e[i], v[i]) + _bdot_tn(b_e[i], v_p[i]) for i in hs]
    w_col = [jnp.sum(jnp.where(eye, jnp.exp(cum_last[i]), 0.0), axis=1, keepdims=True) for i in hs]
    y_out = [_bdot(r_p[i], h0[i]) + y_p[i] for i in hs]
    h_new = [w_col[i] * h0[i] + _bdot(m_lr[i], h0[i]) + g[i] for i in hs]
    return y_out, h_new


RW_HEAD_GROUP = 16


def _rwkv_kernel(r_ref, lw_ref, k_ref, v_ref, a_ref, b_ref, h0_ref, y_ref, hout_ref, h_scr):
    c = pl.program_id(0)

    @pl.when(c == 0)
    def _():
        h_scr[...] = h0_ref[...]

    rows = lax.broadcasted_iota(jnp.int32, (RW_CHUNK, RW_CHUNK), 0)
    cols = lax.broadcasted_iota(jnp.int32, (RW_CHUNK, RW_CHUNK), 1)
    strict = rows > cols
    incl = rows >= cols
    eye = rows == cols
    tri_bf = jnp.where(incl, 1.0, 0.0).astype(BF16)
    n_b, _, width = r_ref.shape
    n = h_scr.shape[-1]
    heads = [(b, h) for b in range(n_b) for h in range(width // n)]
    for i0 in range(0, len(heads), RW_HEAD_GROUP):
        grp = heads[i0:i0 + RW_HEAD_GROUP]
        seqs = [[ref[b, :, h * n:(h + 1) * n] for b, h in grp] for ref in (r_ref, lw_ref, k_ref, v_ref, a_ref, b_ref)]
        states = [h_scr[b * (width // n) + h] for b, h in grp]
        y_out, h_new = _rwkv_chunk_heads(*seqs, states, tri_bf, strict, incl, eye)
        for j, (b, h) in enumerate(grp):
            y_ref[b, :, h * n:(h + 1) * n] = y_out[j]
            h_scr[b * (width // n) + h] = h_new[j]

    @pl.when(c == pl.num_programs(0) - 1)
    def _():
        hout_ref[...] = h_scr[...]


def rwkv7_chunked(r, lw, k, v, a, b, s0):
    bt, L, H, N = r.shape
    assert L % RW_CHUNK == 0
    flat = lambda t: t.reshape(bt, L, H * N)
    h0 = s0.transpose(0, 1, 3, 2).reshape(bt * H, N, N)
    seq_spec = pl.BlockSpec((bt, RW_CHUNK, H * N), lambda c: (0, c, 0))
    st_spec = pl.BlockSpec((bt * H, N, N), lambda c: (0, 0, 0))
    y, h_fin = pl.pallas_call(
        _rwkv_kernel,
        grid=(L // RW_CHUNK,),
        in_specs=[seq_spec] * 6 + [st_spec],
        out_specs=[seq_spec, st_spec],
        out_shape=[jax.ShapeDtypeStruct((bt, L, H * N), F32), jax.ShapeDtypeStruct((bt * H, N, N), F32)],
        scratch_shapes=[pltpu.VMEM((bt * H, N, N), F32)],
        compiler_params=pltpu.CompilerParams(dimension_semantics=("arbitrary",),
                                             vmem_limit_bytes=VMEM_LIMIT_BYTES),
        name="rwkv7_chunked",
    )(flat(r), flat(lw), flat(k), flat(v), flat(a), flat(b), h0)
    return y.reshape(bt, L, H, N), h_fin.reshape(bt, H, N, N).transpose(0, 1, 3, 2)


def _gla_kernel(q_ref, k_ref, v_ref, g_ref, s0_ref, o_ref, sout_ref, s_scr):
    c = pl.program_id(0)

    @pl.when(c == 0)
    def _():
        s_scr[...] = s0_ref[...]

    rows = lax.broadcasted_iota(jnp.int32, (GLA_CHUNK, GLA_CHUNK), 0)
    cols = lax.broadcasted_iota(jnp.int32, (GLA_CHUNK, GLA_CHUNK), 1)
    incl = rows >= cols
    eye = lax.broadcasted_iota(jnp.int32, (GLA_DK, GLA_DK), 0) == lax.broadcasted_iota(jnp.int32, (GLA_DK, GLA_DK), 1)
    tri_bf = jnp.where(incl, 1.0, 0.0).astype(BF16)
    n_b, _, qk_width = q_ref.shape
    n_h = qk_width // GLA_DK
    heads = [(b, h) for b in range(n_b) for h in range(n_h)]
    hs = range(len(heads))
    qs = [q_ref[b, :, h * GLA_DK:(h + 1) * GLA_DK] for b, h in heads]
    ks = [k_ref[b, :, h * GLA_DK:(h + 1) * GLA_DK] for b, h in heads]
    vs = [v_ref[b, :, h * GLA_DV:(h + 1) * GLA_DV] for b, h in heads]
    bcum = []
    for b, h in heads:
        hi, mid, lo = _split3(g_ref[b, :, h * GLA_DK:(h + 1) * GLA_DK])
        bcum.append(jnp.dot(tri_bf, hi, preferred_element_type=F32) + jnp.dot(tri_bf, mid, preferred_element_type=F32)
                    + jnp.dot(tri_bf, lo, preferred_element_type=F32))
    blast = [bcum[i][GLA_CHUNK - 1:GLA_CHUNK, :] for i in hs]
    q_e = [qs[i] * jnp.exp(bcum[i]) for i in hs]
    k_e = [ks[i] * jnp.exp(-bcum[i]) for i in hs]
    k_end = [ks[i] * jnp.exp(blast[i] - bcum[i]) for i in hs]
    att = [jnp.where(incl, _bdot_nt(q_e[i], k_e[i]), 0.0) for i in hs]
    s_old = [s_scr[i] for i in hs]
    o = [_bdot(att[i], vs[i]) + _bdot(q_e[i], s_old[i]) for i in hs]
    kv = [_bdot_tn(k_end[i], vs[i]) for i in hs]
    dec = [jnp.sum(jnp.where(eye, jnp.exp(blast[i]), 0.0), axis=1, keepdims=True) for i in hs]
    for i, (b, h) in enumerate(heads):
        o_ref[b, :, h * GLA_DV:(h + 1) * GLA_DV] = o[i]
        s_scr[i] = dec[i] * s_old[i] + kv[i]

    @pl.when(c == pl.num_programs(0) - 1)
    def _():
        sout_ref[...] = s_scr[...]


def gla_chunked_pallas(q, k, v, log_a, s0):
    bt, L, H, dk = q.shape
    dv = v.shape[-1]
    assert L % GLA_CHUNK == 0
    flat = lambda t: t.reshape(bt, L, H * t.shape[-1])
    qk_spec = pl.BlockSpec((bt, GLA_CHUNK, H * dk), lambda c: (0, c, 0))
    v_spec = pl.BlockSpec((bt, GLA_CHUNK, H * dv), lambda c: (0, c, 0))
    st_spec = pl.BlockSpec((bt * H, dk, dv), lambda c: (0, 0, 0))
    o, s_fin = pl.pallas_call(
        _gla_kernel,
        grid=(L // GLA_CHUNK,),
        in_specs=[qk_spec, qk_spec, v_spec, qk_spec, st_spec],
        out_specs=[v_spec, st_spec],
        out_shape=[jax.ShapeDtypeStruct((bt, L, H * dv), F32), jax.ShapeDtypeStruct((bt * H, dk, dv), F32)],
        scratch_shapes=[pltpu.VMEM((bt * H, dk, dv), F32)],
        compiler_params=pltpu.CompilerParams(dimension_semantics=("arbitrary",),
                                             vmem_limit_bytes=VMEM_LIMIT_BYTES),
        name="gla_chunked",
    )(flat(q), flat(k), flat(v), flat(log_a), s0.reshape(bt * H, dk, dv))
    return o.reshape(bt, L, H, dv), s_fin.reshape(bt, H, dk, dv)


DSA_QB = 128
DSA_KC = 512
MASK_NEG = -1e30
INT_MIN = -2 ** 31
INT16_MIN = -2 ** 15
ONE16 = np.int16(1)
ZERO16 = np.int16(0)
ONE = np.float32(1.0)
ZERO = np.float32(0.0)
G_PER_KV = ATT_HEADS // ATT_KV_HEADS


def _sortable_key(x):
    bits = lax.bitcast_convert_type(x + 0.0, jnp.int32)
    return bits ^ ((bits >> 31) & 0x7FFFFFFF)


def _dsa_kernel(topk, q_ref, qi_ref, wi_ref, k_ref, v_ref, kit_ref, upper_ref, o_ref,
                wb_scr, key_scr, khi_scr, klo_scr, m_scr, acc_scr):
    i = pl.program_id(1)
    n_chunks = (i * DSA_QB) // DSA_KC + 1
    nt = DSA_KC // LANES
    q_pos = i * DSA_QB + lax.broadcasted_iota(jnp.int32, (DSA_QB, LANES), 0)
    lane = lax.broadcasted_iota(jnp.int32, (DSA_QB, LANES), 1)

    for h in range(IDX_HEADS):
        wb_scr[h] = jnp.broadcast_to(wi_ref[0, :, h:h + 1], (DSA_QB, LANES))

    def score_chunk(c, carry):
        dots = jnp.dot(qi_ref[0, 0], kit_ref[0, c], preferred_element_type=F32)
        for jt in range(nt):
            sc = jnp.zeros((DSA_QB, LANES), F32)
            for h in range(IDX_HEADS):
                d = dots[h * DSA_QB:(h + 1) * DSA_QB, jt * LANES:(jt + 1) * LANES]
                sc = sc + jnp.maximum(d, 0.0) * wb_scr[h]
            k_pos = c * DSA_KC + jt * LANES + lane
            sc = jnp.where(k_pos <= q_pos, sc, -jnp.inf)
            key = _sortable_key(sc)
            key_scr[c, :, jt * LANES:(jt + 1) * LANES] = key
            khi_scr[c, :, jt * LANES:(jt + 1) * LANES] = (key >> 16).astype(jnp.int16)
        return carry

    lax.fori_loop(0, n_chunks, score_chunk, 0)

    def count16(scr, cand, strict):
        cand_b = jnp.broadcast_to(cand.astype(jnp.int16), (DSA_QB, LANES))

        def body(c, acc):
            for jt in range(nt):
                tile = scr[c, :, jt * LANES:(jt + 1) * LANES]
                hit = (tile > cand_b) if strict else (tile >= cand_b)
                acc = acc + jnp.where(hit, ONE16, ZERO16)
            return acc

        acc = lax.fori_loop(0, n_chunks, body, jnp.zeros((DSA_QB, LANES), jnp.int16))
        return jnp.sum(acc.astype(F32), axis=1, keepdims=True)

    def bisect16(scr, k_need):
        cur = jnp.where(count16(scr, jnp.zeros((DSA_QB, 1), jnp.int32), False) >= k_need, 0, INT16_MIN)
        cur = cur.astype(jnp.int32)

        def bit_step(p, cur):
            cand = cur | (jnp.int32(1) << (14 - p))
            return jnp.where(count16(scr, cand, False) >= k_need, cand, cur)

        return lax.fori_loop(0, 15, bit_step, cur)

    kf = float(topk)
    thr_hi = bisect16(khi_scr, kf)
    above = count16(khi_scr, thr_hi, True)
    thr_hi_b = jnp.broadcast_to(thr_hi, (DSA_QB, LANES))

    def low_halves(c, carry):
        for jt in range(nt):
            key = key_scr[c, :, jt * LANES:(jt + 1) * LANES]
            lo = (key & 0xFFFF) - 32768
            klo_scr[c, :, jt * LANES:(jt + 1) * LANES] = jnp.where((key >> 16) == thr_hi_b, lo, INT16_MIN).astype(jnp.int16)
        return carry

    lax.fori_loop(0, n_chunks, low_halves, 0)
    thr_lo = bisect16(klo_scr, kf - above)
    n_gt = above + count16(klo_scr, thr_lo, True)
    need = kf - n_gt
    thr = (thr_hi << 16) | (thr_lo + 32768)

    m_scr[...] = jnp.full(m_scr.shape, MASK_NEG, F32)
    acc_scr[...] = jnp.zeros(acc_scr.shape, F32)

    def attend_chunk(c, tie_carry):
        k0 = pl.multiple_of(c * DSA_KC, DSA_KC)
        keys = key_scr[c]
        thr_c = jnp.broadcast_to(thr, (DSA_QB, DSA_KC))
        eq = keys == thr_c
        eq_f = jnp.where(eq, ONE, ZERO)
        rank = tie_carry + jnp.dot(eq_f.astype(BF16), upper_ref[...], preferred_element_type=F32)
        k_pos = k0 + lax.broadcasted_iota(jnp.int32, (DSA_QB, DSA_KC), 1)
        qp = i * DSA_QB + lax.broadcasted_iota(jnp.int32, (DSA_QB, DSA_KC), 0)
        sel = jnp.where(keys > thr_c, ONE, jnp.where(rank < need, eq_f, ZERO))
        bias = jnp.where((sel > 0.5) & (k_pos <= qp), ZERO, np.float32(MASK_NEG))
        v_c = v_ref[0, pl.ds(k0, DSA_KC), :]
        rows = G_PER_KV * DSA_QB
        for n in range(ATT_KV_HEADS):
            s = lax.dot_general(q_ref[0, 0, n], k_ref[0, n, pl.ds(k0, DSA_KC), :], (((1,), (1,)), ((), ())),
                                preferred_element_type=F32)
            s = (s.reshape(G_PER_KV, DSA_QB, DSA_KC) + bias[None]).reshape(rows, DSA_KC)
            tiles = [s[:, jt * LANES:(jt + 1) * LANES] for jt in range(nt)]
            tile_max = functools.reduce(jnp.maximum, tiles)
            m_old = m_scr[n]
            m_new = jnp.maximum(m_old, jnp.max(tile_max, axis=1, keepdims=True))
            alpha = jnp.exp(m_old - m_new)
            p = jnp.concatenate([jnp.exp((t - m_new).astype(BF16)) for t in tiles], axis=1)
            pv = jnp.dot(p, v_c[:, n * LANES:(n + 1) * LANES], preferred_element_type=F32)
            acc_scr[n] = alpha * acc_scr[n] + pv
            m_scr[n] = m_new
        return tie_carry + jnp.sum(eq_f, axis=1, keepdims=True)

    lax.fori_loop(0, n_chunks, attend_chunk, jnp.zeros((DSA_QB, 1), F32))

    for n in range(ATT_KV_HEADS):
        acc = acc_scr[n]
        o_n = acc[:, :ATT_DH] / acc[:, ATT_DH:]
        for g in range(G_PER_KV):
            hd = n * G_PER_KV + g
            o_ref[0, :, hd * ATT_DH:(hd + 1) * ATT_DH] = o_n[g * DSA_QB:(g + 1) * DSA_QB, :]


def dsa_prompt_attention(q, k, v, qi, ki, wi):
    bt, L = q.shape[:2]
    assert L % DSA_KC == 0 and L // 4 >= 1
    topk = min(TOPK_MAX, L // 4)
    nb = L // DSA_QB
    nc = L // DSA_KC
    kit = ki.astype(BF16).reshape(bt, nc, DSA_KC, IDX_DH).transpose(0, 1, 3, 2)
    upper = jnp.triu(jnp.ones((DSA_KC, DSA_KC), BF16), 1)
    rows = G_PER_KV * DSA_QB
    q_blocks = (q * ATT_DH ** -0.5).astype(BF16).reshape(bt, nb, DSA_QB, ATT_KV_HEADS, G_PER_KV, ATT_DH)
    q_blocks = q_blocks.transpose(0, 1, 3, 4, 2, 5).reshape(bt, nb, ATT_KV_HEADS, rows, ATT_DH)
    qi_blocks = qi.astype(BF16).reshape(bt, nb, DSA_QB, IDX_HEADS, IDX_DH).transpose(0, 1, 3, 2, 4)
    qi_blocks = qi_blocks.reshape(bt, nb, IDX_HEADS * DSA_QB, IDX_DH)
    v_aug = jnp.concatenate([v.astype(BF16), jnp.ones(v.shape, BF16)], axis=-1).reshape(bt, L, ATT_KV_HEADS * LANES)
    return pl.pallas_call(
        functools.partial(_dsa_kernel, topk),
        grid=(bt, nb),
        in_specs=[
            pl.BlockSpec((1, 1, ATT_KV_HEADS, rows, ATT_DH), lambda b, i: (b, i, 0, 0, 0)),
            pl.BlockSpec((1, 1, IDX_HEADS * DSA_QB, IDX_DH), lambda b, i: (b, i, 0, 0)),
            pl.BlockSpec((1, DSA_QB, IDX_HEADS), lambda b, i: (b, i, 0)),
            pl.BlockSpec((1, ATT_KV_HEADS, L, ATT_DH), lambda b, i: (b, 0, 0, 0)),
            pl.BlockSpec((1, L, ATT_KV_HEADS * LANES), lambda b, i: (b, 0, 0)),
            pl.BlockSpec((1, nc, IDX_DH, DSA_KC), lambda b, i: (b, 0, 0, 0)),
            pl.BlockSpec((DSA_KC, DSA_KC), lambda b, i: (0, 0)),
        ],
        out_specs=pl.BlockSpec((1, DSA_QB, ATT_Q), lambda b, i: (b, i, 0)),
        out_shape=jax.ShapeDtypeStruct((bt, L, ATT_Q), F32),
        scratch_shapes=[
            pltpu.VMEM((IDX_HEADS, DSA_QB, LANES), F32),
            pltpu.VMEM((nc, DSA_QB, DSA_KC), jnp.int32),
            pltpu.VMEM((nc, DSA_QB, DSA_KC), jnp.int16),
            pltpu.VMEM((nc, DSA_QB, DSA_KC), jnp.int16),
            pltpu.VMEM((ATT_KV_HEADS, rows, LANES), F32),
            pltpu.VMEM((ATT_KV_HEADS, rows, LANES), F32),
        ],
        compiler_params=pltpu.CompilerParams(dimension_semantics=("arbitrary", "arbitrary"),
                                             vmem_limit_bytes=VMEM_LIMIT_BYTES),
        name="dsa_prompt_attention",
    )(q_blocks, qi_blocks, wi,
      k.astype(BF16).transpose(0, 2, 1, 3), v_aug, kit, upper)


NEW_PAD = 16
PAGES_PER_STEP = 16


def _dsa_sample_kernel(topk, n_new, pt_ref, qbd_ref, qi_ref, wrep_ref, knew_ref, vnew_ref, kinew_ref, *rest):
    pages = rest[:3 * PAGES_PER_STEP]
    upper_ref, o_ref, kbuf, vbuf, kibuf = rest[3 * PAGES_PER_STEP:]
    p = pl.program_id(1)
    n_steps = pl.num_programs(1)
    past = n_steps * PAGES_PER_STEP * PAGE_SIZE
    n_keys = kbuf.shape[0]
    nq = qbd_ref.shape[1] // ATT_HEADS
    for buf, refs in zip((kbuf, vbuf, kibuf), (pages[:PAGES_PER_STEP], pages[PAGES_PER_STEP:2 * PAGES_PER_STEP],
                                               pages[2 * PAGES_PER_STEP:])):
        for s, ref in enumerate(refs):
            row0 = pl.multiple_of((p * PAGES_PER_STEP + s) * PAGE_SIZE, PAGE_SIZE)
            buf[pl.ds(row0, PAGE_SIZE), :] = ref[0, 0].astype(BF16)

    @pl.when(p == n_steps - 1)
    def _():
        tail = n_keys - past
        kbuf[past:, :] = jnp.zeros((tail, ATT_KV), BF16)
        vbuf[past:, :] = jnp.zeros((tail, ATT_KV), BF16)
        kibuf[past:, :] = jnp.zeros((tail, IDX_DH), BF16)
        kbuf[past:past + NEW_PAD, :] = knew_ref[0].astype(BF16)
        vbuf[past:past + NEW_PAD, :] = vnew_ref[0].astype(BF16)
        kibuf[past:past + NEW_PAD, :] = kinew_ref[0].astype(BF16)

        dots = lax.dot_general(qi_ref[0], kibuf[...], (((1,), (1,)), ((), ())), preferred_element_type=F32)
        k_idx = lax.broadcasted_iota(jnp.int32, (8, n_keys), 1)
        q_row = lax.broadcasted_iota(jnp.int32, (8, n_keys), 0)
        valid = (k_idx - past <= q_row) & (q_row < nq)
        rows = []
        for qn in range(nq):
            d = jnp.maximum(dots[qn * IDX_HEADS:(qn + 1) * IDX_HEADS], 0.0)
            w = wrep_ref[0, qn * IDX_HEADS:(qn + 1) * IDX_HEADS, :]
            parts = []
            for jt in range(n_keys // LANES):
                parts.append(jnp.sum(d[:, jt * LANES:(jt + 1) * LANES] * w, axis=0, keepdims=True))
            rows.append(jnp.concatenate(parts, axis=1))
        rows.append(jnp.zeros((8 - nq, n_keys), F32))
        sc = jnp.where(valid, jnp.concatenate(rows, axis=0), -jnp.inf)
        keys = _sortable_key(sc)

        kf = float(topk)

        def count_ge(cand):
            return jnp.sum(jnp.where(keys >= cand, ONE, ZERO), axis=1, keepdims=True)

        def digit_step(i, cur_u):
            shift = 28 - 4 * i
            digit = jnp.zeros((8, 1), jnp.int32)
            for jd in range(1, 16):
                cand = (cur_u + (jnp.int32(jd) << shift)) ^ INT_MIN
                digit = digit + jnp.where(count_ge(cand) >= kf, 1, 0)
            return cur_u + (digit << shift)

        thr = lax.fori_loop(0, 8, digit_step, jnp.zeros((8, 1), jnp.int32)) ^ INT_MIN
        n_gt = jnp.sum(jnp.where(keys > thr, ONE, ZERO), axis=1, keepdims=True)
        need = kf - n_gt
        eq_f = jnp.where(keys == thr, ONE, ZERO)
        carry = jnp.zeros((8, 1), F32)
        ranks = []
        for jt in range(n_keys // LANES):
            e = eq_f[:, jt * LANES:(jt + 1) * LANES]
            ranks.append(carry + jnp.dot(e.astype(BF16), upper_ref[...], preferred_element_type=F32))
            carry = carry + jnp.sum(e, axis=1, keepdims=True)
        rank = jnp.concatenate(ranks, axis=1)
        sel = jnp.where(keys > thr, ONE, jnp.where(rank < need, eq_f, ZERO))
        bias = jnp.where((sel > 0.5) & valid, ZERO, np.float32(MASK_NEG))

        s = lax.dot_general(qbd_ref[0], kbuf[...], (((1,), (1,)), ((), ())), preferred_element_type=F32)
        s = jnp.concatenate([s[qn * ATT_HEADS:(qn + 1) * ATT_HEADS] + bias[qn:qn + 1] for qn in range(nq)], axis=0)
        m = jnp.max(s, axis=1, keepdims=True)
        pr = jnp.exp(s - m)
        l = jnp.sum(pr, axis=1, keepdims=True)
        o_ref[0] = jnp.dot(pr.astype(BF16), vbuf[...], preferred_element_type=F32) / l


def dsa_sample_attention(q, k_new, v_new, qi, ki_new, wi, cache_k, cache_v, cache_kidx, page_table, j):
    bt, nq = q.shape[:2]
    n_pages = page_table.shape[1]
    past = n_pages * PAGE_SIZE
    assert nq <= 8 and nq <= NEW_PAD
    topk = min(TOPK_MAX, (past + nq) // 4)
    n_keys = past + LANES
    n_pool = cache_k.shape[1]
    qs = (q * ATT_DH ** -0.5).astype(BF16).reshape(bt, nq, ATT_KV_HEADS, G_PER_KV, 1, ATT_DH)
    eye = jnp.eye(ATT_KV_HEADS, dtype=BF16)[None, None, :, None, :, None]
    qbd = (qs * eye).reshape(bt, nq * ATT_HEADS, ATT_KV)
    qi2 = qi.astype(BF16).reshape(bt, nq * IDX_HEADS, IDX_DH)
    wrep = jnp.broadcast_to(wi.reshape(bt, nq * IDX_HEADS, 1), (bt, nq * IDX_HEADS, LANES))
    pad = lambda t: jnp.pad(t.reshape(bt, nq, -1), ((0, 0), (0, NEW_PAD - nq), (0, 0)))
    upper = jnp.triu(jnp.ones((LANES, LANES), BF16), 1)
    per_b = lambda shape: pl.BlockSpec((1,) + shape, lambda b, p, pt: (b, 0, 0))
    assert n_pages % PAGES_PER_STEP == 0

    def pages(width):
        return [pl.BlockSpec((1, 1, PAGE_SIZE, width), lambda b, p, pt, s=s: (j, pt[b, p * PAGES_PER_STEP + s], 0, 0))
                for s in range(PAGES_PER_STEP)]

    o_all = pl.pallas_call(
        functools.partial(_dsa_sample_kernel, topk, nq),
        grid_spec=pltpu.PrefetchScalarGridSpec(
            num_scalar_prefetch=1,
            grid=(bt, n_pages // PAGES_PER_STEP),
            in_specs=[per_b((nq * ATT_HEADS, ATT_KV)), per_b((nq * IDX_HEADS, IDX_DH)), per_b((nq * IDX_HEADS, LANES)),
                      per_b((NEW_PAD, ATT_KV)), per_b((NEW_PAD, ATT_KV)), per_b((NEW_PAD, IDX_DH))]
            + pages(ATT_KV) + pages(ATT_KV) + pages(IDX_DH)
            + [pl.BlockSpec((LANES, LANES), lambda b, p, pt: (0, 0))],
            out_specs=per_b((nq * ATT_HEADS, ATT_KV)),
            scratch_shapes=[pltpu.VMEM((n_keys, ATT_KV), BF16), pltpu.VMEM((n_keys, ATT_KV), BF16),
                            pltpu.VMEM((n_keys, IDX_DH), BF16)]),
        out_shape=jax.ShapeDtypeStruct((bt, nq * ATT_HEADS, ATT_KV), F32),
        compiler_params=pltpu.CompilerParams(dimension_semantics=("arbitrary", "arbitrary"),
                                             vmem_limit_bytes=VMEM_LIMIT_BYTES),
        name="dsa_sample_attention",
    )(page_table, qbd, qi2, wrep, pad(k_new), pad(v_new), pad(ki_new),
      *([cache_k.reshape(cache_k.shape[0], n_pool, PAGE_SIZE, ATT_KV)] * PAGES_PER_STEP),
      *([cache_v.reshape(cache_v.shape[0], n_pool, PAGE_SIZE, ATT_KV)] * PAGES_PER_STEP),
      *([cache_kidx] * PAGES_PER_STEP), upper)
    o6 = o_all.reshape(bt, nq, ATT_KV_HEADS, G_PER_KV, ATT_KV_HEADS, ATT_DH)
    o = jnp.stack([o6[:, :, n, :, n, :] for n in range(ATT_KV_HEADS)], axis=2)
    return o.reshape(bt, nq, ATT_Q)


def rms_norm(x, eps=EPS):
    return x * lax.rsqrt(jnp.mean(x * x, axis=-1, keepdims=True) + eps)


def layer_norm(x, g, b, eps):
    mu = jnp.mean(x, axis=-1, keepdims=True)
    var = jnp.mean(jnp.square(x - mu), axis=-1, keepdims=True)
    return (x - mu) * lax.rsqrt(var + eps) * g + b


def rotary(x, pos):
    half = x.shape[-1] // 2
    inv_freq = jnp.power(ROPE_THETA, -jnp.arange(half, dtype=F32) / half)
    ang = pos.astype(F32)[:, None] * inv_freq[None, :]
    cos = jnp.cos(ang)[None, :, None, :]
    sin = jnp.sin(ang)[None, :, None, :]
    x1, x2 = x[..., :half], x[..., half:]
    return jnp.concatenate([x1 * cos - x2 * sin, x2 * cos + x1 * sin], axis=-1)


def gla_chunked(q, k, v, log_a, s0):
    bt, L, H, _ = q.shape
    dv = v.shape[-1]
    C = min(GLA_CHUNK, L)
    n = -(-L // C)
    pad = n * C - L

    def prep(t):
        t = jnp.pad(t, ((0, 0), (0, pad), (0, 0), (0, 0)))
        return t.reshape(bt, n, C, H, t.shape[-1]).transpose(1, 0, 3, 2, 4)

    qc, kc, vc, gc = prep(q), prep(k), prep(v), prep(log_a)
    bcum = jnp.cumsum(gc, axis=3)
    blast = bcum[:, :, :, -1:, :]
    q_e = qc * jnp.exp(bcum)
    k_e = kc * jnp.exp(-bcum)
    k_end = kc * jnp.exp(blast - bcum)
    causal = jnp.tril(jnp.ones((C, C), dtype=bool))
    att = jnp.where(causal, jnp.einsum('nbhtd,nbhsd->nbhts', q_e, k_e), 0.0)
    o_intra = jnp.einsum('nbhts,nbhsv->nbhtv', att, vc)
    decay = jnp.exp(blast[:, :, :, 0, :])
    kv = jnp.einsum('nbhsd,nbhsv->nbhdv', k_end, vc)

    def step(s, inp):
        dec, kv_i = inp
        return dec[..., None] * s + kv_i, s

    s_fin, starts = s0, []
    for i in range(n):
        s_fin, s_i = step(s_fin, (decay[i], kv[i]))
        starts.append(s_i)
    s_start = jnp.stack(starts)
    o = o_intra + jnp.einsum('nbhtd,nbhdv->nbhtv', q_e, s_start)
    o = o.transpose(1, 0, 3, 2, 4).reshape(bt, n * C, H, dv)[:, :L]
    return o, s_fin


def rwkv7_scan(r, w, k, v, a, b, s0):
    xs = tuple(t.transpose(1, 0, 2, 3) for t in (r, w, k, v, a, b))

    def step(s, inp):
        r_t, w_t, k_t, v_t, a_t, b_t = inp
        sa = jnp.einsum('bhvk,bhk->bhv', s, a_t)
        s = s * w_t[:, :, None, :] + sa[..., None] * b_t[:, :, None, :] + v_t[..., None] * k_t[:, :, None, :]
        return s, jnp.einsum('bhvk,bhk->bhv', s, r_t)

    s, ys = s0, []
    for t in range(r.shape[1]):
        s, y_t = step(s, tuple(x[t] for x in xs))
        ys.append(y_t)
    return jnp.stack(ys, axis=1), s


EVEN_SPLIT = (GLA_QK, GLA_QK, GLA_V, GLA_V, GLA_GATE_RANK, 3 * RW_W + RW_W_LORA + RW_A_LORA + RW_G_LORA)
ODD_SPLIT = (ATT_Q, ATT_KV, ATT_KV, IDX_HEADS * IDX_DH, IDX_DH, IDX_HEADS)


def even_mixer(parts, bt, L, shift_prev, s_gla, s_rw, j, P):
    q_a, k_a, v_a, r_a, g_low, pb_raw = [t.reshape(bt, L, -1) for t in parts]
    ga = lambda t, d: t.reshape(bt, L, GLA_HEADS, d)
    log_alpha = jax.nn.log_sigmoid(g_low @ P['gla_w_gate'][j] + P['gla_b_gate'][j]) / GLA_TAU
    gla = gla_chunked_pallas if L % GLA_CHUNK == 0 else gla_chunked
    o_a, s_gla_new = gla(ga(q_a, GLA_DK) * GLA_DK ** -0.5, ga(k_a, GLA_DK), ga(v_a, GLA_DV),
                         ga(log_alpha, GLA_DK), s_gla)
    prev = jnp.concatenate([shift_prev[:, None, :], pb_raw[:, :-1]], axis=1)
    pb = pb_raw + (prev - pb_raw) * P['rw_mu'][j]
    o1 = 3 * RW_W + RW_W_LORA
    r_b, k_b, v_b, w_low, a_low, gt_low = jnp.split(pb, [RW_W, 2 * RW_W, 3 * RW_W, o1, o1 + RW_A_LORA], axis=-1)
    w_log = -jax.nn.softplus(-(P['rw_w0'][j] + jnp.tanh(w_low) @ P['rw_w_decay'][j])) - 0.5
    log_decay = -jnp.exp(w_log)
    iclr = jax.nn.sigmoid(P['rw_a0'][j] + a_low @ P['rw_w_iclr'][j])
    gate = jax.nn.sigmoid(gt_low) @ P['rw_w_gate'][j]
    hb = lambda t: t.reshape(bt, L, RW_HEADS, RW_DH)
    kk = hb(k_b * P['rw_k_k'][j])
    kk = kk / jnp.maximum(jnp.sqrt(jnp.sum(kk * kk, axis=-1, keepdims=True)), 1e-12)
    k_b = k_b * (1 + (iclr - 1) * P['rw_k_a'][j])
    r_h, k_h, v_h, a_h = hb(r_b), hb(k_b), hb(v_b), hb(iclr)
    if L % RW_CHUNK == 0:
        y_b, s_rw_new = rwkv7_chunked(r_h, hb(log_decay), k_h, v_h, -kk, kk * a_h, s_rw)
    else:
        y_b, s_rw_new = rwkv7_scan(r_h, hb(jnp.exp(log_decay)), k_h, v_h, -kk, kk * a_h, s_rw)
    flat = lambda t: t.reshape(bt * L, -1)
    return [flat(t) for t in (o_a, r_a, y_b, r_b, k_b, v_b, gate)], pb_raw[:, -1], s_gla_new, s_rw_new


def odd_qkv(parts, bt, L, pos, j, P):
    q, k, v, qi, ki, wi = [t.reshape(bt, L, -1) for t in parts]
    q = rotary(rms_norm(q.reshape(bt, L, ATT_HEADS, ATT_DH)) * P['q_norm_g'][j], pos)
    k = rotary(rms_norm(k.reshape(bt, L, ATT_KV_HEADS, ATT_DH)) * P['k_norm_g'][j], pos)
    v = v.reshape(bt, L, ATT_KV_HEADS, ATT_DH)
    qi = rotary(qi.reshape(bt, L, IDX_HEADS, IDX_DH), pos)
    ki = rotary(layer_norm(ki, P['kidx_ln_g'][j], P['kidx_ln_b'][j], EPS)[:, :, None, :], pos)[:, :, 0, :]
    wi = wi * (IDX_HEADS * IDX_DH) ** -0.5
    return q, k, v, qi, ki, wi


def trunk(x, mods, pos, shift0, gla0, rw0, attend, P):
    bt, L, D = x.shape
    x2 = x.reshape(bt * L, D)
    glas, rws, shifts, ks, vs, kis = [], [], [], [], [], []
    for layer, mod in enumerate(mods):
        sh1, sc1, g1, sh2, sc2, g2 = jnp.split(mod, 6, axis=-1)
        j = layer // 2
        if layer % 2 == 0:
            parts = mod_mm(x2, sh1, sc1, P['w_in_even'][j], EVEN_SPLIT)
            mix_ops, shf, sg, sr = even_mixer(parts, bt, L, shift0[j], gla0[j], rw0[j], j, P)
            glas.append(sg)
            rws.append(sr)
            shifts.append(shf)
            x2 = even_out(*mix_ops, P['gla_norm_g'][j], P['rw_ln_g'][j], P['rw_ln_b'][j], P['rw_r_k'][j],
                          P['w_out_even'][j], x2, g1)
            x2 = ffn_block(x2, sh2, sc2, g2, None, P['ffn_w_gate'][j], P['ffn_w_up'][j], P['ffn_w_down'][j])
        else:
            parts = mod_mm(x2, sh1, sc1, P['w_in_odd'][j], ODD_SPLIT)
            q, k, v, qi, ki, wi = odd_qkv(parts, bt, L, pos, j, P)
            ks.append(k)
            vs.append(v)
            kis.append(ki)
            x2 = mm_res(attend(j, q, k, v, qi, ki, wi).reshape(bt * L, -1), P['w_out_odd'][j], x2, g1)
            x2 = ffn_block(x2, sh2, sc2, g2, P['moe_router'][j], P['moe_w_gate'][j], P['moe_w_up'][j],
                           P['moe_w_down'][j])
    return (x2.reshape(bt, L, D), jnp.stack(glas), jnp.stack(rws), jnp.stack(shifts), jnp.stack(ks), jnp.stack(vs),
            jnp.stack(kis))


def kernel(x_prompt, x_sample, state_gla, state_rwkv, state_shift, cache_k, cache_v, cache_kidx, page_table, c_prompt, c_sample, w_ada, b_ada, w_in_even, gla_w_gate, gla_b_gate, gla_norm_g, rw_mu, rw_w0, rw_w_decay, rw_a0, rw_w_iclr, rw_w_gate, rw_k_k, rw_k_a, rw_r_k, rw_ln_g, rw_ln_b, w_out_even, w_in_odd, q_norm_g, k_norm_g, kidx_ln_g, kidx_ln_b, w_out_odd, ffn_w_gate, ffn_w_up, ffn_w_down, moe_router, moe_w_gate, moe_w_up, moe_w_down):
    P = dict(w_ada=w_ada, b_ada=b_ada, w_in_even=w_in_even, gla_w_gate=gla_w_gate, gla_b_gate=gla_b_gate,
             gla_norm_g=gla_norm_g, rw_mu=rw_mu, rw_w0=rw_w0, rw_w_decay=rw_w_decay, rw_a0=rw_a0,
             rw_w_iclr=rw_w_iclr, rw_w_gate=rw_w_gate, rw_k_k=rw_k_k, rw_k_a=rw_k_a, rw_r_k=rw_r_k,
             rw_ln_g=rw_ln_g, rw_ln_b=rw_ln_b, w_out_even=w_out_even, w_in_odd=w_in_odd,
             q_norm_g=q_norm_g, k_norm_g=k_norm_g, kidx_ln_g=kidx_ln_g, kidx_ln_b=kidx_ln_b,
             w_out_odd=w_out_odd, ffn_w_gate=ffn_w_gate, ffn_w_up=ffn_w_up, ffn_w_down=ffn_w_down,
             moe_router=moe_router, moe_w_gate=moe_w_gate, moe_w_up=moe_w_up, moe_w_down=moe_w_down)
    n_even = state_gla.shape[0]
    b_p, seq = x_prompt.shape[0], x_prompt.shape[1]
    zero_gla = jnp.zeros((n_even, b_p) + state_gla.shape[2:], state_gla.dtype)
    zero_rw = jnp.zeros((n_even, b_p) + state_rwkv.shape[2:], state_rwkv.dtype)
    zero_shift = jnp.zeros((n_even, b_p) + state_shift.shape[2:], state_shift.dtype)
    pos_p = jnp.arange(seq, dtype=jnp.int32)
    past = page_table.shape[1] * PAGE_SIZE
    pos_s = past + jnp.arange(x_sample.shape[1], dtype=jnp.int32)

    def prompt_attend(j, q, k, v, qi, ki, wi):
        return dsa_prompt_attention(q, k, v, qi, ki, wi)

    def sample_attend(j, q, k, v, qi, ki, wi):
        return dsa_sample_attention(q, k, v, qi, ki, wi, cache_k, cache_v, cache_kidx, page_table, j)

    silu_c = jax.nn.silu(jnp.concatenate([c_prompt, c_sample], axis=0))
    mods = [mm(silu_c, w_ada[layer]) + b_ada[layer] for layer in range(w_ada.shape[0])]
    mods_p = [m[:b_p] for m in mods]
    mods_s = [m[b_p:] for m in mods]

    y_prompt, p_gla, p_rw, p_shift, p_k, p_v, p_kidx = trunk(
        x_prompt, mods_p, pos_p, zero_shift, zero_gla, zero_rw, prompt_attend, P)
    y_sample, s_gla, s_rw, s_shift, s_k, s_v, s_kidx = trunk(
        x_sample, mods_s, pos_s, state_shift, state_gla, state_rwkv, sample_attend, P)
    return (y_prompt, y_sample, p_gla, p_rw, p_shift, p_k, p_v, p_kidx, s_gla, s_rw, s_shift, s_k, s_v, s_kidx)
```

```python
import functools
import math

import jax
import jax.numpy as jnp
import numpy as np
from jax import lax
from jax.experimental import pallas as pl
from jax.experimental.pallas import tpu as pltpu

F32 = jnp.float32
BF16 = jnp.bfloat16

PAGE_SIZE = 128
GLA_HEADS, GLA_DK, GLA_DV, GLA_GATE_RANK, GLA_TAU, GLA_CHUNK = 4, 64, 128, 16, 16.0, 64
RW_HEADS, RW_DH, RW_W_LORA, RW_A_LORA, RW_G_LORA, RW_LN_EPS = 8, 64, 32, 32, 96, 64e-5
ATT_HEADS, ATT_KV_HEADS, ATT_DH = 16, 4, 64
IDX_HEADS, IDX_DH = 8, 64
TOPK_MAX, Q_BLOCK, ROPE_THETA = 256, 128, 10000.0
N_EXPERTS, TOP_K_EXPERTS = 8, 2
EPS = 1e-6
GLA_QK = GLA_HEADS * GLA_DK
GLA_V = GLA_HEADS * GLA_DV
RW_W = RW_HEADS * RW_DH
ATT_Q = ATT_HEADS * ATT_DH
ATT_KV = ATT_KV_HEADS * ATT_DH

VMEM_LIMIT_BYTES = 56 * 1024 * 1024
LANES = 128
ROW_TILE = 512


def _mm_kernel(x_ref, w_ref, o_ref):
    o_ref[...] = jnp.dot(x_ref[...].astype(BF16), w_ref[...], preferred_element_type=F32)


def mm(x, w):
    m, k = x.shape
    n = w.shape[1]
    tm = min(ROW_TILE, m)
    assert m % tm == 0
    return pl.pallas_call(
        _mm_kernel,
        grid=(m // tm,),
        in_specs=[pl.BlockSpec((tm, k), lambda i: (i, 0)), pl.BlockSpec((k, n), lambda i: (0, 0))],
        out_specs=pl.BlockSpec((tm, n), lambda i: (i, 0)),
        out_shape=jax.ShapeDtypeStruct((m, n), F32),
        compiler_params=pltpu.CompilerParams(dimension_semantics=("arbitrary",),
                                             vmem_limit_bytes=VMEM_LIMIT_BYTES),
    )(x, w.astype(BF16))


def _mod_operand(mod, n_rows, tm):
    b, d = mod.shape
    per_b = n_rows // b
    if per_b % tm == 0:
        tiles_per_b = per_b // tm
        return mod[:, None, :], pl.BlockSpec((1, 1, d), lambda i, *_: (i // tiles_per_b, 0, 0))
    rows = jnp.repeat(mod, per_b, axis=0).reshape(n_rows // tm, tm, d)
    return rows, pl.BlockSpec((1, tm, d), lambda i, *_: (i, 0, 0))


def _modulated(x, shift, scale):
    xn = x * lax.rsqrt(jnp.mean(x * x, axis=-1, keepdims=True) + EPS)
    return xn * (1.0 + scale) + shift


def _aligned_offsets(widths):
    offs, off = [], 0
    for w in widths:
        offs.append(off)
        off += -(-w // LANES) * LANES
    return offs, off


def _mod_mm_kernel(widths, x_ref, sh_ref, sc_ref, w_ref, *o_refs):
    h = _modulated(x_ref[...], sh_ref[0], sc_ref[0])
    y = jnp.dot(h.astype(BF16), w_ref[...], preferred_element_type=F32)
    for o_ref, off, width in zip(o_refs, _aligned_offsets(widths)[0], widths):
        o_ref[...] = y[:, off:off + width]


def mod_mm(x, shift, scale, w, widths):
    t, d = x.shape
    assert sum(widths) == w.shape[1]
    tm = min(ROW_TILE, t)
    sh, mod_spec = _mod_operand(shift, t, tm)
    sc, _ = _mod_operand(scale, t, tm)
    offs, n_pad = _aligned_offsets(widths)
    starts = np.cumsum([0] + list(widths))
    cols = [jnp.pad(w[:, starts[i]:starts[i + 1]], ((0, 0), (0, -widths[i] % LANES))) for i in range(len(widths))]
    w_pad = jnp.concatenate(cols, axis=1).astype(BF16)
    return pl.pallas_call(
        functools.partial(_mod_mm_kernel, tuple(widths)),
        grid=(t // tm,),
        in_specs=[pl.BlockSpec((tm, d), lambda i: (i, 0)), mod_spec, mod_spec, pl.BlockSpec((d, n_pad), lambda i: (0, 0))],
        out_specs=[pl.BlockSpec((tm, wd), lambda i: (i, 0)) for wd in widths],
        out_shape=[jax.ShapeDtypeStruct((t, wd), F32) for wd in widths],
        compiler_params=pltpu.CompilerParams(dimension_semantics=("arbitrary",),
                                             vmem_limit_bytes=VMEM_LIMIT_BYTES),
        name="modulated_in_proj",
    )(x, sh, sc, w_pad)


def _even_out_kernel(oa_ref, ra_ref, yb_ref, rb_ref, kb_ref, vb_ref, gate_ref, gn_ref, lng_ref, lnb_ref, rk_ref,
                     w_ref, x_ref, g_ref, o_ref):
    pieces = []
    for h in range(GLA_HEADS):
        sl = slice(h * GLA_DV, (h + 1) * GLA_DV)
        o = oa_ref[:, sl]
        r = ra_ref[:, sl]
        o = o * lax.rsqrt(jnp.mean(o * o, axis=-1, keepdims=True) + EPS) * gn_ref[:, sl]
        pieces.append(o * (r * (1.0 / (1.0 + jnp.exp(-r)))))
    for h in range(RW_HEADS):
        sl = slice(h * RW_DH, (h + 1) * RW_DH)
        y = yb_ref[:, sl]
        mu = jnp.mean(y, axis=-1, keepdims=True)
        yc = y - mu
        var = jnp.mean(yc * yc, axis=-1, keepdims=True)
        y = yc * lax.rsqrt(var + RW_LN_EPS) * lng_ref[:, sl] + lnb_ref[:, sl]
        bonus = jnp.sum(rb_ref[:, sl] * kb_ref[:, sl] * rk_ref[:, sl], axis=-1, keepdims=True)
        pieces.append((y + bonus * vb_ref[:, sl]) * gate_ref[:, sl])
    mix = jnp.concatenate(pieces, axis=1).astype(BF16)
    o_ref[...] = x_ref[...] + g_ref[0] * jnp.dot(mix, w_ref[...], preferred_element_type=F32)


def even_out(o_a, r_a, y_b, r_b, k_b, v_b, gate_b, gla_norm_g, ln_g, ln_b, r_k, w, x, gate):
    t, d = x.shape
    tm = min(ROW_TILE, t)
    g, mod_spec = _mod_operand(gate, t, tm)
    seq = pl.BlockSpec((tm, GLA_V), lambda i: (i, 0))
    par = pl.BlockSpec((1, GLA_V), lambda i: (0, 0))
    assert GLA_V == RW_W
    return pl.pallas_call(
        _even_out_kernel,
        grid=(t // tm,),
        in_specs=[seq] * 7 + [par] * 4 + [pl.BlockSpec((GLA_V + RW_W, d), lambda i: (0, 0)),
                                         pl.BlockSpec((tm, d), lambda i: (i, 0)), mod_spec],
        out_specs=pl.BlockSpec((tm, d), lambda i: (i, 0)),
        out_shape=jax.ShapeDtypeStruct((t, d), F32),
        compiler_params=pltpu.CompilerParams(dimension_semantics=("arbitrary",),
                                             vmem_limit_bytes=VMEM_LIMIT_BYTES),
        name="even_out_proj_residual",
    )(o_a, r_a, y_b, r_b, k_b, v_b, gate_b, jnp.tile(gla_norm_g, GLA_HEADS)[None], ln_g[None], ln_b[None],
      r_k.reshape(1, RW_W), w.astype(BF16), x, g)


def _mm_res_kernel(a_ref, w_ref, x_ref, g_ref, o_ref):
    y = jnp.dot(a_ref[...].astype(BF16), w_ref[...], preferred_element_type=F32)
    o_ref[...] = x_ref[...] + g_ref[0] * y


def mm_res(a, w, x, gate):
    t, k = a.shape
    d = w.shape[1]
    tm = min(ROW_TILE, t)
    g, mod_spec = _mod_operand(gate, t, tm)
    return pl.pallas_call(
        _mm_res_kernel,
        grid=(t // tm,),
        in_specs=[pl.BlockSpec((tm, k), lambda i: (i, 0)), pl.BlockSpec((k, d), lambda i: (0, 0)),
                  pl.BlockSpec((tm, d), lambda i: (i, 0)), mod_spec],
        out_specs=pl.BlockSpec((tm, d), lambda i: (i, 0)),
        out_shape=jax.ShapeDtypeStruct((t, d), F32),
        compiler_params=pltpu.CompilerParams(dimension_semantics=("arbitrary",),
                                             vmem_limit_bytes=VMEM_LIMIT_BYTES),
        name="out_proj_residual",
    )(a, w.astype(BF16), x, g)


def _ffn_kernel(routed, x_ref, sh_ref, sc_ref, gt_ref, wr_ref, wg_ref, wu_ref, wd_ref, o_ref, h_scr, cmb_scr, acc_scr):
    e = pl.program_id(1)
    lane = lax.broadcasted_iota(jnp.int32, cmb_scr.shape, 1)

    @pl.when(e == 0)
    def _():
        hb = _modulated(x_ref[...], sh_ref[0], sc_ref[0]).astype(BF16)
        h_scr[...] = hb
        acc_scr[...] = jnp.zeros(acc_scr.shape, F32)
        if routed:
            logits = jnp.dot(hb, wr_ref[...], preferred_element_type=F32)
            lg = jnp.where(lane < N_EXPERTS, logits, -jnp.inf)
            m1 = jnp.max(lg, axis=1, keepdims=True)
            i1 = jnp.min(jnp.where(lg == m1, lane, LANES), axis=1, keepdims=True)
            lg2 = jnp.where(lane == i1, -jnp.inf, lg)
            m2 = jnp.max(lg2, axis=1, keepdims=True)
            i2 = jnp.min(jnp.where(lg2 == m2, lane, LANES), axis=1, keepdims=True)
            t = jnp.exp(m2 - m1)
            p1 = 1.0 / (1.0 + t)
            cmb_scr[...] = jnp.where(lane == i1, p1, 0.0) + jnp.where(lane == i2, t * p1, 0.0)

    hb = h_scr[...]
    g = jnp.dot(hb, wg_ref[0], preferred_element_type=F32)
    u = jnp.dot(hb, wu_ref[0], preferred_element_type=F32)
    act = g * (1.0 / (1.0 + jnp.exp(-g))) * u
    y = jnp.dot(act.astype(BF16), wd_ref[0], preferred_element_type=F32)
    if routed:
        y = y * jnp.sum(jnp.where(lane == e, cmb_scr[...], 0.0), axis=1, keepdims=True)
    acc_scr[...] += y

    @pl.when(e == pl.num_programs(1) - 1)
    def _():
        o_ref[...] = x_ref[...] + gt_ref[0] * acc_scr[...]


FFN_BLOCK = 1408


def ffn_block(x, shift, scale, gate, w_router, w_gate, w_up, w_down):
    t, d = x.shape
    routed = w_router is not None
    tm = min(ROW_TILE, t)
    sh, mod_spec = _mod_operand(shift, t, tm)
    sc, _ = _mod_operand(scale, t, tm)
    gt, _ = _mod_operand(gate, t, tm)
    if routed:
        n_e, _, f = w_gate.shape
        assert f == FFN_BLOCK and n_e == N_EXPERTS
        wr = jnp.pad(w_router, ((0, 0), (0, LANES - n_e))).astype(BF16)
        up_spec = pl.BlockSpec((1, d, f), lambda i, e: (e, 0, 0))
        down_spec = pl.BlockSpec((1, f, d), lambda i, e: (e, 0, 0))
    else:
        f_all = w_gate.shape[1]
        assert f_all % FFN_BLOCK == 0
        n_e, f = f_all // FFN_BLOCK, FFN_BLOCK
        wr = jnp.zeros((d, LANES), BF16)
        w_gate, w_up, w_down = w_gate[None], w_up[None], w_down[None]
        up_spec = pl.BlockSpec((1, d, f), lambda i, e: (0, 0, e))
        down_spec = pl.BlockSpec((1, f, d), lambda i, e: (0, e, 0))
    return pl.pallas_call(
        functools.partial(_ffn_kernel, routed),
        grid=(t // tm, n_e),
        in_specs=[pl.BlockSpec((tm, d), lambda i, e: (i, 0)), mod_spec, mod_spec, mod_spec,
                  pl.BlockSpec((d, LANES), lambda i, e: (0, 0)), up_spec, up_spec, down_spec],
        out_specs=pl.BlockSpec((tm, d), lambda i, e: (i, 0)),
        out_shape=jax.ShapeDtypeStruct((t, d), F32),
        scratch_shapes=[pltpu.VMEM((tm, d), BF16), pltpu.VMEM((tm, LANES), F32), pltpu.VMEM((tm, d), F32)],
        compiler_params=pltpu.CompilerParams(dimension_semantics=("arbitrary", "arbitrary"),
                                             vmem_limit_bytes=VMEM_LIMIT_BYTES),
        name="routed_experts" if routed else "dense_swiglu",
    )(x, sh, sc, gt, wr, w_gate.astype(BF16), w_up.astype(BF16), w_down.astype(BF16))


RW_CHUNK = 64


def _bdot(a, b):
    return jnp.dot(a.astype(BF16), b.astype(BF16), preferred_element_type=F32)


def _bdot_nt(a, b):
    return lax.dot_general(a.astype(BF16), b.astype(BF16), (((1,), (1,)), ((), ())), preferred_element_type=F32)


def _bdot_tn(a, b):
    return lax.dot_general(a.astype(BF16), b.astype(BF16), (((0,), (0,)), ((), ())), preferred_element_type=F32)


def _split3(x):
    hi = x.astype(BF16)
    r1 = x - hi.astype(F32)
    mid = r1.astype(BF16)
    lo = (r1 - mid.astype(F32)).astype(BF16)
    return hi, mid, lo


def _rwkv_chunk_heads(r, lw, k, v, a, b, h0, tri_bf, strict, incl, eye):
    nh = len(r)
    hs = range(nh)
    cum = []
    for i in hs:
        hi, mid, lo = _split3(lw[i])
        cum.append(jnp.dot(tri_bf, hi, preferred_element_type=F32) + jnp.dot(tri_bf, mid, preferred_element_type=F32)
                   + jnp.dot(tri_bf, lo, preferred_element_type=F32))
    cum_last = [cum[i][RW_CHUNK - 1:RW_CHUNK, :] for i in hs]
    e_neg = [jnp.exp(-cum[i]) for i in hs]
    e_end = [jnp.exp(cum_last[i] - cum[i]) for i in hs]
    a_t = [a[i] * jnp.exp(cum[i] - lw[i]) for i in hs]
    r_t = [r[i] * jnp.exp(cum[i]) for i in hs]
    k_t = [k[i] * e_neg[i] for i in hs]
    b_t = [b[i] * e_neg[i] for i in hs]
    k_e = [k[i] * e_end[i] for i in hs]
    b_e = [b[i] * e_end[i] for i in hs]
    a_ab = [jnp.where(strict, _bdot_nt(a_t[i], b_t[i]), 0.0) for i in hs]
    a_ak = [jnp.where(strict, _bdot_nt(a_t[i], k_t[i]), 0.0) for i in hs]
    a_rb = [jnp.where(incl, _bdot_nt(r_t[i], b_t[i]), 0.0) for i in hs]
    a_rk = [jnp.where(incl, _bdot_nt(r_t[i], k_t[i]), 0.0) for i in hs]
    x = [eye + a_ab[i] for i in hs]
    y = list(a_ab)
    for _ in range(int(math.log2(RW_CHUNK)) - 1):
        y = [_bdot(y[i], y[i]) for i in hs]
        x = [x[i] + _bdot(x[i], y[i]) for i in hs]
    a_p = [_bdot(x[i], a_t[i]) for i in hs]
    akv = [_bdot(a_ak[i], v[i]) for i in hs]
    v_p = [_bdot(x[i], akv[i]) for i in hs]
    r_p = [r_t[i] + _bdot(a_rb[i], a_p[i]) for i in hs]
    y_p = [_bdot(a_rk[i], v[i]) + _bdot(a_rb[i], v_p[i]) for i in hs]
    m_lr = [_bdot_tn(b_e[i], a_p[i]) for i in hs]
    g = [_bdot_tn(k_e[i], v[i]) + _bdot_tn(b_e[i], v_p[i]) for i in hs]
    w_col = [jnp.sum(jnp.where(eye, jnp.exp(cum_last[i]), 0.0), axis=1, keepdims=True) for i in hs]
    y_out = [_bdot(r_p[i], h0[i]) + y_p[i] for i in hs]
    h_new = [w_col[i] * h0[i] + _bdot(m_lr[i], h0[i]) + g[i] for i in hs]
    return y_out, h_new


RW_HEAD_GROUP = 16


def _rwkv_kernel(r_ref, lw_ref, k_ref, v_ref, a_ref, b_ref, h0_ref, y_ref, hout_ref, h_scr):
    c = pl.program_id(0)

    @pl.when(c == 0)
    def _():
        h_scr[...] = h0_ref[...]

    rows = lax.broadcasted_iota(jnp.int32, (RW_CHUNK, RW_CHUNK), 0)
    cols = lax.broadcasted_iota(jnp.int32, (RW_CHUNK, RW_CHUNK), 1)
    strict = rows > cols
    incl = rows >= cols
    eye = rows == cols
    tri_bf = jnp.where(incl, 1.0, 0.0).astype(BF16)
    n_b, _, width = r_ref.shape
    n = h_scr.shape[-1]
    heads = [(b, h) for b in range(n_b) for h in range(width // n)]
    for i0 in range(0, len(heads), RW_HEAD_GROUP):
        grp = heads[i0:i0 + RW_HEAD_GROUP]
        seqs = [[ref[b, :, h * n:(h + 1) * n] for b, h in grp] for ref in (r_ref, lw_ref, k_ref, v_ref, a_ref, b_ref)]
        states = [h_scr[b * (width // n) + h] for b, h in grp]
        y_out, h_new = _rwkv_chunk_heads(*seqs, states, tri_bf, strict, incl, eye)
        for j, (b, h) in enumerate(grp):
            y_ref[b, :, h * n:(h + 1) * n] = y_out[j]
            h_scr[b * (width // n) + h] = h_new[j]

    @pl.when(c == pl.num_programs(0) - 1)
    def _():
        hout_ref[...] = h_scr[...]


def rwkv7_chunked(r, lw, k, v, a, b, s0):
    bt, L, H, N = r.shape
    assert L % RW_CHUNK == 0
    flat = lambda t: t.reshape(bt, L, H * N)
    h0 = s0.transpose(0, 1, 3, 2).reshape(bt * H, N, N)
    seq_spec = pl.BlockSpec((bt, RW_CHUNK, H * N), lambda c: (0, c, 0))
    st_spec = pl.BlockSpec((bt * H, N, N), lambda c: (0, 0, 0))
    y, h_fin = pl.pallas_call(
        _rwkv_kernel,
        grid=(L // RW_CHUNK,),
        in_specs=[seq_spec] * 6 + [st_spec],
        out_specs=[seq_spec, st_spec],
        out_shape=[jax.ShapeDtypeStruct((bt, L, H * N), F32), jax.ShapeDtypeStruct((bt * H, N, N), F32)],
        scratch_shapes=[pltpu.VMEM((bt * H, N, N), F32)],
        compiler_params=pltpu.CompilerParams(dimension_semantics=("arbitrary",),
                                             vmem_limit_bytes=VMEM_LIMIT_BYTES),
        name="rwkv7_chunked",
    )(flat(r), flat(lw), flat(k), flat(v), flat(a), flat(b), h0)
    return y.reshape(bt, L, H, N), h_fin.reshape(bt, H, N, N).transpose(0, 1, 3, 2)


def _gla_kernel(q_ref, k_ref, v_ref, g_ref, s0_ref, o_ref, sout_ref, s_scr):
    c = pl.program_id(0)

    @pl.when(c == 0)
    def _():
        s_scr[...] = s0_ref[...]

    rows = lax.broadcasted_iota(jnp.int32, (GLA_CHUNK, GLA_CHUNK), 0)
    cols = lax.broadcasted_iota(jnp.int32, (GLA_CHUNK, GLA_CHUNK), 1)
    incl = rows >= cols
    eye = lax.broadcasted_iota(jnp.int32, (GLA_DK, GLA_DK), 0) == lax.broadcasted_iota(jnp.int32, (GLA_DK, GLA_DK), 1)
    tri_bf = jnp.where(incl, 1.0, 0.0).astype(BF16)
    n_b, _, qk_width = q_ref.shape
    n_h = qk_width // GLA_DK
    heads = [(b, h) for b in range(n_b) for h in range(n_h)]
    hs = range(len(heads))
    qs = [q_ref[b, :, h * GLA_DK:(h + 1) * GLA_DK] for b, h in heads]
    ks = [k_ref[b, :, h * GLA_DK:(h + 1) * GLA_DK] for b, h in heads]
    vs = [v_ref[b, :, h * GLA_DV:(h + 1) * GLA_DV] for b, h in heads]
    bcum = []
    for b, h in heads:
        hi, mid, lo = _split3(g_ref[b, :, h * GLA_DK:(h + 1) * GLA_DK])
        bcum.append(jnp.dot(tri_bf, hi, preferred_element_type=F32) + jnp.dot(tri_bf, mid, preferred_element_type=F32)
                    + jnp.dot(tri_bf, lo, preferred_element_type=F32))
    blast = [bcum[i][GLA_CHUNK - 1:GLA_CHUNK, :] for i in hs]
    q_e = [qs[i] * jnp.exp(bcum[i]) for i in hs]
    k_e = [ks[i] * jnp.exp(-bcum[i]) for i in hs]
    k_end = [ks[i] * jnp.exp(blast[i] - bcum[i]) for i in hs]
    att = [jnp.where(incl, _bdot_nt(q_e[i], k_e[i]), 0.0) for i in hs]
    s_old = [s_scr[i] for i in hs]
    o = [_bdot(att[i], vs[i]) + _bdot(q_e[i], s_old[i]) for i in hs]
    kv = [_bdot_tn(k_end[i], vs[i]) for i in hs]
    dec = [jnp.sum(jnp.where(eye, jnp.exp(blast[i]), 0.0), axis=1, keepdims=True) for i in hs]
    for i, (b, h) in enumerate(heads):
        o_ref[b, :, h * GLA_DV:(h + 1) * GLA_DV] = o[i]
        s_scr[i] = dec[i] * s_old[i] + kv[i]

    @pl.when(c == pl.num_programs(0) - 1)
    def _():
        sout_ref[...] = s_scr[...]


def gla_chunked_pallas(q, k, v, log_a, s0):
    bt, L, H, dk = q.shape
    dv = v.shape[-1]
    assert L % GLA_CHUNK == 0
    flat = lambda t: t.reshape(bt, L, H * t.shape[-1])
    qk_spec = pl.BlockSpec((bt, GLA_CHUNK, H * dk), lambda c: (0, c, 0))
    v_spec = pl.BlockSpec((bt, GLA_CHUNK, H * dv), lambda c: (0, c, 0))
    st_spec = pl.BlockSpec((bt * H, dk, dv), lambda c: (0, 0, 0))
    o, s_fin = pl.pallas_call(
        _gla_kernel,
        grid=(L // GLA_CHUNK,),
        in_specs=[qk_spec, qk_spec, v_spec, qk_spec, st_spec],
        out_specs=[v_spec, st_spec],
        out_shape=[jax.ShapeDtypeStruct((bt, L, H * dv), F32), jax.ShapeDtypeStruct((bt * H, dk, dv), F32)],
        scratch_shapes=[pltpu.VMEM((bt * H, dk, dv), F32)],
        compiler_params=pltpu.CompilerParams(dimension_semantics=("arbitrary",),
                                             vmem_limit_bytes=VMEM_LIMIT_BYTES),
        name="gla_chunked",
    )(flat(q), flat(k), flat(v), flat(log_a), s0.reshape(bt * H, dk, dv))
    return o.reshape(bt, L, H, dv), s_fin.reshape(bt, H, dk, dv)


DSA_QB = 128
DSA_KC = 512
MASK_NEG = -1e30
INT_MIN = -2 ** 31
INT16_MIN = -2 ** 15
ONE16 = np.int16(1)
ZERO16 = np.int16(0)
ONE = np.float32(1.0)
ZERO = np.float32(0.0)
G_PER_KV = ATT_HEADS // ATT_KV_HEADS


def _sortable_key(x):
    bits = lax.bitcast_convert_type(x + 0.0, jnp.int32)
    return bits ^ ((bits >> 31) & 0x7FFFFFFF)


def _dsa_kernel(topk, q_ref, qi_ref, wi_ref, k_ref, v_ref, kit_ref, upper_ref, o_ref,
                wb_scr, key_scr, khi_scr, klo_scr, m_scr, acc_scr):
    i = pl.program_id(1)
    n_chunks = (i * DSA_QB) // DSA_KC + 1
    nt = DSA_KC // LANES
    q_pos = i * DSA_QB + lax.broadcasted_iota(jnp.int32, (DSA_QB, LANES), 0)
    lane = lax.broadcasted_iota(jnp.int32, (DSA_QB, LANES), 1)

    for h in range(IDX_HEADS):
        wb_scr[h] = jnp.broadcast_to(wi_ref[0, :, h:h + 1], (DSA_QB, LANES))

    def score_chunk(c, carry):
        dots = jnp.dot(qi_ref[0, 0], kit_ref[0, c], preferred_element_type=F32)
        for jt in range(nt):
            sc = jnp.zeros((DSA_QB, LANES), F32)
            for h in range(IDX_HEADS):
                d = dots[h * DSA_QB:(h + 1) * DSA_QB, jt * LANES:(jt + 1) * LANES]
                sc = sc + jnp.maximum(d, 0.0) * wb_scr[h]
            k_pos = c * DSA_KC + jt * LANES + lane
            sc = jnp.where(k_pos <= q_pos, sc, -jnp.inf)
            key = _sortable_key(sc)
            key_scr[c, :, jt * LANES:(jt + 1) * LANES] = key
            khi_scr[c, :, jt * LANES:(jt + 1) * LANES] = (key >> 16).astype(jnp.int16)
        return carry

    lax.fori_loop(0, n_chunks, score_chunk, 0)

    def count16(scr, cand, strict):
        cand_b = jnp.broadcast_to(cand.astype(jnp.int16), (DSA_QB, LANES))

        def body(c, acc):
            for jt in range(nt):
                tile = scr[c, :, jt * LANES:(jt + 1) * LANES]
                hit = (tile > cand_b) if strict else (tile >= cand_b)
                acc = acc + jnp.where(hit, ONE16, ZERO16)
            return acc

        acc = lax.fori_loop(0, n_chunks, body, jnp.zeros((DSA_QB, LANES), jnp.int16))
        return jnp.sum(acc.astype(F32), axis=1, keepdims=True)

    def bisect16(scr, k_need):
        def count3(cands):
            cands_b = [jnp.broadcast_to(cd.astype(jnp.int16), (DSA_QB, LANES)) for cd in cands]

            def body(c, accs):
                accs = list(accs)
                for jt in range(nt):
                    tile = scr[c, :, jt * LANES:(jt + 1) * LANES]
                    for j in range(3):
                        accs[j] = accs[j] + jnp.where(tile >= cands_b[j], ONE16, ZERO16)
                return tuple(accs)

            zero = jnp.zeros((DSA_QB, LANES), jnp.int16)
            accs = lax.fori_loop(0, n_chunks, body, (zero, zero, zero))
            return [jnp.sum(acc.astype(F32), axis=1, keepdims=True) for acc in accs]

        def pair_step(p, cur_u):
            shift = 14 - 2 * p
            counts = count3([cur_u + (jnp.int32(j) << shift) + INT16_MIN for j in (1, 2, 3)])
            digit = sum(jnp.where(cnt >= k_need, 1, 0) for cnt in counts)
            return cur_u + (digit.astype(jnp.int32) << shift)

        return lax.fori_loop(0, 8, pair_step, jnp.zeros((DSA_QB, 1), jnp.int32)) + INT16_MIN

    kf = float(topk)
    thr_hi = bisect16(khi_scr, kf)
    above = count16(khi_scr, thr_hi, True)
    thr_hi_b = jnp.broadcast_to(thr_hi, (DSA_QB, LANES))

    def low_halves(c, carry):
        for jt in range(nt):
            key = key_scr[c, :, jt * LANES:(jt + 1) * LANES]
            lo = (key & 0xFFFF) - 32768
            klo_scr[c, :, jt * LANES:(jt + 1) * LANES] = jnp.where((key >> 16) == thr_hi_b, lo, INT16_MIN).astype(jnp.int16)
        return carry

    lax.fori_loop(0, n_chunks, low_halves, 0)
    thr_lo = bisect16(klo_scr, kf - above)
    n_gt = above + count16(klo_scr, thr_lo, True)
    need = kf - n_gt
    thr = (thr_hi << 16) | (thr_lo + 32768)

    m_scr[...] = jnp.full(m_scr.shape, MASK_NEG, F32)
    acc_scr[...] = jnp.zeros(acc_scr.shape, F32)

    def attend_chunk(c, tie_carry):
        k0 = pl.multiple_of(c * DSA_KC, DSA_KC)
        keys = key_scr[c]
        thr_c = jnp.broadcast_to(thr, (DSA_QB, DSA_KC))
        eq = keys == thr_c
        eq_f = jnp.where(eq, ONE, ZERO)
        rank = tie_carry + jnp.dot(eq_f.astype(BF16), upper_ref[...], preferred_element_type=F32)
        k_pos = k0 + lax.broadcasted_iota(jnp.int32, (DSA_QB, DSA_KC), 1)
        qp = i * DSA_QB + lax.broadcasted_iota(jnp.int32, (DSA_QB, DSA_KC), 0)
        sel = jnp.where(keys > thr_c, ONE, jnp.where(rank < need, eq_f, ZERO))
        bias = jnp.where((sel > 0.5) & (k_pos <= qp), ZERO, np.float32(MASK_NEG))
        v_c = v_ref[0, pl.ds(k0, DSA_KC), :]
        rows = G_PER_KV * DSA_QB
        for n in range(ATT_KV_HEADS):
            s = lax.dot_general(q_ref[0, 0, n], k_ref[0, n, pl.ds(k0, DSA_KC), :], (((1,), (1,)), ((), ())),
                                preferred_element_type=F32)
            s = (s.reshape(G_PER_KV, DSA_QB, DSA_KC) + bias[None]).reshape(rows, DSA_KC)
            tiles = [s[:, jt * LANES:(jt + 1) * LANES] for jt in range(nt)]
            tile_max = functools.reduce(jnp.maximum, tiles)
            m_old = m_scr[n]
            m_new = jnp.maximum(m_old, jnp.max(tile_max, axis=1, keepdims=True))
            alpha = jnp.exp(m_old - m_new)
            p = jnp.concatenate([jnp.exp((t - m_new).astype(BF16)) for t in tiles], axis=1)
            pv = jnp.dot(p, v_c[:, n * LANES:(n + 1) * LANES], preferred_element_type=F32)
            acc_scr[n] = alpha * acc_scr[n] + pv
            m_scr[n] = m_new
        return tie_carry + jnp.sum(eq_f, axis=1, keepdims=True)

    lax.fori_loop(0, n_chunks, attend_chunk, jnp.zeros((DSA_QB, 1), F32))

    for n in range(ATT_KV_HEADS):
        acc = acc_scr[n]
        o_n = acc[:, :ATT_DH] / acc[:, ATT_DH:]
        for g in range(G_PER_KV):
            hd = n * G_PER_KV + g
            o_ref[0, :, hd * ATT_DH:(hd + 1) * ATT_DH] = o_n[g * DSA_QB:(g + 1) * DSA_QB, :]


def dsa_prompt_attention(q, k, v, qi, ki, wi):
    bt, L = q.shape[:2]
    assert L % DSA_KC == 0 and L // 4 >= 1
    topk = min(TOPK_MAX, L // 4)
    nb = L // DSA_QB
    nc = L // DSA_KC
    kit = ki.astype(BF16).reshape(bt, nc, DSA_KC, IDX_DH).transpose(0, 1, 3, 2)
    upper = jnp.triu(jnp.ones((DSA_KC, DSA_KC), BF16), 1)
    rows = G_PER_KV * DSA_QB
    q_blocks = (q * ATT_DH ** -0.5).astype(BF16).reshape(bt, nb, DSA_QB, ATT_KV_HEADS, G_PER_KV, ATT_DH)
    q_blocks = q_blocks.transpose(0, 1, 3, 4, 2, 5).reshape(bt, nb, ATT_KV_HEADS, rows, ATT_DH)
    qi_blocks = qi.astype(BF16).reshape(bt, nb, DSA_QB, IDX_HEADS, IDX_DH).transpose(0, 1, 3, 2, 4)
    qi_blocks = qi_blocks.reshape(bt, nb, IDX_HEADS * DSA_QB, IDX_DH)
    v_aug = jnp.concatenate([v.astype(BF16), jnp.ones(v.shape, BF16)], axis=-1).reshape(bt, L, ATT_KV_HEADS * LANES)
    return pl.pallas_call(
        functools.partial(_dsa_kernel, topk),
        grid=(bt, nb),
        in_specs=[
            pl.BlockSpec((1, 1, ATT_KV_HEADS, rows, ATT_DH), lambda b, i: (b, i, 0, 0, 0)),
            pl.BlockSpec((1, 1, IDX_HEADS * DSA_QB, IDX_DH), lambda b, i: (b, i, 0, 0)),
            pl.BlockSpec((1, DSA_QB, IDX_HEADS), lambda b, i: (b, i, 0)),
            pl.BlockSpec((1, ATT_KV_HEADS, L, ATT_DH), lambda b, i: (b, 0, 0, 0)),
            pl.BlockSpec((1, L, ATT_KV_HEADS * LANES), lambda b, i: (b, 0, 0)),
            pl.BlockSpec((1, nc, IDX_DH, DSA_KC), lambda b, i: (b, 0, 0, 0)),
            pl.BlockSpec((DSA_KC, DSA_KC), lambda b, i: (0, 0)),
        ],
        out_specs=pl.BlockSpec((1, DSA_QB, ATT_Q), lambda b, i: (b, i, 0)),
        out_shape=jax.ShapeDtypeStruct((bt, L, ATT_Q), F32),
        scratch_shapes=[
            pltpu.VMEM((IDX_HEADS, DSA_QB, LANES), F32),
            pltpu.VMEM((nc, DSA_QB, DSA_KC), jnp.int32),
            pltpu.VMEM((nc, DSA_QB, DSA_KC), jnp.int16),
            pltpu.VMEM((nc, DSA_QB, DSA_KC), jnp.int16),
            pltpu.VMEM((ATT_KV_HEADS, rows, LANES), F32),
            pltpu.VMEM((ATT_KV_HEADS, rows, LANES), F32),
        ],
        compiler_params=pltpu.CompilerParams(dimension_semantics=("arbitrary", "arbitrary"),
                                             vmem_limit_bytes=VMEM_LIMIT_BYTES),
        name="dsa_prompt_attention",
    )(q_blocks, qi_blocks, wi,
      k.astype(BF16).transpose(0, 2, 1, 3), v_aug, kit, upper)


NEW_PAD = 16
PAGES_PER_STEP = 16


def _dsa_sample_kernel(topk, n_new, pt_ref, qbd_ref, qi_ref, wrep_ref, knew_ref, vnew_ref, kinew_ref, *rest):
    pages = rest[:3 * PAGES_PER_STEP]
    upper_ref, o_ref, kbuf, vbuf, kibuf = rest[3 * PAGES_PER_STEP:]
    p = pl.program_id(1)
    n_steps = pl.num_programs(1)
    past = n_steps * PAGES_PER_STEP * PAGE_SIZE
    n_keys = kbuf.shape[0]
    nq = qbd_ref.shape[1] // ATT_HEADS
    for buf, refs in zip((kbuf, vbuf, kibuf), (pages[:PAGES_PER_STEP], pages[PAGES_PER_STEP:2 * PAGES_PER_STEP],
                                               pages[2 * PAGES_PER_STEP:])):
        for s, ref in enumerate(refs):
            row0 = pl.multiple_of((p * PAGES_PER_STEP + s) * PAGE_SIZE, PAGE_SIZE)
            buf[pl.ds(row0, PAGE_SIZE), :] = ref[0, 0].astype(BF16)

    @pl.when(p == n_steps - 1)
    def _():
        tail = n_keys - past
        kbuf[past:, :] = jnp.zeros((tail, ATT_KV), BF16)
        vbuf[past:, :] = jnp.zeros((tail, ATT_KV), BF16)
        kibuf[past:, :] = jnp.zeros((tail, IDX_DH), BF16)
        kbuf[past:past + NEW_PAD, :] = knew_ref[0].astype(BF16)
        vbuf[past:past + NEW_PAD, :] = vnew_ref[0].astype(BF16)
        kibuf[past:past + NEW_PAD, :] = kinew_ref[0].astype(BF16)

        dots = lax.dot_general(qi_ref[0], kibuf[...], (((1,), (1,)), ((), ())), preferred_element_type=F32)
        k_idx = lax.broadcasted_iota(jnp.int32, (8, n_keys), 1)
        q_row = lax.broadcasted_iota(jnp.int32, (8, n_keys), 0)
        valid = (k_idx - past <= q_row) & (q_row < nq)
        rows = []
        for qn in range(nq):
            d = jnp.maximum(dots[qn * IDX_HEADS:(qn + 1) * IDX_HEADS], 0.0)
            w = wrep_ref[0, qn * IDX_HEADS:(qn + 1) * IDX_HEADS, :]
            parts = []
            for jt in range(n_keys // LANES):
                parts.append(jnp.sum(d[:, jt * LANES:(jt + 1) * LANES] * w, axis=0, keepdims=True))
            rows.append(jnp.concatenate(parts, axis=1))
        rows.append(jnp.zeros((8 - nq, n_keys), F32))
        sc = jnp.where(valid, jnp.concatenate(rows, axis=0), -jnp.inf)
        keys = _sortable_key(sc)

        kf = float(topk)

        def count_ge(cand):
            return jnp.sum(jnp.where(keys >= cand, ONE, ZERO), axis=1, keepdims=True)

        def digit_step(i, cur_u):
            shift = 28 - 4 * i
            digit = jnp.zeros((8, 1), jnp.int32)
            for jd in range(1, 16):
                cand = (cur_u + (jnp.int32(jd) << shift)) ^ INT_MIN
                digit = digit + jnp.where(count_ge(cand) >= kf, 1, 0)
            return cur_u + (digit << shift)

        thr = lax.fori_loop(0, 8, digit_step, jnp.zeros((8, 1), jnp.int32)) ^ INT_MIN
        n_gt = jnp.sum(jnp.where(keys > thr, ONE, ZERO), axis=1, keepdims=True)
        need = kf - n_gt
        eq_f = jnp.where(keys == thr, ONE, ZERO)
        carry = jnp.zeros((8, 1), F32)
        ranks = []
        for jt in range(n_keys // LANES):
            e = eq_f[:, jt * LANES:(jt + 1) * LANES]
            ranks.append(carry + jnp.dot(e.astype(BF16), upper_ref[...], preferred_element_type=F32))
            carry = carry + jnp.sum(e, axis=1, keepdims=True)
        rank = jnp.concatenate(ranks, axis=1)
        sel = jnp.where(keys > thr, ONE, jnp.where(rank < need, eq_f, ZERO))
        bias = jnp.where((sel > 0.5) & valid, ZERO, np.float32(MASK_NEG))

        s = lax.dot_general(qbd_ref[0], kbuf[...], (((1,), (1,)), ((), ())), preferred_element_type=F32)
        s = jnp.concatenate([s[qn * ATT_HEADS:(qn + 1) * ATT_HEADS] + bias[qn:qn + 1] for qn in range(nq)], axis=0)
        m = jnp.max(s, axis=1, keepdims=True)
        pr = jnp.exp(s - m)
        l = jnp.sum(pr, axis=1, keepdims=True)
        o_ref[0] = jnp.dot(pr.astype(BF16), vbuf[...], preferred_element_type=F32) / l


def dsa_sample_attention(q, k_new, v_new, qi, ki_new, wi, cache_k, cache_v, cache_kidx, page_table, j):
    bt, nq = q.shape[:2]
    n_pages = page_table.shape[1]
    past = n_pages * PAGE_SIZE
    assert nq <= 8 and nq <= NEW_PAD
    topk = min(TOPK_MAX, (past + nq) // 4)
    n_keys = past + LANES
    n_pool = cache_k.shape[1]
    qs = (q * ATT_DH ** -0.5).astype(BF16).reshape(bt, nq, ATT_KV_HEADS, G_PER_KV, 1, ATT_DH)
    eye = jnp.eye(ATT_KV_HEADS, dtype=BF16)[None, None, :, None, :, None]
    qbd = (qs * eye).reshape(bt, nq * ATT_HEADS, ATT_KV)
    qi2 = qi.astype(BF16).reshape(bt, nq * IDX_HEADS, IDX_DH)
    wrep = jnp.broadcast_to(wi.reshape(bt, nq * IDX_HEADS, 1), (bt, nq * IDX_HEADS, LANES))
    pad = lambda t: jnp.pad(t.reshape(bt, nq, -1), ((0, 0), (0, NEW_PAD - nq), (0, 0)))
    upper = jnp.triu(jnp.ones((LANES, LANES), BF16), 1)
    per_b = lambda shape: pl.BlockSpec((1,) + shape, lambda b, p, pt: (b, 0, 0))
    assert n_pages % PAGES_PER_STEP == 0

    def pages(width):
        return [pl.BlockSpec((1, 1, PAGE_SIZE, width), lambda b, p, pt, s=s: (j, pt[b, p * PAGES_PER_STEP + s], 0, 0))
                for s in range(PAGES_PER_STEP)]

    o_all = pl.pallas_call(
        functools.partial(_dsa_sample_kernel, topk, nq),
        grid_spec=pltpu.PrefetchScalarGridSpec(
            num_scalar_prefetch=1,
            grid=(bt, n_pages // PAGES_PER_STEP),
            in_specs=[per_b((nq * ATT_HEADS, ATT_KV)), per_b((nq * IDX_HEADS, IDX_DH)), per_b((nq * IDX_HEADS, LANES)),
                      per_b((NEW_PAD, ATT_KV)), per_b((NEW_PAD, ATT_KV)), per_b((NEW_PAD, IDX_DH))]
            + pages(ATT_KV) + pages(ATT_KV) + pages(IDX_DH)
            + [pl.BlockSpec((LANES, LANES), lambda b, p, pt: (0, 0))],
            out_specs=per_b((nq * ATT_HEADS, ATT_KV)),
            scratch_shapes=[pltpu.VMEM((n_keys, ATT_KV), BF16), pltpu.VMEM((n_keys, ATT_KV), BF16),
                            pltpu.VMEM((n_keys, IDX_DH), BF16)]),
        out_shape=jax.ShapeDtypeStruct((bt, nq * ATT_HEADS, ATT_KV), F32),
        compiler_params=pltpu.CompilerParams(dimension_semantics=("arbitrary", "arbitrary"),
                                             vmem_limit_bytes=VMEM_LIMIT_BYTES),
        name="dsa_sample_attention",
    )(page_table, qbd, qi2, wrep, pad(k_new), pad(v_new), pad(ki_new),
      *([cache_k.reshape(cache_k.shape[0], n_pool, PAGE_SIZE, ATT_KV)] * PAGES_PER_STEP),
      *([cache_v.reshape(cache_v.shape[0], n_pool, PAGE_SIZE, ATT_KV)] * PAGES_PER_STEP),
      *([cache_kidx] * PAGES_PER_STEP), upper)
    o6 = o_all.reshape(bt, nq, ATT_KV_HEADS, G_PER_KV, ATT_KV_HEADS, ATT_DH)
    o = jnp.stack([o6[:, :, n, :, n, :] for n in range(ATT_KV_HEADS)], axis=2)
    return o.reshape(bt, nq, ATT_Q)


def rms_norm(x, eps=EPS):
    return x * lax.rsqrt(jnp.mean(x * x, axis=-1, keepdims=True) + eps)


def layer_norm(x, g, b, eps):
    mu = jnp.mean(x, axis=-1, keepdims=True)
    var = jnp.mean(jnp.square(x - mu), axis=-1, keepdims=True)
    return (x - mu) * lax.rsqrt(var + eps) * g + b


def rotary(x, pos):
    half = x.shape[-1] // 2
    inv_freq = jnp.power(ROPE_THETA, -jnp.arange(half, dtype=F32) / half)
    ang = pos.astype(F32)[:, None] * inv_freq[None, :]
    cos = jnp.cos(ang)[None, :, None, :]
    sin = jnp.sin(ang)[None, :, None, :]
    x1, x2 = x[..., :half], x[..., half:]
    return jnp.concatenate([x1 * cos - x2 * sin, x2 * cos + x1 * sin], axis=-1)


def gla_chunked(q, k, v, log_a, s0):
    bt, L, H, _ = q.shape
    dv = v.shape[-1]
    C = min(GLA_CHUNK, L)
    n = -(-L // C)
    pad = n * C - L

    def prep(t):
        t = jnp.pad(t, ((0, 0), (0, pad), (0, 0), (0, 0)))
        return t.reshape(bt, n, C, H, t.shape[-1]).transpose(1, 0, 3, 2, 4)

    qc, kc, vc, gc = prep(q), prep(k), prep(v), prep(log_a)
    bcum = jnp.cumsum(gc, axis=3)
    blast = bcum[:, :, :, -1:, :]
    q_e = qc * jnp.exp(bcum)
    k_e = kc * jnp.exp(-bcum)
    k_end = kc * jnp.exp(blast - bcum)
    causal = jnp.tril(jnp.ones((C, C), dtype=bool))
    att = jnp.where(causal, jnp.einsum('nbhtd,nbhsd->nbhts', q_e, k_e), 0.0)
    o_intra = jnp.einsum('nbhts,nbhsv->nbhtv', att, vc)
    decay = jnp.exp(blast[:, :, :, 0, :])
    kv = jnp.einsum('nbhsd,nbhsv->nbhdv', k_end, vc)

    def step(s, inp):
        dec, kv_i = inp
        return dec[..., None] * s + kv_i, s

    s_fin, starts = s0, []
    for i in range(n):
        s_fin, s_i = step(s_fin, (decay[i], kv[i]))
        starts.append(s_i)
    s_start = jnp.stack(starts)
    o = o_intra + jnp.einsum('nbhtd,nbhdv->nbhtv', q_e, s_start)
    o = o.transpose(1, 0, 3, 2, 4).reshape(bt, n * C, H, dv)[:, :L]
    return o, s_fin


def rwkv7_scan(r, w, k, v, a, b, s0):
    xs = tuple(t.transpose(1, 0, 2, 3) for t in (r, w, k, v, a, b))

    def step(s, inp):
        r_t, w_t, k_t, v_t, a_t, b_t = inp
        sa = jnp.einsum('bhvk,bhk->bhv', s, a_t)
        s = s * w_t[:, :, None, :] + sa[..., None] * b_t[:, :, None, :] + v_t[..., None] * k_t[:, :, None, :]
        return s, jnp.einsum('bhvk,bhk->bhv', s, r_t)

    s, ys = s0, []
    for t in range(r.shape[1]):
        s, y_t = step(s, tuple(x[t] for x in xs))
        ys.append(y_t)
    return jnp.stack(ys, axis=1), s


EVEN_SPLIT = (GLA_QK, GLA_QK, GLA_V, GLA_V, GLA_GATE_RANK, 3 * RW_W + RW_W_LORA + RW_A_LORA + RW_G_LORA)
ODD_SPLIT = (ATT_Q, ATT_KV, ATT_KV, IDX_HEADS * IDX_DH, IDX_DH, IDX_HEADS)


def even_mixer(parts, bt, L, shift_prev, s_gla, s_rw, j, P):
    q_a, k_a, v_a, r_a, g_low, pb_raw = [t.reshape(bt, L, -1) for t in parts]
    ga = lambda t, d: t.reshape(bt, L, GLA_HEADS, d)
    log_alpha = jax.nn.log_sigmoid(g_low @ P['gla_w_gate'][j] + P['gla_b_gate'][j]) / GLA_TAU
    gla = gla_chunked_pallas if L % GLA_CHUNK == 0 else gla_chunked
    o_a, s_gla_new = gla(ga(q_a, GLA_DK) * GLA_DK ** -0.5, ga(k_a, GLA_DK), ga(v_a, GLA_DV),
                         ga(log_alpha, GLA_DK), s_gla)
    prev = jnp.concatenate([shift_prev[:, None, :], pb_raw[:, :-1]], axis=1)
    pb = pb_raw + (prev - pb_raw) * P['rw_mu'][j]
    o1 = 3 * RW_W + RW_W_LORA
    r_b, k_b, v_b, w_low, a_low, gt_low = jnp.split(pb, [RW_W, 2 * RW_W, 3 * RW_W, o1, o1 + RW_A_LORA], axis=-1)
    w_log = -jax.nn.softplus(-(P['rw_w0'][j] + jnp.tanh(w_low) @ P['rw_w_decay'][j])) - 0.5
    log_decay = -jnp.exp(w_log)
    iclr = jax.nn.sigmoid(P['rw_a0'][j] + a_low @ P['rw_w_iclr'][j])
    gate = jax.nn.sigmoid(gt_low) @ P['rw_w_gate'][j]
    hb = lambda t: t.reshape(bt, L, RW_HEADS, RW_DH)
    kk = hb(k_b * P['rw_k_k'][j])
    kk = kk / jnp.maximum(jnp.sqrt(jnp.sum(kk * kk, axis=-1, keepdims=True)), 1e-12)
    k_b = k_b * (1 + (iclr - 1) * P['rw_k_a'][j])
    r_h, k_h, v_h, a_h = hb(r_b), hb(k_b), hb(v_b), hb(iclr)
    if L % RW_CHUNK == 0:
        y_b, s_rw_new = rwkv7_chunked(r_h, hb(log_decay), k_h, v_h, -kk, kk * a_h, s_rw)
    else:
        y_b, s_rw_new = rwkv7_scan(r_h, hb(jnp.exp(log_decay)), k_h, v_h, -kk, kk * a_h, s_rw)
    flat = lambda t: t.reshape(bt * L, -1)
    return [flat(t) for t in (o_a, r_a, y_b, r_b, k_b, v_b, gate)], pb_raw[:, -1], s_gla_new, s_rw_new


def odd_qkv(parts, bt, L, pos, j, P):
    q, k, v, qi, ki, wi = [t.reshape(bt, L, -1) for t in parts]
    q = rotary(rms_norm(q.reshape(bt, L, ATT_HEADS, ATT_DH)) * P['q_norm_g'][j], pos)
    k = rotary(rms_norm(k.reshape(bt, L, ATT_KV_HEADS, ATT_DH)) * P['k_norm_g'][j], pos)
    v = v.reshape(bt, L, ATT_KV_HEADS, ATT_DH)
    qi = rotary(qi.reshape(bt, L, IDX_HEADS, IDX_DH), pos)
    ki = rotary(layer_norm(ki, P['kidx_ln_g'][j], P['kidx_ln_b'][j], EPS)[:, :, None, :], pos)[:, :, 0, :]
    wi = wi * (IDX_HEADS * IDX_DH) ** -0.5
    return q, k, v, qi, ki, wi


def trunk(x, mods, pos, shift0, gla0, rw0, attend, P):
    bt, L, D = x.shape
    x2 = x.reshape(bt * L, D)
    glas, rws, shifts, ks, vs, kis = [], [], [], [], [], []
    for layer, mod in enumerate(mods):
        sh1, sc1, g1, sh2, sc2, g2 = jnp.split(mod, 6, axis=-1)
        j = layer // 2
        if layer % 2 == 0:
            parts = mod_mm(x2, sh1, sc1, P['w_in_even'][j], EVEN_SPLIT)
            mix_ops, shf, sg, sr = even_mixer(parts, bt, L, shift0[j], gla0[j], rw0[j], j, P)
            glas.append(sg)
            rws.append(sr)
            shifts.append(shf)
            x2 = even_out(*mix_ops, P['gla_norm_g'][j], P['rw_ln_g'][j], P['rw_ln_b'][j], P['rw_r_k'][j],
                          P['w_out_even'][j], x2, g1)
            x2 = ffn_block(x2, sh2, sc2, g2, None, P['ffn_w_gate'][j], P['ffn_w_up'][j], P['ffn_w_down'][j])
        else:
            parts = mod_mm(x2, sh1, sc1, P['w_in_odd'][j], ODD_SPLIT)
            q, k, v, qi, ki, wi = odd_qkv(parts, bt, L, pos, j, P)
            ks.append(k)
            vs.append(v)
            kis.append(ki)
            x2 = mm_res(attend(j, q, k, v, qi, ki, wi).reshape(bt * L, -1), P['w_out_odd'][j], x2, g1)
            x2 = ffn_block(x2, sh2, sc2, g2, P['moe_router'][j], P['moe_w_gate'][j], P['moe_w_up'][j],
                           P['moe_w_down'][j])
    return (x2.reshape(bt, L, D), jnp.stack(glas), jnp.stack(rws), jnp.stack(shifts), jnp.stack(ks), jnp.stack(vs),
            jnp.stack(kis))


def kernel(x_prompt, x_sample, state_gla, state_rwkv, state_shift, cache_k, cache_v, cache_kidx, page_table, c_prompt, c_sample, w_ada, b_ada, w_in_even, gla_w_gate, gla_b_gate, gla_norm_g, rw_mu, rw_w0, rw_w_decay, rw_a0, rw_w_iclr, rw_w_gate, rw_k_k, rw_k_a, rw_r_k, rw_ln_g, rw_ln_b, w_out_even, w_in_odd, q_norm_g, k_norm_g, kidx_ln_g, kidx_ln_b, w_out_odd, ffn_w_gate, ffn_w_up, ffn_w_down, moe_router, moe_w_gate, moe_w_up, moe_w_down):
    P = dict(w_ada=w_ada, b_ada=b_ada, w_in_even=w_in_even, gla_w_gate=gla_w_gate, gla_b_gate=gla_b_gate,
             gla_norm_g=gla_norm_g, rw_mu=rw_mu, rw_w0=rw_w0, rw_w_decay=rw_w_decay, rw_a0=rw_a0,
             rw_w_iclr=rw_w_iclr, rw_w_gate=rw_w_gate, rw_k_k=rw_k_k, rw_k_a=rw_k_a, rw_r_k=rw_r_k,
             rw_ln_g=rw_ln_g, rw_ln_b=rw_ln_b, w_out_even=w_out_even, w_in_odd=w_in_odd,
             q_norm_g=q_norm_g, k_norm_g=k_norm_g, kidx_ln_g=kidx_ln_g, kidx_ln_b=kidx_ln_b,
             w_out_odd=w_out_odd, ffn_w_gate=ffn_w_gate, ffn_w_up=ffn_w_up, ffn_w_down=ffn_w_down,
             moe_router=moe_router, moe_w_gate=moe_w_gate, moe_w_up=moe_w_up, moe_w_down=moe_w_down)
    n_even = state_gla.shape[0]
    b_p, seq = x_prompt.shape[0], x_prompt.shape[1]
    zero_gla = jnp.zeros((n_even, b_p) + state_gla.shape[2:], state_gla.dtype)
    zero_rw = jnp.zeros((n_even, b_p) + state_rwkv.shape[2:], state_rwkv.dtype)
    zero_shift = jnp.zeros((n_even, b_p) + state_shift.shape[2:], state_shift.dtype)
    pos_p = jnp.arange(seq, dtype=jnp.int32)
    past = page_table.shape[1] * PAGE_SIZE
    pos_s = past + jnp.arange(x_sample.shape[1], dtype=jnp.int32)

    def prompt_attend(j, q, k, v, qi, ki, wi):
        return dsa_prompt_attention(q, k, v, qi, ki, wi)

    def sample_attend(j, q, k, v, qi, ki, wi):
        return dsa_sample_attention(q, k, v, qi, ki, wi, cache_k, cache_v, cache_kidx, page_table, j)

    silu_c = jax.nn.silu(jnp.concatenate([c_prompt, c_sample], axis=0))
    mods = [mm(silu_c, w_ada[layer]) + b_ada[layer] for layer in range(w_ada.shape[0])]
    mods_p = [m[:b_p] for m in mods]
    mods_s = [m[b_p:] for m in mods]

    y_prompt, p_gla, p_rw, p_shift, p_k, p_v, p_kidx = trunk(
        x_prompt, mods_p, pos_p, zero_shift, zero_gla, zero_rw, prompt_attend, P)
    y_sample, s_gla, s_rw, s_shift, s_k, s_v, s_kidx = trunk(
        x_sample, mods_s, pos_s, state_shift, state_gla, state_rwkv, sample_attend, P)
    return (y_prompt, y_sample, p_gla, p_rw, p_shift, p_k, p_v, p_kidx, s_gla, s_rw, s_shift, s_k, s_v, s_kidx)
```
